```python
import math
import jax, jax.numpy as jnp
from jax import lax
import numpy as np

D_MODEL = 1024
BATCH = 8
SEQ = 8192
DEPTH = 4
DEC_BATCH = 32
DEC_SEQ = 16
PAST_LEN = 2048

CHUNK = 64
N_EVEN = (DEPTH + 1) // 2
N_ODD = DEPTH // 2
NORM_EPS = 1e-6
D_A = D_MODEL // 2
DH_A = 64
H_A = D_A // DH_A
LORA_W = 64
LORA_A = 64
LORA_G = 128
A_PROJ = 3 * D_A + LORA_W + LORA_A + LORA_G
GN_EPS = 64e-5
D_B = D_MODEL // 2
DH_B = 64
H_B = D_B // DH_B
H_I = 4
D_IDX = 64
TOPK_MAX = 256
Q_BLOCK = 128
B_PROJ = 3 * D_B + H_I * D_IDX + D_IDX + H_I
IN_EVEN = A_PROJ + B_PROJ
REL_BUCKETS = 32
REL_MAX_DIST = 128
D_CONV = D_MODEL
CONV_W = 31
PEER_HEADS = 8
PEER_DK = 256
N_KEYS = 128
N_EXPERTS = N_KEYS * N_KEYS
TOPK_HALF = 16
PEER_TOPK = 16
PEER_BLOCK = 512

kernel_name = 'hybrid_stream_rwkv7_dsa_conformer_peer'


def _rmsnorm(x, g):
    xf = x.astype(jnp.float32)
    y = xf * lax.rsqrt(jnp.mean(xf * xf, axis=-1, keepdims=True) + NORM_EPS)
    return (y * g).astype(x.dtype)


def _t5_bucket(rel):
    nb = REL_BUCKETS // 2
    ret = jnp.where(rel > 0, nb, 0)
    n = jnp.abs(rel)
    max_exact = nb // 2
    nf = jnp.maximum(n, 1).astype(jnp.float32)
    large = max_exact + (jnp.log(nf / max_exact) / math.log(REL_MAX_DIST / max_exact)
                         * (nb - max_exact)).astype(jnp.int32)
    large = jnp.minimum(large, nb - 1)
    return ret + jnp.where(n < max_exact, n, large)


def _wkv_scan(r, w, k, v, kk, a, s0):
    xs = tuple(jnp.moveaxis(t.astype(jnp.float32), 1, 0) for t in (r, w, k, v, kk, a))

    def step(S, inp):
        r_t, w_t, k_t, v_t, kk_t, a_t = inp
        sa = jnp.einsum('bhvk,bhk->bhv', S, -kk_t)
        S = (S * w_t[:, :, None, :] + sa[..., None] * (kk_t * a_t)[:, :, None, :]
             + v_t[..., None] * k_t[:, :, None, :])
        return S, jnp.einsum('bhvk,bhk->bhv', S, r_t)

    S, o = lax.scan(step, s0.astype(jnp.float32), xs)
    return jnp.moveaxis(o, 0, 1), S.astype(s0.dtype)


def _dsa_attend(q, k_all, v_all, qi, wi, ki_all, q_pos, k_pos, rel_bias, topk):
    Bx, T, H, Dh = q.shape
    qb = Q_BLOCK if T % Q_BLOCK == 0 else T
    nblk = T // qb

    def to_blocks(t):
        return jnp.moveaxis(t.reshape(Bx, nblk, qb, *t.shape[2:]), 1, 0)

    k_chunk = k_pos // CHUNK

    def block(args):
        qx, qix, wix, pos = args
        sc = jax.nn.relu(jnp.einsum('bqhd,bsd->bqhs', qix, ki_all).astype(jnp.float32) * D_IDX ** -0.5)
        idx = jnp.einsum('bqhs,bqh->bqs', sc, wix.astype(jnp.float32) * H_I ** -0.5)
        vis = k_chunk[None, :] <= (pos // CHUNK)[:, None]
        idx = jnp.where(vis[None], idx, -jnp.inf)
        top_s, top_i = lax.top_k(idx, topk)
        k_sel = jax.vmap(lambda kb, ib: kb[ib])(k_all, top_i)
        v_sel = jax.vmap(lambda vb, ib: vb[ib])(v_all, top_i)
        logit = jnp.einsum('bqhd,bqkhd->bqhk', qx, k_sel).astype(jnp.float32) * Dh ** -0.5
        rel = k_pos[top_i] - pos[None, :, None]
        bias = rel_bias[_t5_bucket(rel)]
        logit = logit + jnp.moveaxis(bias, -1, 2).astype(jnp.float32)
        logit = jnp.where(jnp.isfinite(top_s)[:, :, None, :], logit, -jnp.inf)
        p = jax.nn.softmax(logit, axis=-1)
        return jnp.einsum('bqhk,bqkhd->bqhd', p.astype(v_sel.dtype), v_sel)

    out = lax.map(block, (to_blocks(q), to_blocks(qi), to_blocks(wi), q_pos.reshape(nblk, qb)))
    return jnp.moveaxis(out, 0, 1).reshape(Bx, T, H, Dh)


def _even_mixer(x, shift_prev, wkv_prev, k_past, v_past, kidx_past,
                g_mix, w_in, mu, w0, w_du, a0, w_au, w_gu, k_k, k_a, r_k, gn_w, gn_b,
                qn_g, kn_g, rel_bias, w_out):
    Bx, T, _ = x.shape
    f32 = jnp.float32
    h = _rmsnorm(x, g_mix)
    p = h @ w_in
    pa, pb = p[..., :A_PROJ], p[..., A_PROJ:]
    pa_prev = jnp.concatenate([shift_prev[:, None, :].astype(pa.dtype), pa[:, :-1]], axis=1)
    xm = (pa + (pa_prev - pa) * mu).astype(f32)
    r, k, v, dw, da, dg = jnp.split(
        xm, [D_A, 2 * D_A, 3 * D_A, 3 * D_A + LORA_W, 3 * D_A + LORA_W + LORA_A], axis=-1)
    w_log = -jax.nn.softplus(-(w0 + jnp.tanh(dw) @ w_du)) - 0.5
    decay = jnp.exp(-jnp.exp(w_log))
    a = jax.nn.sigmoid(a0 + da @ w_au)
    gate = jax.nn.sigmoid(dg) @ w_gu
    kk = (k * k_k).reshape(Bx, T, H_A, DH_A)
    kk = kk / jnp.maximum(jnp.sqrt(jnp.sum(kk * kk, axis=-1, keepdims=True)), 1e-12)
    k = k * (1.0 + (a - 1.0) * k_a)
    r, k, v, decay, a = (t.reshape(Bx, T, H_A, DH_A) for t in (r, k, v, decay, a))
    o, wkv_new = _wkv_scan(r, decay, k, v, kk, a, wkv_prev)
    mo = jnp.mean(o, axis=-1, keepdims=True)
    vo = jnp.mean(jnp.square(o - mo), axis=-1, keepdims=True)
    o = ((o - mo) * lax.rsqrt(vo + GN_EPS)).reshape(Bx, T, D_A) * gn_w + gn_b
    o = o + (jnp.sum(r * k * r_k, axis=-1, keepdims=True) * v).reshape(Bx, T, D_A)
    o_a = (o * gate).astype(x.dtype)
    q, kb, vb, qi, ki, wi = jnp.split(
        pb, [D_B, 2 * D_B, 3 * D_B, 3 * D_B + H_I * D_IDX, 3 * D_B + H_I * D_IDX + D_IDX], axis=-1)
    q = _rmsnorm(q.reshape(Bx, T, H_B, DH_B), qn_g)
    kb = _rmsnorm(kb.reshape(Bx, T, H_B, DH_B), kn_g)
    vb = vb.reshape(Bx, T, H_B, DH_B)
    qi = qi.reshape(Bx, T, H_I, D_IDX)
    offset = k_past.shape[1]
    L = offset + T
    topk = min(TOPK_MAX, L // 4)
    k_all = jnp.concatenate([k_past.astype(kb.dtype), kb], axis=1)
    v_all = jnp.concatenate([v_past.astype(vb.dtype), vb], axis=1)
    ki_all = jnp.concatenate([kidx_past.astype(ki.dtype), ki], axis=1)
    q_pos = offset + jnp.arange(T, dtype=jnp.int32)
    k_pos = jnp.arange(L, dtype=jnp.int32)
    o_b = _dsa_attend(q, k_all, v_all, qi, wi, ki_all, q_pos, k_pos, rel_bias, topk)
    out = jnp.concatenate([o_a, o_b.reshape(Bx, T, D_B).astype(x.dtype)], axis=-1) @ w_out
    return out, pa[:, -1], wkv_new, kb, vb, ki


def _conv_mixer(x, conv_prev, g_mix, w1, b1, w_dw, b_dw, ln_g, ln_b, w2, b2):
    h = _rmsnorm(x, g_mix)
    u = h @ w1 + b1
    glu = u[..., :D_CONV] * jax.nn.sigmoid(u[..., D_CONV:])
    padded = jnp.concatenate([conv_prev.astype(glu.dtype), glu], axis=1)
    y = lax.conv_general_dilated(padded, w_dw[:, None, :].astype(glu.dtype), (1,), 'VALID',
                                 dimension_numbers=('NWC', 'WIO', 'NWC'),
                                 feature_group_count=D_CONV) + b_dw
    yf = y.astype(jnp.float32)
    m = jnp.mean(yf, axis=-1, keepdims=True)
    var = jnp.mean(jnp.square(yf - m), axis=-1, keepdims=True)
    yn = jax.nn.silu((yf - m) * lax.rsqrt(var + 1e-5) * ln_g + ln_b)
    out = yn.astype(x.dtype) @ w2 + b2
    return out, padded[:, -(CONV_W - 1):]


def _peer(x, g, wq, subkeys, u_tab, v_tab):
    Bx, T, D = x.shape
    n = Bx * T
    h = _rmsnorm(x, g).reshape(n, D)
    blk = min(PEER_BLOCK, n)
    nb = -(-n // blk)
    h = jnp.pad(h, ((0, nb * blk - n), (0, 0)))

    def block(hx):
        q = (hx @ wq).reshape(blk, PEER_HEADS, 2, PEER_DK // 2)
        s = jnp.einsum('nhpd,hpkd->nhpk', q, subkeys).astype(jnp.float32)
        sv, si = lax.top_k(s, TOPK_HALF)
        cand = (sv[:, :, 0, :, None] + sv[:, :, 1, None, :]).reshape(blk, PEER_HEADS, TOPK_HALF * TOPK_HALF)
        cs, ci = lax.top_k(cand, PEER_TOPK)
        i1 = jnp.take_along_axis(si[:, :, 0], ci // TOPK_HALF, axis=-1)
        i2 = jnp.take_along_axis(si[:, :, 1], ci % TOPK_HALF, axis=-1)
        e = i1 * N_KEYS + i2
        gates = jax.nn.softmax(cs, axis=-1)
        act = jax.nn.gelu(jnp.einsum('nhed,nd->nhe', u_tab[e], hx).astype(jnp.float32), approximate=False)
        return jnp.einsum('nhe,nhed->nd', (gates * act).astype(hx.dtype), v_tab[e])

    out = lax.map(block, h.reshape(nb, blk, D))
    return out.reshape(nb * blk, D)[:n].reshape(Bx, T, D)


def setup_inputs(seed: int = 0) -> dict:
    key = jax.random.key(seed)
    ks = iter(jax.random.split(key, 48))

    def nrm(shape, scale):
        return jax.random.normal(next(ks), shape, jnp.float32) * scale

    def gain(shape):
        return 1.0 + nrm(shape, 0.05)

    return {
        'x_prompt': nrm((BATCH, SEQ, D_MODEL), 1.0),
        'x_sample': nrm((DEC_BATCH, DEC_SEQ, D_MODEL), 1.0),
        'state_wkv': nrm((N_EVEN, DEC_BATCH, H_A, DH_A, DH_A), 0.1),
        'state_shift': nrm((N_EVEN, DEC_BATCH, A_PROJ), 1.0),
        'cache_k': nrm((N_EVEN, DEC_BATCH, PAST_LEN, H_B, DH_B), 1.0),
        'cache_v': nrm((N_EVEN, DEC_BATCH, PAST_LEN, H_B, DH_B), 1.0),
        'cache_kidx': nrm((N_EVEN, DEC_BATCH, PAST_LEN, D_IDX), 1.0),
        'state_conv': nrm((N_ODD, DEC_BATCH, CONV_W - 1, D_CONV), 0.5),
        'norm_mix': gain((DEPTH, D_MODEL)),
        'norm_ffn': gain((DEPTH, D_MODEL)),
        'w_in': nrm((N_EVEN, D_MODEL, IN_EVEN), D_MODEL ** -0.5),
        'mu_shift': jax.random.uniform(next(ks), (N_EVEN, A_PROJ), jnp.float32),
        'w0': jax.random.uniform(next(ks), (N_EVEN, D_A), jnp.float32, -4.0, 0.0),
        'w_decay_up': nrm((N_EVEN, LORA_W, D_A), 0.1),
        'a0': nrm((N_EVEN, D_A), 0.5),
        'w_iclr_up': nrm((N_EVEN, LORA_A, D_A), 0.1),
        'w_gate_up': nrm((N_EVEN, LORA_G, D_A), LORA_G ** -0.5),
        'k_k': 0.85 + nrm((N_EVEN, D_A), 0.05),
        'k_a': gain((N_EVEN, D_A)),
        'r_k': nrm((N_EVEN, H_A, DH_A), 0.1),
        'gn_w': gain((N_EVEN, D_A)),
        'gn_b': nrm((N_EVEN, D_A), 0.01),
        'qn_g': gain((N_EVEN, DH_B)),
        'kn_g': gain((N_EVEN, DH_B)),
        'rel_bias': nrm((REL_BUCKETS, H_B), 0.1),
        'w_out': nrm((N_EVEN, D_A + D_B, D_MODEL), (D_A + D_B) ** -0.5),
        'conv_w1': nrm((N_ODD, D_MODEL, 2 * D_CONV), D_MODEL ** -0.5),
        'conv_b1': nrm((N_ODD, 2 * D_CONV), 0.01),
        'conv_dw': nrm((N_ODD, CONV_W, D_CONV), CONV_W ** -0.5),
        'conv_bdw': nrm((N_ODD, D_CONV), 0.01),
        'conv_ln_g': gain((N_ODD, D_CONV)),
        'conv_ln_b': nrm((N_ODD, D_CONV), 0.01),
        'conv_w2': nrm((N_ODD, D_CONV, D_MODEL), D_CONV ** -0.5),
        'conv_b2': nrm((N_ODD, D_MODEL), 0.01),
        'peer_wq': nrm((DEPTH, D_MODEL, PEER_HEADS * PEER_DK), D_MODEL ** -0.5),
        'peer_subkeys': nrm((DEPTH, PEER_HEADS, 2, N_KEYS, PEER_DK // 2), (PEER_DK // 2) ** -0.5),
        'peer_u': nrm((DEPTH, N_EXPERTS, D_MODEL), D_MODEL ** -0.5),
        'peer_v': nrm((DEPTH, N_EXPERTS, D_MODEL), (PEER_HEADS * PEER_TOPK) ** -0.5),
    }


def reference(x_prompt, x_sample, state_wkv, state_shift, cache_k, cache_v, cache_kidx, state_conv,
              norm_mix, norm_ffn, w_in, mu_shift, w0, w_decay_up, a0, w_iclr_up, w_gate_up,
              k_k, k_a, r_k, gn_w, gn_b, qn_g, kn_g, rel_bias, w_out,
              conv_w1, conv_b1, conv_dw, conv_bdw, conv_ln_g, conv_ln_b, conv_w2, conv_b2,
              peer_wq, peer_subkeys, peer_u, peer_v):
    xp, xs = x_prompt, x_sample
    Bp = xp.shape[0]
    dt = xp.dtype
    wkv_p, shift_p, k_p, v_p, kidx_p, conv_p = [], [], [], [], [], []
    wkv_s, shift_s, k_s, v_s, kidx_s, conv_s = [], [], [], [], [], []
    for li in range(DEPTH):
        if li % 2 == 0:
            e = li // 2
            prm = (norm_mix[li], w_in[e], mu_shift[e], w0[e], w_decay_up[e], a0[e], w_iclr_up[e],
                   w_gate_up[e], k_k[e], k_a[e], r_k[e], gn_w[e], gn_b[e], qn_g[e], kn_g[e],
                   rel_bias, w_out[e])
            mp, sh, wk, kb, vb, ki = _even_mixer(
                xp, jnp.zeros((Bp, A_PROJ), dt), jnp.zeros((Bp, H_A, DH_A, DH_A), dt),
                jnp.zeros((Bp, 0, H_B, DH_B), dt), jnp.zeros((Bp, 0, H_B, DH_B), dt),
                jnp.zeros((Bp, 0, D_IDX), dt), *prm)
            wkv_p.append(wk); shift_p.append(sh); k_p.append(kb); v_p.append(vb); kidx_p.append(ki)
            ms, sh, wk, kb, vb, ki = _even_mixer(
                xs, state_shift[e], state_wkv[e], cache_k[e], cache_v[e], cache_kidx[e], *prm)
            wkv_s.append(wk); shift_s.append(sh); k_s.append(kb); v_s.append(vb); kidx_s.append(ki)
        else:
            o = li // 2
            prm = (norm_mix[li], conv_w1[o], conv_b1[o], conv_dw[o], conv_bdw[o],
                   conv_ln_g[o], conv_ln_b[o], conv_w2[o], conv_b2[o])
            mp, cp = _conv_mixer(xp, jnp.zeros((Bp, CONV_W - 1, D_CONV), dt), *prm)
            ms, cs = _conv_mixer(xs, state_conv[o], *prm)
            conv_p.append(cp); conv_s.append(cs)
        xp = xp + mp
        xs = xs + ms
        pprm = (norm_ffn[li], peer_wq[li], peer_subkeys[li], peer_u[li], peer_v[li])
        xp = xp + _peer(xp, *pprm)
        xs = xs + _peer(xs, *pprm)
    return (xp, xs,
            jnp.stack(wkv_p), jnp.stack(shift_p), jnp.stack(k_p), jnp.stack(v_p), jnp.stack(kidx_p), jnp.stack(conv_p),
            jnp.stack(wkv_s), jnp.stack(shift_s), jnp.stack(k_s), jnp.stack(v_s), jnp.stack(kidx_s), jnp.stack(conv_s))
```

```python
import functools
import math

import jax
import jax.numpy as jnp
from jax import lax
from jax.experimental import pallas as pl
from jax.experimental.pallas import tpu as pltpu

D_MODEL = 1024
DEPTH = 4
CHUNK = 64
NORM_EPS = 1e-6
D_A = D_MODEL // 2
DH_A = 64
H_A = D_A // DH_A
LORA_W = 64
LORA_A = 64
LORA_G = 128
A_PROJ = 3 * D_A + LORA_W + LORA_A + LORA_G
GN_EPS = 64e-5
D_B = D_MODEL // 2
DH_B = 64
H_B = D_B // DH_B
H_I = 4
D_IDX = 64
TOPK_MAX = 256
Q_BLOCK = 128
B_PROJ = 3 * D_B + H_I * D_IDX + D_IDX + H_I
REL_BUCKETS = 32
REL_MAX_DIST = 128
D_CONV = D_MODEL
CONV_W = 31
PEER_HEADS = 8
PEER_DK = 256
N_KEYS = 128
TOPK_HALF = 16
PEER_TOPK = 16
PEER_BLOCK = 512

LANES = 128
VMEM_LIMIT = 48 * 1024 * 1024

F32 = jnp.float32
BF16 = jnp.bfloat16


def _round_up(n, m):
    return -(-n // m) * m


def _col_tile(m):
    for t in (1024, 896, 768, 640, 512, 384, 256, 128):
        if m % t == 0:
            return t
    raise ValueError(m)


def _norm_linear_kernel(x_ref, g_ref, w_ref, b_ref, o_ref, h_scr):
    @pl.when(pl.program_id(1) == 0)
    def _():
        x = x_ref[...]
        ms = jnp.mean(x * x, axis=-1, keepdims=True)
        h_scr[...] = (x * lax.rsqrt(ms + NORM_EPS) * g_ref[...]).astype(BF16)

    o_ref[...] = jnp.dot(h_scr[...], w_ref[...], preferred_element_type=F32) + b_ref[...]


def norm_linear(x, g, w, b=None, row_tile=512):
    n, d = x.shape
    m = w.shape[1]
    mp = _round_up(m, 2 * LANES)
    wb = jnp.pad(w.astype(BF16), ((0, 0), (0, mp - m)))
    bb = jnp.zeros((1, mp), F32) if b is None else jnp.pad(b.astype(F32), (0, mp - m))[None]
    tn = _col_tile(mp)
    tm = min(row_tile, n)
    assert n % tm == 0
    out = pl.pallas_call(
        _norm_linear_kernel,
        grid=(n // tm, mp // tn),
        in_specs=[
            pl.BlockSpec((tm, d), lambda i, j: (i, 0)),
            pl.BlockSpec((1, d), lambda i, j: (0, 0)),
            pl.BlockSpec((d, tn), lambda i, j: (0, j)),
            pl.BlockSpec((1, tn), lambda i, j: (0, j)),
        ],
        out_specs=pl.BlockSpec((tm, tn), lambda i, j: (i, j)),
        out_shape=jax.ShapeDtypeStruct((n, mp), F32),
        scratch_shapes=[pltpu.VMEM((tm, d), BF16)],
        compiler_params=pltpu.CompilerParams(
            dimension_semantics=("arbitrary", "arbitrary"), vmem_limit_bytes=VMEM_LIMIT),
    )(x, g[None].astype(F32), wb, bb)
    return out[:, :m] if mp != m else out


def _linear_residual_kernel(x_ref, w_ref, b_ref, r_ref, o_ref):
    o_ref[...] = (r_ref[...] + b_ref[...]
                  + jnp.dot(x_ref[...].astype(BF16), w_ref[...], preferred_element_type=F32))


def linear_residual(x, w, b, res, row_tile=512):
    n, k = x.shape
    m = w.shape[1]
    tm = min(row_tile, n)
    assert n % tm == 0 and m % LANES == 0
    bb = jnp.zeros((1, m), F32) if b is None else b.astype(F32)[None]
    return pl.pallas_call(
        _linear_residual_kernel,
        grid=(n // tm,),
        in_specs=[
            pl.BlockSpec((tm, k), lambda i: (i, 0)),
            pl.BlockSpec((k, m), lambda i: (0, 0)),
            pl.BlockSpec((1, m), lambda i: (0, 0)),
            pl.BlockSpec((tm, m), lambda i: (i, 0)),
        ],
        out_specs=pl.BlockSpec((tm, m), lambda i: (i, 0)),
        out_shape=jax.ShapeDtypeStruct((n, m), F32),
        compiler_params=pltpu.CompilerParams(
            dimension_semantics=("arbitrary",), vmem_limit_bytes=VMEM_LIMIT),
    )(x, w.astype(BF16), bb, res)


def _rmsnorm(x, g):
    xf = x.astype(F32)
    y = xf * lax.rsqrt(jnp.mean(xf * xf, axis=-1, keepdims=True) + NORM_EPS)
    return (y * g).astype(x.dtype)


def _t5_bucket(rel):
    nb = REL_BUCKETS // 2
    ret = jnp.where(rel > 0, nb, 0)
    n = jnp.abs(rel)
    max_exact = nb // 2
    nf = jnp.maximum(n, 1).astype(F32)
    large = max_exact + (jnp.log(nf / max_exact) / math.log(REL_MAX_DIST / max_exact)
                         * (nb - max_exact)).astype(jnp.int32)
    large = jnp.minimum(large, nb - 1)
    return ret + jnp.where(n < max_exact, n, large)


def _wkv_scan(r, w, k, v, kk, a, s0):
    xs = tuple(jnp.moveaxis(t.astype(F32), 1, 0) for t in (r, w, k, v, kk, a))

    def step(S, inp):
        r_t, w_t, k_t, v_t, kk_t, a_t = inp
        sa = jnp.einsum('bhvk,bhk->bhv', S, -kk_t)
        S = (S * w_t[:, :, None, :] + sa[..., None] * (kk_t * a_t)[:, :, None, :]
             + v_t[..., None] * k_t[:, :, None, :])
        return S, jnp.einsum('bhvk,bhk->bhv', S, r_t)

    S, o = lax.scan(step, s0.astype(F32), xs)
    return jnp.moveaxis(o, 0, 1), S.astype(s0.dtype)


def _dsa_attend(q, k_all, v_all, qi, wi, ki_all, q_pos, k_pos, rel_bias, topk):
    Bx, T, H, Dh = q.shape
    qb = Q_BLOCK if T % Q_BLOCK == 0 else T
    nblk = T // qb

    def to_blocks(t):
        return jnp.moveaxis(t.reshape(Bx, nblk, qb, *t.shape[2:]), 1, 0)

    k_chunk = k_pos // CHUNK

    def block(args):
        qx, qix, wix, pos = args
        sc = jax.nn.relu(jnp.einsum('bqhd,bsd->bqhs', qix, ki_all).astype(F32) * D_IDX ** -0.5)
        idx = jnp.einsum('bqhs,bqh->bqs', sc, wix.astype(F32) * H_I ** -0.5)
        vis = k_chunk[None, :] <= (pos // CHUNK)[:, None]
        idx = jnp.where(vis[None], idx, -jnp.inf)
        top_s, top_i = lax.top_k(idx, topk)
        k_sel = jax.vmap(lambda kb, ib: kb[ib])(k_all, top_i)
        v_sel = jax.vmap(lambda vb, ib: vb[ib])(v_all, top_i)
        logit = jnp.einsum('bqhd,bqkhd->bqhk', qx, k_sel).astype(F32) * Dh ** -0.5
        rel = k_pos[top_i] - pos[None, :, None]
        bias = rel_bias[_t5_bucket(rel)]
        logit = logit + jnp.moveaxis(bias, -1, 2).astype(F32)
        logit = jnp.where(jnp.isfinite(top_s)[:, :, None, :], logit, -jnp.inf)
        p = jax.nn.softmax(logit, axis=-1)
        return jnp.einsum('bqhk,bqkhd->bqhd', p.astype(v_sel.dtype), v_sel)

    out = lax.map(block, (to_blocks(q), to_blocks(qi), to_blocks(wi), q_pos.reshape(nblk, qb)))
    return jnp.moveaxis(out, 0, 1).reshape(Bx, T, H, Dh)


def _even_mixer(x, shift_prev, wkv_prev, k_past, v_past, kidx_past,
                g_mix, w_in, mu, w0, w_du, a0, w_au, w_gu, k_k, k_a, r_k, gn_w, gn_b,
                qn_g, kn_g, rel_bias, w_out):
    Bx, T, _ = x.shape
    p = norm_linear(x.reshape(Bx * T, D_MODEL), g_mix, w_in).reshape(Bx, T, -1)
    pa, pb = p[..., :A_PROJ], p[..., A_PROJ:]
    pa_prev = jnp.concatenate([shift_prev[:, None, :].astype(pa.dtype), pa[:, :-1]], axis=1)
    xm = (pa + (pa_prev - pa) * mu).astype(F32)
    r, k, v, dw, da, dg = jnp.split(
        xm, [D_A, 2 * D_A, 3 * D_A, 3 * D_A + LORA_W, 3 * D_A + LORA_W + LORA_A], axis=-1)
    w_log = -jax.nn.softplus(-(w0 + jnp.tanh(dw) @ w_du)) - 0.5
    decay = jnp.exp(-jnp.exp(w_log))
    a = jax.nn.sigmoid(a0 + da @ w_au)
    gate = jax.nn.sigmoid(dg) @ w_gu
    kk = (k * k_k).reshape(Bx, T, H_A, DH_A)
    kk = kk / jnp.maximum(jnp.sqrt(jnp.sum(kk * kk, axis=-1, keepdims=True)), 1e-12)
    k = k * (1.0 + (a - 1.0) * k_a)
    r, k, v, decay, a = (t.reshape(Bx, T, H_A, DH_A) for t in (r, k, v, decay, a))
    o, wkv_new = _wkv_scan(r, decay, k, v, kk, a, wkv_prev)
    mo = jnp.mean(o, axis=-1, keepdims=True)
    vo = jnp.mean(jnp.square(o - mo), axis=-1, keepdims=True)
    o = ((o - mo) * lax.rsqrt(vo + GN_EPS)).reshape(Bx, T, D_A) * gn_w + gn_b
    o = o + (jnp.sum(r * k * r_k, axis=-1, keepdims=True) * v).reshape(Bx, T, D_A)
    o_a = (o * gate).astype(x.dtype)
    q, kb, vb, qi, ki, wi = jnp.split(
        pb, [D_B, 2 * D_B, 3 * D_B, 3 * D_B + H_I * D_IDX, 3 * D_B + H_I * D_IDX + D_IDX], axis=-1)
    q = _rmsnorm(q.reshape(Bx, T, H_B, DH_B), qn_g)
    kb = _rmsnorm(kb.reshape(Bx, T, H_B, DH_B), kn_g)
    vb = vb.reshape(Bx, T, H_B, DH_B)
    qi = qi.reshape(Bx, T, H_I, D_IDX)
    offset = k_past.shape[1]
    L = offset + T
    topk = min(TOPK_MAX, L // 4)
    k_all = jnp.concatenate([k_past.astype(kb.dtype), kb], axis=1)
    v_all = jnp.concatenate([v_past.astype(vb.dtype), vb], axis=1)
    ki_all = jnp.concatenate([kidx_past.astype(ki.dtype), ki], axis=1)
    q_pos = offset + jnp.arange(T, dtype=jnp.int32)
    k_pos = jnp.arange(L, dtype=jnp.int32)
    o_b = _dsa_attend(q, k_all, v_all, qi, wi, ki_all, q_pos, k_pos, rel_bias, topk)
    cat = jnp.concatenate([o_a, o_b.reshape(Bx, T, D_B).astype(x.dtype)], axis=-1)
    x_new = linear_residual(cat.reshape(Bx * T, D_MODEL), w_out, None,
                            x.reshape(Bx * T, D_MODEL)).reshape(Bx, T, D_MODEL)
    return x_new, pa[:, -1], wkv_new, kb, vb, ki


def _conv_mixer(x, conv_prev, g_mix, w1, b1, w_dw, b_dw, ln_g, ln_b, w2, b2):
    Bx, T, _ = x.shape
    u = norm_linear(x.reshape(Bx * T, D_MODEL), g_mix, w1, b1).reshape(Bx, T, -1)
    glu = u[..., :D_CONV] * jax.nn.sigmoid(u[..., D_CONV:])
    padded = jnp.concatenate([conv_prev.astype(glu.dtype), glu], axis=1)
    y = lax.conv_general_dilated(padded, w_dw[:, None, :].astype(glu.dtype), (1,), 'VALID',
                                 dimension_numbers=('NWC', 'WIO', 'NWC'),
                                 feature_group_count=D_CONV) + b_dw
    yf = y.astype(F32)
    m = jnp.mean(yf, axis=-1, keepdims=True)
    var = jnp.mean(jnp.square(yf - m), axis=-1, keepdims=True)
    yn = jax.nn.silu((yf - m) * lax.rsqrt(var + 1e-5) * ln_g + ln_b)
    x_new = linear_residual(yn.reshape(Bx * T, D_CONV), w2, b2,
                            x.reshape(Bx * T, D_MODEL)).reshape(Bx, T, D_MODEL)
    return x_new, padded[:, -(CONV_W - 1):]


def _peer(x, g, wq, subkeys, u_tab, v_tab):
    Bx, T, D = x.shape
    n = Bx * T
    h = _rmsnorm(x, g).reshape(n, D)
    qall = norm_linear(x.reshape(n, D), g, wq)
    blk = min(PEER_BLOCK, n)
    nb = n // blk

    def block(args):
        hx, q = args
        q = q.reshape(blk, PEER_HEADS, 2, PEER_DK // 2)
        s = jnp.einsum('nhpd,hpkd->nhpk', q, subkeys).astype(F32)
        sv, si = lax.top_k(s, TOPK_HALF)
        cand = (sv[:, :, 0, :, None] + sv[:, :, 1, None, :]).reshape(blk, PEER_HEADS, TOPK_HALF * TOPK_HALF)
        cs, ci = lax.top_k(cand, PEER_TOPK)
        i1 = jnp.take_along_axis(si[:, :, 0], ci // TOPK_HALF, axis=-1)
        i2 = jnp.take_along_axis(si[:, :, 1], ci % TOPK_HALF, axis=-1)
        e = i1 * N_KEYS + i2
        gates = jax.nn.softmax(cs, axis=-1)
        act = jax.nn.gelu(jnp.einsum('nhed,nd->nhe', u_tab[e], hx).astype(F32), approximate=False)
        return jnp.einsum('nhe,nhed->nd', (gates * act).astype(hx.dtype), v_tab[e])

    out = lax.map(block, (h.reshape(nb, blk, D), qall.reshape(nb, blk, -1)))
    return x + out.reshape(Bx, T, D)


def kernel(x_prompt, x_sample, state_wkv, state_shift, cache_k, cache_v, cache_kidx, state_conv,
           norm_mix, norm_ffn, w_in, mu_shift, w0, w_decay_up, a0, w_iclr_up, w_gate_up,
           k_k, k_a, r_k, gn_w, gn_b, qn_g, kn_g, rel_bias, w_out,
           conv_w1, conv_b1, conv_dw, conv_bdw, conv_ln_g, conv_ln_b, conv_w2, conv_b2,
           peer_wq, peer_subkeys, peer_u, peer_v):
    xp, xs = x_prompt, x_sample
    Bp = xp.shape[0]
    dt = xp.dtype
    wkv_p, shift_p, k_p, v_p, kidx_p, conv_p = [], [], [], [], [], []
    wkv_s, shift_s, k_s, v_s, kidx_s, conv_s = [], [], [], [], [], []
    for li in range(DEPTH):
        if li % 2 == 0:
            e = li // 2
            prm = (norm_mix[li], w_in[e], mu_shift[e], w0[e], w_decay_up[e], a0[e], w_iclr_up[e],
                   w_gate_up[e], k_k[e], k_a[e], r_k[e], gn_w[e], gn_b[e], qn_g[e], kn_g[e],
                   rel_bias, w_out[e])
            xp, sh, wk, kb, vb, ki = _even_mixer(
                xp, jnp.zeros((Bp, A_PROJ), dt), jnp.zeros((Bp, H_A, DH_A, DH_A), dt),
                jnp.zeros((Bp, 0, H_B, DH_B), dt), jnp.zeros((Bp, 0, H_B, DH_B), dt),
                jnp.zeros((Bp, 0, D_IDX), dt), *prm)
            wkv_p.append(wk); shift_p.append(sh); k_p.append(kb); v_p.append(vb); kidx_p.append(ki)
            xs, sh, wk, kb, vb, ki = _even_mixer(
                xs, state_shift[e], state_wkv[e], cache_k[e], cache_v[e], cache_kidx[e], *prm)
            wkv_s.append(wk); shift_s.append(sh); k_s.append(kb); v_s.append(vb); kidx_s.append(ki)
        else:
            o = li // 2
            prm = (norm_mix[li], conv_w1[o], conv_b1[o], conv_dw[o], conv_bdw[o],
                   conv_ln_g[o], conv_ln_b[o], conv_w2[o], conv_b2[o])
            xp, cp = _conv_mixer(xp, jnp.zeros((Bp, CONV_W - 1, D_CONV), dt), *prm)
            xs, cs = _conv_mixer(xs, state_conv[o], *prm)
            conv_p.append(cp); conv_s.append(cs)
        pprm = (norm_ffn[li], peer_wq[li], peer_subkeys[li], peer_u[li], peer_v[li])
        xp = _peer(xp, *pprm)
        xs = _peer(xs, *pprm)
    return (xp, xs,
            jnp.stack(wkv_p), jnp.stack(shift_p), jnp.stack(k_p), jnp.stack(v_p), jnp.stack(kidx_p), jnp.stack(conv_p),
            jnp.stack(wkv_s), jnp.stack(shift_s), jnp.stack(k_s), jnp.stack(v_s), jnp.stack(kidx_s), jnp.stack(conv_s))
```

```python
import functools
import math

import jax
import jax.numpy as jnp
from jax import lax
from jax.experimental import pallas as pl
from jax.experimental.pallas import tpu as pltpu

D_MODEL = 1024
DEPTH = 4
CHUNK = 64
NORM_EPS = 1e-6
D_A = D_MODEL // 2
DH_A = 64
H_A = D_A // DH_A
LORA_W = 64
LORA_A = 64
LORA_G = 128
A_PROJ = 3 * D_A + LORA_W + LORA_A + LORA_G
GN_EPS = 64e-5
D_B = D_MODEL // 2
DH_B = 64
H_B = D_B // DH_B
H_I = 4
D_IDX = 64
TOPK_MAX = 256
Q_BLOCK = 128
B_PROJ = 3 * D_B + H_I * D_IDX + D_IDX + H_I
REL_BUCKETS = 32
REL_MAX_DIST = 128
D_CONV = D_MODEL
CONV_W = 31
PEER_HEADS = 8
PEER_DK = 256
N_KEYS = 128
TOPK_HALF = 16
PEER_TOPK = 16
PEER_BLOCK = 512

LANES = 128
VMEM_LIMIT = 48 * 1024 * 1024
PEER_TABLE_VMEM_LIMIT = 56 * 1024 * 1024

F32 = jnp.float32
BF16 = jnp.bfloat16


def _round_up(n, m):
    return -(-n // m) * m


def _col_tile(m):
    for t in (1024, 896, 768, 640, 512, 384, 256, 128):
        if m % t == 0:
            return t
    raise ValueError(m)


def _norm_linear_kernel(x_ref, g_ref, w_ref, b_ref, o_ref, h_scr):
    @pl.when(pl.program_id(1) == 0)
    def _():
        x = x_ref[...]
        ms = jnp.mean(x * x, axis=-1, keepdims=True)
        h_scr[...] = (x * lax.rsqrt(ms + NORM_EPS) * g_ref[...]).astype(BF16)

    o_ref[...] = jnp.dot(h_scr[...], w_ref[...], preferred_element_type=F32) + b_ref[...]


def norm_linear(x, g, w, b=None, row_tile=512):
    n, d = x.shape
    m = w.shape[1]
    mp = _round_up(m, 2 * LANES)
    wb = jnp.pad(w.astype(BF16), ((0, 0), (0, mp - m)))
    bb = jnp.zeros((1, mp), F32) if b is None else jnp.pad(b.astype(F32), (0, mp - m))[None]
    tn = _col_tile(mp)
    tm = min(row_tile, n)
    assert n % tm == 0
    out = pl.pallas_call(
        _norm_linear_kernel,
        grid=(n // tm, mp // tn),
        in_specs=[
            pl.BlockSpec((tm, d), lambda i, j: (i, 0)),
            pl.BlockSpec((1, d), lambda i, j: (0, 0)),
            pl.BlockSpec((d, tn), lambda i, j: (0, j)),
            pl.BlockSpec((1, tn), lambda i, j: (0, j)),
        ],
        out_specs=pl.BlockSpec((tm, tn), lambda i, j: (i, j)),
        out_shape=jax.ShapeDtypeStruct((n, mp), F32),
        scratch_shapes=[pltpu.VMEM((tm, d), BF16)],
        compiler_params=pltpu.CompilerParams(
            dimension_semantics=("arbitrary", "arbitrary"), vmem_limit_bytes=VMEM_LIMIT),
    )(x, g[None].astype(F32), wb, bb)
    return out[:, :m] if mp != m else out


def _linear_residual_kernel(x_ref, w_ref, b_ref, r_ref, o_ref):
    o_ref[...] = (r_ref[...] + b_ref[...]
                  + jnp.dot(x_ref[...].astype(BF16), w_ref[...], preferred_element_type=F32))


def linear_residual(x, w, b, res, row_tile=512):
    n, k = x.shape
    m = w.shape[1]
    tm = min(row_tile, n)
    assert n % tm == 0 and m % LANES == 0
    bb = jnp.zeros((1, m), F32) if b is None else b.astype(F32)[None]
    return pl.pallas_call(
        _linear_residual_kernel,
        grid=(n // tm,),
        in_specs=[
            pl.BlockSpec((tm, k), lambda i: (i, 0)),
            pl.BlockSpec((k, m), lambda i: (0, 0)),
            pl.BlockSpec((1, m), lambda i: (0, 0)),
            pl.BlockSpec((tm, m), lambda i: (i, 0)),
        ],
        out_specs=pl.BlockSpec((tm, m), lambda i: (i, 0)),
        out_shape=jax.ShapeDtypeStruct((n, m), F32),
        compiler_params=pltpu.CompilerParams(
            dimension_semantics=("arbitrary",), vmem_limit_bytes=VMEM_LIMIT),
    )(x, w.astype(BF16), bb, res)


PEER_TB = 128
PEER_E = PEER_HEADS * PEER_TOPK
N_EXPERTS = N_KEYS * N_KEYS
ROW_WORDS = D_MODEL // 2
ROW_SUB = ROW_WORDS // LANES
PSTRIDE = PEER_E + 8
PEER_DOWN_CHUNK = 32
NEG_INF = float("-inf")


def _top_rows(s, k):
    n = s.shape[0]
    rows = lax.broadcasted_iota(jnp.int32, s.shape, 0).astype(F32)
    out_rows = lax.broadcasted_iota(jnp.int32, (k, s.shape[1]), 0)
    vals = jnp.zeros((k, s.shape[1]), F32)
    ids = jnp.zeros((k, s.shape[1]), F32)
    for it in range(k):
        m = jnp.max(s, axis=0, keepdims=True)
        first = jnp.min(jnp.where(s == m, rows, float(n)), axis=0, keepdims=True)
        vals = jnp.where(out_rows == it, m, vals)
        ids = jnp.where(out_rows == it, first, ids)
        s = jnp.where(rows == first, NEG_INF, s)
    return vals, ids


def _peer_select_kernel(x_ref, g_ref, wq_ref, sk_ref, h_ref, idx_ref, gate_ref):
    x = x_ref[...]
    ms = jnp.mean(x * x, axis=-1, keepdims=True)
    h = x * lax.rsqrt(ms + NORM_EPS) * g_ref[...]
    h_ref[...] = h
    q = jnp.dot(h.astype(BF16), wq_ref[...], preferred_element_type=F32).astype(BF16)
    half = PEER_DK // 2
    tb = x.shape[0]
    crow = lax.broadcasted_iota(jnp.int32, (TOPK_HALF * TOPK_HALF, tb), 0).astype(F32)
    orow = lax.broadcasted_iota(jnp.int32, (PEER_TOPK, tb), 0)
    ids = []
    for hd in range(PEER_HEADS):
        sv, si = [], []
        for p in range(2):
            c = (hd * 2 + p) * half
            s = lax.dot_general(sk_ref[hd * 2 + p], q[:, c:c + half],
                                (((1,), (1,)), ((), ())), preferred_element_type=F32)
            v, i = _top_rows(s, TOPK_HALF)
            sv.append(v)
            si.append(i)
        cand = jnp.concatenate([sv[0][i:i + 1] + sv[1] for i in range(TOPK_HALF)], axis=0)
        eid = jnp.concatenate([si[0][i:i + 1] * float(N_KEYS) + si[1] for i in range(TOPK_HALF)], axis=0)
        cs = jnp.zeros((PEER_TOPK, tb), F32)
        ce = jnp.zeros((PEER_TOPK, tb), F32)
        for it in range(PEER_TOPK):
            m = jnp.max(cand, axis=0, keepdims=True)
            first = jnp.min(jnp.where(cand == m, crow, float(crow.shape[0])), axis=0, keepdims=True)
            hit = crow == first
            e = jnp.max(jnp.where(hit, eid, -1.0), axis=0, keepdims=True)
            cs = jnp.where(orow == it, m, cs)
            ce = jnp.where(orow == it, e, ce)
            cand = jnp.where(hit, NEG_INF, cand)
        ex = jnp.exp(cs - cs[0:1])
        gates = ex / jnp.sum(ex, axis=0, keepdims=True)
        ids.append(ce)
        gate_ref[0, hd * PEER_TOPK:(hd + 1) * PEER_TOPK, :] = gates
    idx_ref[0] = jnp.concatenate(ids, axis=0).T.astype(jnp.int32)


def _unpack_row(row):
    lo = pltpu.bitcast(lax.shift_left(row, 16), F32)
    hi = pltpu.bitcast(jnp.bitwise_and(row, jnp.int32(-65536)), F32)
    return lo, hi


def _load_block_scalars(src_hbm, dst_smem, sem):
    cp = pltpu.make_async_copy(src_hbm.at[pl.program_id(0)], dst_smem, sem)
    cp.start()
    cp.wait()


def _load_table_once(tab_hbm, tab_vmem, sem):
    @pl.when(pl.program_id(0) == 0)
    def _():
        cp = pltpu.make_async_copy(tab_hbm, tab_vmem, sem)
        cp.start()
        cp.wait()


def _peer_up_kernel(idx_hbm, tab_hbm, h_ref, gate_ref, w_ref, tab, idx_s, pbuf_a, pbuf_b, act_s, sems):
    _load_table_once(tab_hbm, tab, sems.at[0])
    _load_block_scalars(idx_hbm, idx_s, sems.at[1])
    lane = lax.broadcasted_iota(jnp.int32, (PEER_E, PEER_TB), 1)
    act_s[...] = jnp.zeros((PEER_E, PEER_TB), F32)
    pbuf_b[...] = jnp.zeros(pbuf_b.shape, F32)

    def gather(t, pbuf):
        ht = h_ref[t]
        h_lo, h_hi = ht[0:ROW_SUB], ht[ROW_SUB:2 * ROW_SUB]
        base = t * PEER_E
        for e in range(PEER_E):
            lo, hi = _unpack_row(tab[idx_s[base + e]])
            pbuf[pl.ds(e, ROW_SUB, stride=PSTRIDE), :] = lo * h_lo + hi * h_hi

    def reduce(t, pbuf):
        acc = pbuf[0:PEER_E, :]
        for c in range(1, ROW_SUB):
            acc = acc + pbuf[c * PSTRIDE:c * PSTRIDE + PEER_E, :]
        col = jnp.sum(acc, axis=-1, keepdims=True)
        act_s[...] = jnp.where(lane == t, col, act_s[...])

    def token_pair(j, carry):
        reduce(2 * j - 1, pbuf_b)
        gather(2 * j, pbuf_a)
        reduce(2 * j, pbuf_a)
        gather(2 * j + 1, pbuf_b)
        return carry

    lax.fori_loop(0, PEER_TB // 2, token_pair, 0)
    reduce(PEER_TB - 1, pbuf_b)
    a = act_s[...]
    w_ref[0] = (gate_ref[0] * (0.5 * a * (1.0 + lax.erf(a * (2.0 ** -0.5))))).T


def _peer_down_kernel(idx_hbm, w_hbm, tab_hbm, x_ref, o_ref, tab, idx_s, w_s, sems):
    _load_table_once(tab_hbm, tab, sems.at[0])
    _load_block_scalars(idx_hbm, idx_s, sems.at[1])
    _load_block_scalars(w_hbm, w_s, sems.at[2])
    n_acc = 4
    zero = jnp.zeros((ROW_SUB, LANES), F32)

    def token(t, carry):
        def chunk(c, accs):
            acc_lo, acc_hi = list(accs[:n_acc]), list(accs[n_acc:])
            base = t * PEER_E + c * PEER_DOWN_CHUNK
            for j in range(PEER_DOWN_CHUNK):
                lo, hi = _unpack_row(tab[idx_s[base + j]])
                w = w_s[base + j]
                acc_lo[j % n_acc] = acc_lo[j % n_acc] + w * lo
                acc_hi[j % n_acc] = acc_hi[j % n_acc] + w * hi
            return tuple(acc_lo) + tuple(acc_hi)

        accs = lax.fori_loop(0, PEER_E // PEER_DOWN_CHUNK, chunk, (zero,) * (2 * n_acc))
        lo = (accs[0] + accs[1]) + (accs[2] + accs[3])
        hi = (accs[4] + accs[5]) + (accs[6] + accs[7])
        o_ref[t] = x_ref[t] + jnp.concatenate([lo, hi], axis=0)
        return carry

    lax.fori_loop(0, PEER_TB, token, 0)


def _pack_table(tab):
    n = tab.shape[0]
    bits = lax.bitcast_convert_type(tab.astype(jnp.bfloat16), jnp.uint16).astype(jnp.uint32)
    words = bits[:, :ROW_WORDS] | (bits[:, ROW_WORDS:] << 16)
    return lax.bitcast_convert_type(words, jnp.int32).reshape(n, ROW_SUB, LANES)


def peer(x, g, wq, subkeys, u_tab, v_tab):
    n, d = x.shape
    assert n % PEER_TB == 0 and d == D_MODEL
    nb = n // PEER_TB
    sk = subkeys.reshape(PEER_HEADS * 2, N_KEYS, PEER_DK // 2).astype(BF16)
    params = pltpu.CompilerParams(dimension_semantics=("arbitrary",), vmem_limit_bytes=VMEM_LIMIT)
    h, idx, gates = pl.pallas_call(
        _peer_select_kernel,
        grid=(nb,),
        in_specs=[
            pl.BlockSpec((PEER_TB, d), lambda i: (i, 0)),
            pl.BlockSpec((1, d), lambda i: (0, 0)),
            pl.BlockSpec((d, PEER_HEADS * PEER_DK), lambda i: (0, 0)),
            pl.BlockSpec((PEER_HEADS * 2, N_KEYS, PEER_DK // 2), lambda i: (0, 0, 0)),
        ],
        out_specs=[
            pl.BlockSpec((PEER_TB, d), lambda i: (i, 0)),
            pl.BlockSpec((1, PEER_TB, PEER_E), lambda i: (i, 0, 0)),
            pl.BlockSpec((1, PEER_E, PEER_TB), lambda i: (i, 0, 0)),
        ],
        out_shape=[
            jax.ShapeDtypeStruct((n, d), F32),
            jax.ShapeDtypeStruct((nb, PEER_TB, PEER_E), jnp.int32),
            jax.ShapeDtypeStruct((nb, PEER_E, PEER_TB), F32),
        ],
        compiler_params=params,
        name="peer_select",
    )(x, g[None].astype(F32), wq.astype(BF16), sk)

    table_params = pltpu.CompilerParams(dimension_semantics=("arbitrary",),
                                        vmem_limit_bytes=PEER_TABLE_VMEM_LIMIT)
    tok_tiles = (PEER_TB, d // LANES, LANES)
    w = pl.pallas_call(
        _peer_up_kernel,
        grid=(nb,),
        in_specs=[
            pl.BlockSpec(memory_space=pl.ANY),
            pl.BlockSpec(memory_space=pl.ANY),
            pl.BlockSpec(tok_tiles, lambda i: (i, 0, 0)),
            pl.BlockSpec((1, PEER_E, PEER_TB), lambda i: (i, 0, 0)),
        ],
        out_specs=pl.BlockSpec((1, PEER_TB, PEER_E), lambda i: (i, 0, 0)),
        out_shape=jax.ShapeDtypeStruct((nb, PEER_TB, PEER_E), F32),
        scratch_shapes=[
            pltpu.VMEM((N_EXPERTS, ROW_SUB, LANES), jnp.int32),
            pltpu.SMEM((PEER_TB * PEER_E,), jnp.int32),
            pltpu.VMEM((ROW_SUB * PSTRIDE, LANES), F32),
            pltpu.VMEM((ROW_SUB * PSTRIDE, LANES), F32),
            pltpu.VMEM((PEER_E, PEER_TB), F32),
            pltpu.SemaphoreType.DMA((2,)),
        ],
        compiler_params=table_params,
        name="peer_up",
    )(idx.reshape(nb, PEER_TB * PEER_E), _pack_table(u_tab), h.reshape(n, d // LANES, LANES), gates)

    out = pl.pallas_call(
        _peer_down_kernel,
        grid=(nb,),
        in_specs=[
            pl.BlockSpec(memory_space=pl.ANY),
            pl.BlockSpec(memory_space=pl.ANY),
            pl.BlockSpec(memory_space=pl.ANY),
            pl.BlockSpec(tok_tiles, lambda i: (i, 0, 0)),
        ],
        out_specs=pl.BlockSpec(tok_tiles, lambda i: (i, 0, 0)),
        out_shape=jax.ShapeDtypeStruct((n, d // LANES, LANES), F32),
        scratch_shapes=[
            pltpu.VMEM((N_EXPERTS, ROW_SUB, LANES), jnp.int32),
            pltpu.SMEM((PEER_TB * PEER_E,), jnp.int32),
            pltpu.SMEM((PEER_TB * PEER_E,), F32),
            pltpu.SemaphoreType.DMA((3,)),
        ],
        compiler_params=table_params,
        name="peer_down",
    )(idx.reshape(nb, PEER_TB * PEER_E), w.reshape(nb, PEER_TB * PEER_E), _pack_table(v_tab),
      x.reshape(n, d // LANES, LANES))
    return out.reshape(n, d)


def _rmsnorm(x, g):
    xf = x.astype(F32)
    y = xf * lax.rsqrt(jnp.mean(xf * xf, axis=-1, keepdims=True) + NORM_EPS)
    return (y * g).astype(x.dtype)


def _t5_bucket(rel):
    nb = REL_BUCKETS // 2
    ret = jnp.where(rel > 0, nb, 0)
    n = jnp.abs(rel)
    max_exact = nb // 2
    nf = jnp.maximum(n, 1).astype(F32)
    large = max_exact + (jnp.log(nf / max_exact) / math.log(REL_MAX_DIST / max_exact)
                         * (nb - max_exact)).astype(jnp.int32)
    large = jnp.minimum(large, nb - 1)
    return ret + jnp.where(n < max_exact, n, large)


def _wkv_scan(r, w, k, v, kk, a, s0):
    xs = tuple(jnp.moveaxis(t.astype(F32), 1, 0) for t in (r, w, k, v, kk, a))

    def step(S, inp):
        r_t, w_t, k_t, v_t, kk_t, a_t = inp
        sa = jnp.einsum('bhvk,bhk->bhv', S, -kk_t)
        S = (S * w_t[:, :, None, :] + sa[..., None] * (kk_t * a_t)[:, :, None, :]
             + v_t[..., None] * k_t[:, :, None, :])
        return S, jnp.einsum('bhvk,bhk->bhv', S, r_t)

    S, o = lax.scan(step, s0.astype(F32), xs)
    return jnp.moveaxis(o, 0, 1), S.astype(s0.dtype)


def _dsa_attend(q, k_all, v_all, qi, wi, ki_all, q_pos, k_pos, rel_bias, topk):
    Bx, T, H, Dh = q.shape
    qb = Q_BLOCK if T % Q_BLOCK == 0 else T
    nblk = T // qb

    def to_blocks(t):
        return jnp.moveaxis(t.reshape(Bx, nblk, qb, *t.shape[2:]), 1, 0)

    k_chunk = k_pos // CHUNK

    def block(args):
        qx, qix, wix, pos = args
        sc = jax.nn.relu(jnp.einsum('bqhd,bsd->bqhs', qix, ki_all).astype(F32) * D_IDX ** -0.5)
        idx = jnp.einsum('bqhs,bqh->bqs', sc, wix.astype(F32) * H_I ** -0.5)
        vis = k_chunk[None, :] <= (pos // CHUNK)[:, None]
        idx = jnp.where(vis[None], idx, -jnp.inf)
        top_s, top_i = lax.top_k(idx, topk)
        k_sel = jax.vmap(lambda kb, ib: kb[ib])(k_all, top_i)
        v_sel = jax.vmap(lambda vb, ib: vb[ib])(v_all, top_i)
        logit = jnp.einsum('bqhd,bqkhd->bqhk', qx, k_sel).astype(F32) * Dh ** -0.5
        rel = k_pos[top_i] - pos[None, :, None]
        bias = rel_bias[_t5_bucket(rel)]
        logit = logit + jnp.moveaxis(bias, -1, 2).astype(F32)
        logit = jnp.where(jnp.isfinite(top_s)[:, :, None, :], logit, -jnp.inf)
        p = jax.nn.softmax(logit, axis=-1)
        return jnp.einsum('bqhk,bqkhd->bqhd', p.astype(v_sel.dtype), v_sel)

    out = lax.map(block, (to_blocks(q), to_blocks(qi), to_blocks(wi), q_pos.reshape(nblk, qb)))
    return jnp.moveaxis(out, 0, 1).reshape(Bx, T, H, Dh)


def _even_mixer(x, shift_prev, wkv_prev, k_past, v_past, kidx_past,
                g_mix, w_in, mu, w0, w_du, a0, w_au, w_gu, k_k, k_a, r_k, gn_w, gn_b,
                qn_g, kn_g, rel_bias, w_out):
    Bx, T, _ = x.shape
    p = norm_linear(x.reshape(Bx * T, D_MODEL), g_mix, w_in).reshape(Bx, T, -1)
    pa, pb = p[..., :A_PROJ], p[..., A_PROJ:]
    pa_prev = jnp.concatenate([shift_prev[:, None, :].astype(pa.dtype), pa[:, :-1]], axis=1)
    xm = (pa + (pa_prev - pa) * mu).astype(F32)
    r, k, v, dw, da, dg = jnp.split(
        xm, [D_A, 2 * D_A, 3 * D_A, 3 * D_A + LORA_W, 3 * D_A + LORA_W + LORA_A], axis=-1)
    w_log = -jax.nn.softplus(-(w0 + jnp.tanh(dw) @ w_du)) - 0.5
    decay = jnp.exp(-jnp.exp(w_log))
    a = jax.nn.sigmoid(a0 + da @ w_au)
    gate = jax.nn.sigmoid(dg) @ w_gu
    kk = (k * k_k).reshape(Bx, T, H_A, DH_A)
    kk = kk / jnp.maximum(jnp.sqrt(jnp.sum(kk * kk, axis=-1, keepdims=True)), 1e-12)
    k = k * (1.0 + (a - 1.0) * k_a)
    r, k, v, decay, a = (t.reshape(Bx, T, H_A, DH_A) for t in (r, k, v, decay, a))
    o, wkv_new = _wkv_scan(r, decay, k, v, kk, a, wkv_prev)
    mo = jnp.mean(o, axis=-1, keepdims=True)
    vo = jnp.mean(jnp.square(o - mo), axis=-1, keepdims=True)
    o = ((o - mo) * lax.rsqrt(vo + GN_EPS)).reshape(Bx, T, D_A) * gn_w + gn_b
    o = o + (jnp.sum(r * k * r_k, axis=-1, keepdims=True) * v).reshape(Bx, T, D_A)
    o_a = (o * gate).astype(x.dtype)
    q, kb, vb, qi, ki, wi = jnp.split(
        pb, [D_B, 2 * D_B, 3 * D_B, 3 * D_B + H_I * D_IDX, 3 * D_B + H_I * D_IDX + D_IDX], axis=-1)
    q = _rmsnorm(q.reshape(Bx, T, H_B, DH_B), qn_g)
    kb = _rmsnorm(kb.reshape(Bx, T, H_B, DH_B), kn_g)
    vb = vb.reshape(Bx, T, H_B, DH_B)
    qi = qi.reshape(Bx, T, H_I, D_IDX)
    offset = k_past.shape[1]
    L = offset + T
    topk = min(TOPK_MAX, L // 4)
    k_all = jnp.concatenate([k_past.astype(kb.dtype), kb], axis=1)
    v_all = jnp.concatenate([v_past.astype(vb.dtype), vb], axis=1)
    ki_all = jnp.concatenate([kidx_past.astype(ki.dtype), ki], axis=1)
    q_pos = offset + jnp.arange(T, dtype=jnp.int32)
    k_pos = jnp.arange(L, dtype=jnp.int32)
    o_b = _dsa_attend(q, k_all, v_all, qi, wi, ki_all, q_pos, k_pos, rel_bias, topk)
    cat = jnp.concatenate([o_a, o_b.reshape(Bx, T, D_B).astype(x.dtype)], axis=-1)
    x_new = linear_residual(cat.reshape(Bx * T, D_MODEL), w_out, None,
                            x.reshape(Bx * T, D_MODEL)).reshape(Bx, T, D_MODEL)
    return x_new, pa[:, -1], wkv_new, kb, vb, ki


def _conv_mixer(x, conv_prev, g_mix, w1, b1, w_dw, b_dw, ln_g, ln_b, w2, b2):
    Bx, T, _ = x.shape
    u = norm_linear(x.reshape(Bx * T, D_MODEL), g_mix, w1, b1).reshape(Bx, T, -1)
    glu = u[..., :D_CONV] * jax.nn.sigmoid(u[..., D_CONV:])
    padded = jnp.concatenate([conv_prev.astype(glu.dtype), glu], axis=1)
    y = lax.conv_general_dilated(padded, w_dw[:, None, :].astype(glu.dtype), (1,), 'VALID',
                                 dimension_numbers=('NWC', 'WIO', 'NWC'),
                                 feature_group_count=D_CONV) + b_dw
    yf = y.astype(F32)
    m = jnp.mean(yf, axis=-1, keepdims=True)
    var = jnp.mean(jnp.square(yf - m), axis=-1, keepdims=True)
    yn = jax.nn.silu((yf - m) * lax.rsqrt(var + 1e-5) * ln_g + ln_b)
    x_new = linear_residual(yn.reshape(Bx * T, D_CONV), w2, b2,
                            x.reshape(Bx * T, D_MODEL)).reshape(Bx, T, D_MODEL)
    return x_new, padded[:, -(CONV_W - 1):]


def _peer_both(xp, xs, g, wq, subkeys, u_tab, v_tab):
    np_ = xp.shape[0] * xp.shape[1]
    rows = jnp.concatenate([xp.reshape(np_, D_MODEL), xs.reshape(-1, D_MODEL)], axis=0)
    out = peer(rows, g, wq, subkeys, u_tab, v_tab)
    return out[:np_].reshape(xp.shape), out[np_:].reshape(xs.shape)


def kernel(x_prompt, x_sample, state_wkv, state_shift, cache_k, cache_v, cache_kidx, state_conv,
           norm_mix, norm_ffn, w_in, mu_shift, w0, w_decay_up, a0, w_iclr_up, w_gate_up,
           k_k, k_a, r_k, gn_w, gn_b, qn_g, kn_g, rel_bias, w_out,
           conv_w1, conv_b1, conv_dw, conv_bdw, conv_ln_g, conv_ln_b, conv_w2, conv_b2,
           peer_wq, peer_subkeys, peer_u, peer_v):
    xp, xs = x_prompt, x_sample
    Bp = xp.shape[0]
    dt = xp.dtype
    wkv_p, shift_p, k_p, v_p, kidx_p, conv_p = [], [], [], [], [], []
    wkv_s, shift_s, k_s, v_s, kidx_s, conv_s = [], [], [], [], [], []
    for li in range(DEPTH):
        if li % 2 == 0:
            e = li // 2
            prm = (norm_mix[li], w_in[e], mu_shift[e], w0[e], w_decay_up[e], a0[e], w_iclr_up[e],
                   w_gate_up[e], k_k[e], k_a[e], r_k[e], gn_w[e], gn_b[e], qn_g[e], kn_g[e],
                   rel_bias, w_out[e])
            xp, sh, wk, kb, vb, ki = _even_mixer(
                xp, jnp.zeros((Bp, A_PROJ), dt), jnp.zeros((Bp, H_A, DH_A, DH_A), dt),
                jnp.zeros((Bp, 0, H_B, DH_B), dt), jnp.zeros((Bp, 0, H_B, DH_B), dt),
                jnp.zeros((Bp, 0, D_IDX), dt), *prm)
            wkv_p.append(wk); shift_p.append(sh); k_p.append(kb); v_p.append(vb); kidx_p.append(ki)
            xs, sh, wk, kb, vb, ki = _even_mixer(
                xs, state_shift[e], state_wkv[e], cache_k[e], cache_v[e], cache_kidx[e], *prm)
            wkv_s.append(wk); shift_s.append(sh); k_s.append(kb); v_s.append(vb); kidx_s.append(ki)
        else:
            o = li // 2
            prm = (norm_mix[li], conv_w1[o], conv_b1[o], conv_dw[o], conv_bdw[o],
                   conv_ln_g[o], conv_ln_b[o], conv_w2[o], conv_b2[o])
            xp, cp = _conv_mixer(xp, jnp.zeros((Bp, CONV_W - 1, D_CONV), dt), *prm)
            xs, cs = _conv_mixer(xs, state_conv[o], *prm)
            conv_p.append(cp); conv_s.append(cs)
        pprm = (norm_ffn[li], peer_wq[li], peer_subkeys[li], peer_u[li], peer_v[li])
        xp, xs = _peer_both(xp, xs, *pprm)
    return (xp, xs,
            jnp.stack(wkv_p), jnp.stack(shift_p), jnp.stack(k_p), jnp.stack(v_p), jnp.stack(kidx_p), jnp.stack(conv_p),
            jnp.stack(wkv_s), jnp.stack(shift_s), jnp.stack(k_s), jnp.stack(v_s), jnp.stack(kidx_s), jnp.stack(conv_s))
```

```python
import functools
import math

import jax
import jax.numpy as jnp
from jax import lax
from jax.experimental import pallas as pl
from jax.experimental.pallas import tpu as pltpu

D_MODEL = 1024
DEPTH = 4
CHUNK = 64
NORM_EPS = 1e-6
D_A = D_MODEL // 2
DH_A = 64
H_A = D_A // DH_A
LORA_W = 64
LORA_A = 64
LORA_G = 128
A_PROJ = 3 * D_A + LORA_W + LORA_A + LORA_G
GN_EPS = 64e-5
D_B = D_MODEL // 2
DH_B = 64
H_B = D_B // DH_B
H_I = 4
D_IDX = 64
TOPK_MAX = 256
Q_BLOCK = 128
B_PROJ = 3 * D_B + H_I * D_IDX + D_IDX + H_I
REL_BUCKETS = 32
REL_MAX_DIST = 128
D_CONV = D_MODEL
CONV_W = 31
PEER_HEADS = 8
PEER_DK = 256
N_KEYS = 128
TOPK_HALF = 16
PEER_TOPK = 16
PEER_BLOCK = 512

LANES = 128
VMEM_LIMIT = 48 * 1024 * 1024
PEER_TABLE_VMEM_LIMIT = 56 * 1024 * 1024

F32 = jnp.float32
BF16 = jnp.bfloat16


def _round_up(n, m):
    return -(-n // m) * m


def _col_tile(m):
    for t in (1024, 896, 768, 640, 512, 384, 256, 128):
        if m % t == 0:
            return t
    raise ValueError(m)


def _norm_linear_kernel(x_ref, g_ref, w_ref, b_ref, o_ref, h_scr):
    @pl.when(pl.program_id(1) == 0)
    def _():
        x = x_ref[...]
        ms = jnp.mean(x * x, axis=-1, keepdims=True)
        h_scr[...] = (x * lax.rsqrt(ms + NORM_EPS) * g_ref[...]).astype(BF16)

    o_ref[...] = jnp.dot(h_scr[...], w_ref[...], preferred_element_type=F32) + b_ref[...]


def norm_linear(x, g, w, b=None, row_tile=512):
    n, d = x.shape
    m = w.shape[1]
    mp = _round_up(m, 2 * LANES)
    wb = jnp.pad(w.astype(BF16), ((0, 0), (0, mp - m)))
    bb = jnp.zeros((1, mp), F32) if b is None else jnp.pad(b.astype(F32), (0, mp - m))[None]
    tn = _col_tile(mp)
    tm = min(row_tile, n)
    assert n % tm == 0
    out = pl.pallas_call(
        _norm_linear_kernel,
        grid=(n // tm, mp // tn),
        in_specs=[
            pl.BlockSpec((tm, d), lambda i, j: (i, 0)),
            pl.BlockSpec((1, d), lambda i, j: (0, 0)),
            pl.BlockSpec((d, tn), lambda i, j: (0, j)),
            pl.BlockSpec((1, tn), lambda i, j: (0, j)),
        ],
        out_specs=pl.BlockSpec((tm, tn), lambda i, j: (i, j)),
        out_shape=jax.ShapeDtypeStruct((n, mp), F32),
        scratch_shapes=[pltpu.VMEM((tm, d), BF16)],
        compiler_params=pltpu.CompilerParams(
            dimension_semantics=("arbitrary", "arbitrary"), vmem_limit_bytes=VMEM_LIMIT),
    )(x, g[None].astype(F32), wb, bb)
    return out[:, :m] if mp != m else out


def _linear_residual_kernel(x_ref, w_ref, b_ref, r_ref, o_ref):
    o_ref[...] = (r_ref[...] + b_ref[...]
                  + jnp.dot(x_ref[...].astype(BF16), w_ref[...], preferred_element_type=F32))


def linear_residual(x, w, b, res, row_tile=512):
    n, k = x.shape
    m = w.shape[1]
    tm = min(row_tile, n)
    assert n % tm == 0 and m % LANES == 0
    bb = jnp.zeros((1, m), F32) if b is None else b.astype(F32)[None]
    return pl.pallas_call(
        _linear_residual_kernel,
        grid=(n // tm,),
        in_specs=[
            pl.BlockSpec((tm, k), lambda i: (i, 0)),
            pl.BlockSpec((k, m), lambda i: (0, 0)),
            pl.BlockSpec((1, m), lambda i: (0, 0)),
            pl.BlockSpec((tm, m), lambda i: (i, 0)),
        ],
        out_specs=pl.BlockSpec((tm, m), lambda i: (i, 0)),
        out_shape=jax.ShapeDtypeStruct((n, m), F32),
        compiler_params=pltpu.CompilerParams(
            dimension_semantics=("arbitrary",), vmem_limit_bytes=VMEM_LIMIT),
    )(x, w.astype(BF16), bb, res)


PEER_TB = 128
PEER_E = PEER_HEADS * PEER_TOPK
N_EXPERTS = N_KEYS * N_KEYS
ROW_WORDS = D_MODEL // 2
ROW_SUB = ROW_WORDS // LANES
PSTRIDE = PEER_E + 8
PEER_DOWN_CHUNK = 32
NEG_INF = float("-inf")


def _top_rows(s, k):
    n = s.shape[0]
    rows = lax.broadcasted_iota(jnp.int32, s.shape, 0).astype(F32)
    out_rows = lax.broadcasted_iota(jnp.int32, (k, s.shape[1]), 0)
    vals = jnp.zeros((k, s.shape[1]), F32)
    ids = jnp.zeros((k, s.shape[1]), F32)
    for it in range(k):
        m = jnp.max(s, axis=0, keepdims=True)
        first = jnp.min(jnp.where(s == m, rows, float(n)), axis=0, keepdims=True)
        vals = jnp.where(out_rows == it, m, vals)
        ids = jnp.where(out_rows == it, first, ids)
        s = jnp.where(rows == first, NEG_INF, s)
    return vals, ids


def _peer_select_kernel(x_ref, g_ref, wq_ref, sk_ref, h_ref, idx_ref, gate_ref):
    x = x_ref[...]
    ms = jnp.mean(x * x, axis=-1, keepdims=True)
    h = x * lax.rsqrt(ms + NORM_EPS) * g_ref[...]
    h_ref[...] = h
    q = jnp.dot(h.astype(BF16), wq_ref[...], preferred_element_type=F32).astype(BF16)
    half = PEER_DK // 2
    tb = x.shape[0]
    crow = lax.broadcasted_iota(jnp.int32, (TOPK_HALF * TOPK_HALF, tb), 0).astype(F32)
    orow = lax.broadcasted_iota(jnp.int32, (PEER_TOPK, tb), 0)
    ids = []
    for hd in range(PEER_HEADS):
        sv, si = [], []
        for p in range(2):
            c = (hd * 2 + p) * half
            s = lax.dot_general(sk_ref[hd * 2 + p], q[:, c:c + half],
                                (((1,), (1,)), ((), ())), preferred_element_type=F32)
            v, i = _top_rows(s, TOPK_HALF)
            sv.append(v)
            si.append(i)
        cand = jnp.concatenate([sv[0][i:i + 1] + sv[1] for i in range(TOPK_HALF)], axis=0)
        eid = jnp.concatenate([si[0][i:i + 1] * float(N_KEYS) + si[1] for i in range(TOPK_HALF)], axis=0)
        cs = jnp.zeros((PEER_TOPK, tb), F32)
        ce = jnp.zeros((PEER_TOPK, tb), F32)
        for it in range(PEER_TOPK):
            m = jnp.max(cand, axis=0, keepdims=True)
            first = jnp.min(jnp.where(cand == m, crow, float(crow.shape[0])), axis=0, keepdims=True)
            hit = crow == first
            e = jnp.max(jnp.where(hit, eid, -1.0), axis=0, keepdims=True)
            cs = jnp.where(orow == it, m, cs)
            ce = jnp.where(orow == it, e, ce)
            cand = jnp.where(hit, NEG_INF, cand)
        ex = jnp.exp(cs - cs[0:1])
        gates = ex / jnp.sum(ex, axis=0, keepdims=True)
        ids.append(ce)
        gate_ref[0, hd * PEER_TOPK:(hd + 1) * PEER_TOPK, :] = gates
    idx_ref[0] = jnp.concatenate(ids, axis=0).T.astype(jnp.int32)


def _unpack_row(row):
    lo = pltpu.bitcast(lax.shift_left(row, 16), F32)
    hi = pltpu.bitcast(jnp.bitwise_and(row, jnp.int32(-65536)), F32)
    return lo, hi


def _load_block_scalars(src_hbm, dst_smem, sem):
    cp = pltpu.make_async_copy(src_hbm.at[pl.program_id(0)], dst_smem, sem)
    cp.start()
    cp.wait()


def _load_table_once(tab_hbm, tab_vmem, sem):
    @pl.when(pl.program_id(0) == 0)
    def _():
        cp = pltpu.make_async_copy(tab_hbm, tab_vmem, sem)
        cp.start()
        cp.wait()


def _peer_up_kernel(idx_hbm, tab_hbm, h_ref, gate_ref, w_ref, tab, idx_s, pbuf_a, pbuf_b, act_s, sems):
    _load_table_once(tab_hbm, tab, sems.at[0])
    _load_block_scalars(idx_hbm, idx_s, sems.at[1])
    lane = lax.broadcasted_iota(jnp.int32, (PEER_E, PEER_TB), 1)
    act_s[...] = jnp.zeros((PEER_E, PEER_TB), F32)
    pbuf_b[...] = jnp.zeros(pbuf_b.shape, F32)

    def gather(t, pbuf):
        ht = h_ref[t]
        h_lo, h_hi = ht[0:ROW_SUB], ht[ROW_SUB:2 * ROW_SUB]
        base = t * PEER_E
        for e in range(PEER_E):
            lo, hi = _unpack_row(tab[idx_s[base + e]])
            pbuf[pl.ds(e, ROW_SUB, stride=PSTRIDE), :] = lo * h_lo + hi * h_hi

    def reduce(t, pbuf):
        acc = pbuf[0:PEER_E, :]
        for c in range(1, ROW_SUB):
            acc = acc + pbuf[c * PSTRIDE:c * PSTRIDE + PEER_E, :]
        col = jnp.sum(acc, axis=-1, keepdims=True)
        act_s[...] = jnp.where(lane == t, col, act_s[...])

    def token_pair(j, carry):
        reduce(2 * j - 1, pbuf_b)
        gather(2 * j, pbuf_a)
        reduce(2 * j, pbuf_a)
        gather(2 * j + 1, pbuf_b)
        return carry

    lax.fori_loop(0, PEER_TB // 2, token_pair, 0)
    reduce(PEER_TB - 1, pbuf_b)
    a = act_s[...]
    w_ref[0] = (gate_ref[0] * (0.5 * a * (1.0 + lax.erf(a * (2.0 ** -0.5))))).T


def _peer_down_kernel(idx_hbm, w_hbm, tab_hbm, x_ref, o_ref, tab, idx_s, w_s, sems):
    _load_table_once(tab_hbm, tab, sems.at[0])
    _load_block_scalars(idx_hbm, idx_s, sems.at[1])
    _load_block_scalars(w_hbm, w_s, sems.at[2])
    n_acc = 4
    zero = jnp.zeros((ROW_SUB, LANES), F32)

    def token(t, carry):
        def chunk(c, accs):
            acc_lo, acc_hi = list(accs[:n_acc]), list(accs[n_acc:])
            base = t * PEER_E + c * PEER_DOWN_CHUNK
            for j in range(PEER_DOWN_CHUNK):
                lo, hi = _unpack_row(tab[idx_s[base + j]])
                w = w_s[base + j]
                acc_lo[j % n_acc] = acc_lo[j % n_acc] + w * lo
                acc_hi[j % n_acc] = acc_hi[j % n_acc] + w * hi
            return tuple(acc_lo) + tuple(acc_hi)

        accs = lax.fori_loop(0, PEER_E // PEER_DOWN_CHUNK, chunk, (zero,) * (2 * n_acc))
        lo = (accs[0] + accs[1]) + (accs[2] + accs[3])
        hi = (accs[4] + accs[5]) + (accs[6] + accs[7])
        o_ref[t] = x_ref[t] + jnp.concatenate([lo, hi], axis=0)
        return carry

    lax.fori_loop(0, PEER_TB, token, 0)


def _pack_table(tab):
    n = tab.shape[0]
    bits = lax.bitcast_convert_type(tab.astype(jnp.bfloat16), jnp.uint16).astype(jnp.uint32)
    words = bits[:, :ROW_WORDS] | (bits[:, ROW_WORDS:] << 16)
    return lax.bitcast_convert_type(words, jnp.int32).reshape(n, ROW_SUB, LANES)


def peer(x, g, wq, subkeys, u_tab, v_tab):
    n, d = x.shape
    assert n % PEER_TB == 0 and d == D_MODEL
    nb = n // PEER_TB
    sk = subkeys.reshape(PEER_HEADS * 2, N_KEYS, PEER_DK // 2).astype(BF16)
    params = pltpu.CompilerParams(dimension_semantics=("arbitrary",), vmem_limit_bytes=VMEM_LIMIT)
    h, idx, gates = pl.pallas_call(
        _peer_select_kernel,
        grid=(nb,),
        in_specs=[
            pl.BlockSpec((PEER_TB, d), lambda i: (i, 0)),
            pl.BlockSpec((1, d), lambda i: (0, 0)),
            pl.BlockSpec((d, PEER_HEADS * PEER_DK), lambda i: (0, 0)),
            pl.BlockSpec((PEER_HEADS * 2, N_KEYS, PEER_DK // 2), lambda i: (0, 0, 0)),
        ],
        out_specs=[
            pl.BlockSpec((PEER_TB, d), lambda i: (i, 0)),
            pl.BlockSpec((1, PEER_TB, PEER_E), lambda i: (i, 0, 0)),
            pl.BlockSpec((1, PEER_E, PEER_TB), lambda i: (i, 0, 0)),
        ],
        out_shape=[
            jax.ShapeDtypeStruct((n, d), F32),
            jax.ShapeDtypeStruct((nb, PEER_TB, PEER_E), jnp.int32),
            jax.ShapeDtypeStruct((nb, PEER_E, PEER_TB), F32),
        ],
        compiler_params=params,
        name="peer_select",
    )(x, g[None].astype(F32), wq.astype(BF16), sk)

    table_params = pltpu.CompilerParams(dimension_semantics=("arbitrary",),
                                        vmem_limit_bytes=PEER_TABLE_VMEM_LIMIT)
    tok_tiles = (PEER_TB, d // LANES, LANES)
    w = pl.pallas_call(
        _peer_up_kernel,
        grid=(nb,),
        in_specs=[
            pl.BlockSpec(memory_space=pl.ANY),
            pl.BlockSpec(memory_space=pl.ANY),
            pl.BlockSpec(tok_tiles, lambda i: (i, 0, 0)),
            pl.BlockSpec((1, PEER_E, PEER_TB), lambda i: (i, 0, 0)),
        ],
        out_specs=pl.BlockSpec((1, PEER_TB, PEER_E), lambda i: (i, 0, 0)),
        out_shape=jax.ShapeDtypeStruct((nb, PEER_TB, PEER_E), F32),
        scratch_shapes=[
            pltpu.VMEM((N_EXPERTS, ROW_SUB, LANES), jnp.int32),
            pltpu.SMEM((PEER_TB * PEER_E,), jnp.int32),
            pltpu.VMEM((ROW_SUB * PSTRIDE, LANES), F32),
            pltpu.VMEM((ROW_SUB * PSTRIDE, LANES), F32),
            pltpu.VMEM((PEER_E, PEER_TB), F32),
            pltpu.SemaphoreType.DMA((2,)),
        ],
        compiler_params=table_params,
        name="peer_up",
    )(idx.reshape(nb, PEER_TB * PEER_E), _pack_table(u_tab), h.reshape(n, d // LANES, LANES), gates)

    out = pl.pallas_call(
        _peer_down_kernel,
        grid=(nb,),
        in_specs=[
            pl.BlockSpec(memory_space=pl.ANY),
            pl.BlockSpec(memory_space=pl.ANY),
            pl.BlockSpec(memory_space=pl.ANY),
            pl.BlockSpec(tok_tiles, lambda i: (i, 0, 0)),
        ],
        out_specs=pl.BlockSpec(tok_tiles, lambda i: (i, 0, 0)),
        out_shape=jax.ShapeDtypeStruct((n, d // LANES, LANES), F32),
        scratch_shapes=[
            pltpu.VMEM((N_EXPERTS, ROW_SUB, LANES), jnp.int32),
            pltpu.SMEM((PEER_TB * PEER_E,), jnp.int32),
            pltpu.SMEM((PEER_TB * PEER_E,), F32),
            pltpu.SemaphoreType.DMA((3,)),
        ],
        compiler_params=table_params,
        name="peer_down",
    )(idx.reshape(nb, PEER_TB * PEER_E), w.reshape(nb, PEER_TB * PEER_E), _pack_table(v_tab),
      x.reshape(n, d // LANES, LANES))
    return out.reshape(n, d)


DSA_KT = 256
DSA_NEAR = 3
DSA_POS_BITS = 14
CHUNK_SHIFT = CHUNK.bit_length() - 1
INT_MIN = -2 ** 31
KEY_NEG_INF = 0x807FFFFF - 2 ** 32
_NT = (((1,), (1,)), ((), ()))


def _sort_key(x):
    b = pltpu.bitcast(x, jnp.int32)
    return b ^ ((b >> 31) & 0x7FFFFFFF)


def _dsa_kernel(qpad_ref, qi_ref, wit_ref, k_ref, vt_ref, ki_ref, nb_ref, o_ref,
                keys_scr, m_scr, l_scr, acc_scr, *, offset, n_keys, topk, qb):
    kt_ = DSA_KT
    q0 = offset + pl.program_id(1) * qb
    qpos = q0 + lax.broadcasted_iota(jnp.int32, (1, qb), 1)
    vis_end = jnp.minimum((lax.shift_right_logical(qpos, CHUNK_SHIFT) + 1) * CHUNK, n_keys)
    blk_end = jnp.minimum(((q0 + qb - 1) // CHUNK + 1) * CHUNK, n_keys)
    n_tiles = (blk_end + kt_ - 1) // kt_
    row = lax.broadcasted_iota(jnp.int32, (kt_, qb), 0)

    qi = qi_ref[0]
    qis = [qi[:, h * D_IDX:(h + 1) * D_IDX] for h in range(H_I)]
    wit = wit_ref[0] * (H_I ** -0.5)

    def score_tile(t, c):
        k0 = pl.multiple_of(t * kt_, kt_)
        kit = ki_ref[0, pl.ds(k0, kt_), :]
        idx = jnp.zeros((kt_, qb), F32)
        for h in range(H_I):
            s = lax.dot_general(kit, qis[h], _NT, preferred_element_type=F32)
            idx = idx + jnp.maximum(s * (D_IDX ** -0.5), 0.0) * wit[h:h + 1]
        idx = jnp.where(idx == 0.0, 0.0, idx)
        keys_scr[pl.ds(k0, kt_), :] = jnp.where(row + k0 < vis_end, _sort_key(idx), KEY_NEG_INF)
        return c

    lax.fori_loop(0, n_tiles, score_tile, 0)

    def count(pred):
        def body(t, acc):
            k0 = pl.multiple_of(t * kt_, kt_)
            hit = pred(keys_scr[pl.ds(k0, kt_), :], row + k0)
            return acc + jnp.sum(jnp.where(hit, 1.0, 0.0).reshape(kt_ // 8, 8, qb), axis=0)
        acc = lax.fori_loop(0, n_tiles, body, jnp.zeros((8, qb), F32))
        return jnp.sum(acc, axis=0, keepdims=True)

    kf = float(topk)
    c0 = count(lambda kt, kp: kt >= 0)
    thr0 = jnp.where(c0 >= kf, 0, INT_MIN).astype(jnp.int32)

    def thr_bit(j, thr):
        cand = thr | lax.shift_left(jnp.int32(1), 30 - j)
        c = count(lambda kt, kp: kt >= cand)
        return jnp.where(c >= kf, cand, thr)

    thr = lax.fori_loop(0, 31, thr_bit, thr0)
    need = kf - count(lambda kt, kp: kt > thr)

    def pos_bit(j, lo):
        cand = lo + lax.shift_left(jnp.int32(1), DSA_POS_BITS - 1 - j)
        c = count(lambda kt, kp: (kt == thr) & (kp < cand))
        return jnp.where(c < need, cand, lo)

    cut = lax.fori_loop(0, DSA_POS_BITS, pos_bit, jnp.zeros((1, qb), jnp.int32)) + 1
    select_all = vis_end <= topk
    thr = jnp.where(select_all, KEY_NEG_INF, thr)
    cut = jnp.where(select_all, 0, cut)

    m_scr[...] = jnp.full(m_scr.shape, NEG_INF, F32)
    l_scr[...] = jnp.zeros(l_scr.shape, F32)
    acc_scr[...] = jnp.zeros(acc_scr.shape, F32)

    def attend_tile(t, c):
        k0 = pl.multiple_of(t * kt_, kt_)
        keys = keys_scr[pl.ds(k0, kt_), :]
        sel = (keys > thr) | ((keys == thr) & (row + k0 < cut))
        step = (k0 - q0 + (DSA_NEAR - 1) * LANES) // LANES
        nidx = jnp.where(step < 0, DSA_NEAR, step)
        for h in range(H_B):
            kh = k_ref[0, pl.ds(k0, kt_), (h // 2) * LANES:(h // 2 + 1) * LANES]
            qh = qpad_ref[0, :, h * LANES:(h + 1) * LANES]
            s = lax.dot_general(kh, qh, _NT, preferred_element_type=F32) * (DH_B ** -0.5)
            s = jnp.where(sel, s + nb_ref[nidx, h], NEG_INF)
            m_old = m_scr[h:h + 1]
            m_new = jnp.maximum(m_old, jnp.max(s, axis=0, keepdims=True))
            m_safe = jnp.where(m_new == NEG_INF, 0.0, m_new)
            p = jnp.exp(s - m_safe)
            alpha = jnp.exp(m_old - m_safe)
            m_scr[h:h + 1] = m_new
            l_scr[h:h + 1] = alpha * l_scr[h:h + 1] + jnp.sum(p, axis=0, keepdims=True)
            vth = vt_ref[0, h * DH_B:(h + 1) * DH_B, pl.ds(k0, kt_)]
            acc_scr[h * DH_B:(h + 1) * DH_B] = (alpha * acc_scr[h * DH_B:(h + 1) * DH_B]
                                                + jnp.dot(vth, p.astype(BF16), preferred_element_type=F32))
        return c

    lax.fori_loop(0, n_tiles, attend_tile, 0)
    for h in range(H_B):
        o_ref[0, h * DH_B:(h + 1) * DH_B] = acc_scr[h * DH_B:(h + 1) * DH_B] / l_scr[h:h + 1]


def _near_bias_tiles(rel_bias, qb):
    k = jnp.arange(DSA_KT, dtype=jnp.int32)[:, None]
    q = jnp.arange(qb, dtype=jnp.int32)[None, :]
    rels = [(j - (DSA_NEAR - 1)) * LANES + k - q for j in range(DSA_NEAR)]
    rels.append(jnp.full((DSA_KT, qb), -(DSA_NEAR * LANES + DSA_KT), jnp.int32))
    tiles = rel_bias[_t5_bucket(jnp.stack(rels))]
    return jnp.moveaxis(tiles, -1, 1).astype(F32)


def dsa_attention(q, k_all, v_all, qi, wi, ki_all, rel_bias, offset):
    bx, t, _, _ = q.shape
    n_keys = k_all.shape[1]
    topk = min(TOPK_MAX, n_keys // 4)
    qb = Q_BLOCK if t % Q_BLOCK == 0 else t
    lp = _round_up(n_keys, DSA_KT)
    assert lp <= 2 ** DSA_POS_BITS and offset % DSA_KT == 0 and (qb == Q_BLOCK or t == qb)
    pad = ((0, 0), (0, lp - n_keys), (0, 0))
    half = jax.nn.one_hot(jnp.arange(H_B) % 2, 2, dtype=q.dtype)
    qpad = (q[:, :, :, None, :] * half[None, None, :, :, None]).reshape(bx, t, H_B * LANES).astype(BF16)
    kk = jnp.pad(k_all.reshape(bx, n_keys, D_B), pad).astype(BF16)
    vt = jnp.pad(v_all.reshape(bx, n_keys, D_B), pad).astype(BF16).transpose(0, 2, 1)
    ki = jnp.pad(ki_all, pad).astype(BF16)
    wit = jnp.pad(wi.astype(F32).transpose(0, 2, 1), ((0, 0), (0, 8 - H_I), (0, 0)))
    nb = _near_bias_tiles(rel_bias, qb)
    kern = functools.partial(_dsa_kernel, offset=offset, n_keys=n_keys, topk=topk, qb=qb)
    ot = pl.pallas_call(
        kern,
        grid=(bx, t // qb),
        in_specs=[
            pl.BlockSpec((1, qb, H_B * LANES), lambda b, i: (b, i, 0)),
            pl.BlockSpec((1, qb, H_I * D_IDX), lambda b, i: (b, i, 0)),
            pl.BlockSpec((1, 8, qb), lambda b, i: (b, 0, i)),
            pl.BlockSpec((1, lp, D_B), lambda b, i: (b, 0, 0)),
            pl.BlockSpec((1, D_B, lp), lambda b, i: (b, 0, 0)),
            pl.BlockSpec((1, lp, D_IDX), lambda b, i: (b, 0, 0)),
            pl.BlockSpec((DSA_NEAR + 1, H_B, DSA_KT, qb), lambda b, i: (0, 0, 0, 0)),
        ],
        out_specs=pl.BlockSpec((1, D_B, qb), lambda b, i: (b, 0, i)),
        out_shape=jax.ShapeDtypeStruct((bx, D_B, t), F32),
        scratch_shapes=[
            pltpu.VMEM((lp, qb), jnp.int32),
            pltpu.VMEM((8, qb), F32),
            pltpu.VMEM((8, qb), F32),
            pltpu.VMEM((D_B, qb), F32),
        ],
        compiler_params=pltpu.CompilerParams(
            dimension_semantics=("arbitrary", "arbitrary"), vmem_limit_bytes=PEER_TABLE_VMEM_LIMIT),
        name="dsa_attention",
    )(qpad, qi.reshape(bx, t, H_I * D_IDX).astype(BF16), wit, kk, vt, ki, nb)
    return ot.transpose(0, 2, 1)


def _rmsnorm(x, g):
    xf = x.astype(F32)
    y = xf * lax.rsqrt(jnp.mean(xf * xf, axis=-1, keepdims=True) + NORM_EPS)
    return (y * g).astype(x.dtype)


def _t5_bucket(rel):
    nb = REL_BUCKETS // 2
    ret = jnp.where(rel > 0, nb, 0)
    n = jnp.abs(rel)
    max_exact = nb // 2
    nf = jnp.maximum(n, 1).astype(F32)
    large = max_exact + (jnp.log(nf / max_exact) / math.log(REL_MAX_DIST / max_exact)
                         * (nb - max_exact)).astype(jnp.int32)
    large = jnp.minimum(large, nb - 1)
    return ret + jnp.where(n < max_exact, n, large)


def _wkv_scan(r, w, k, v, kk, a, s0):
    xs = tuple(jnp.moveaxis(t.astype(F32), 1, 0) for t in (r, w, k, v, kk, a))

    def step(S, inp):
        r_t, w_t, k_t, v_t, kk_t, a_t = inp
        sa = jnp.einsum('bhvk,bhk->bhv', S, -kk_t)
        S = (S * w_t[:, :, None, :] + sa[..., None] * (kk_t * a_t)[:, :, None, :]
             + v_t[..., None] * k_t[:, :, None, :])
        return S, jnp.einsum('bhvk,bhk->bhv', S, r_t)

    S, o = lax.scan(step, s0.astype(F32), xs)
    return jnp.moveaxis(o, 0, 1), S.astype(s0.dtype)


def _dsa_attend(q, k_all, v_all, qi, wi, ki_all, q_pos, k_pos, rel_bias, topk):
    Bx, T, H, Dh = q.shape
    qb = Q_BLOCK if T % Q_BLOCK == 0 else T
    nblk = T // qb

    def to_blocks(t):
        return jnp.moveaxis(t.reshape(Bx, nblk, qb, *t.shape[2:]), 1, 0)

    k_chunk = k_pos // CHUNK

    def block(args):
        qx, qix, wix, pos = args
        sc = jax.nn.relu(jnp.einsum('bqhd,bsd->bqhs', qix, ki_all).astype(F32) * D_IDX ** -0.5)
        idx = jnp.einsum('bqhs,bqh->bqs', sc, wix.astype(F32) * H_I ** -0.5)
        vis = k_chunk[None, :] <= (pos // CHUNK)[:, None]
        idx = jnp.where(vis[None], idx, -jnp.inf)
        top_s, top_i = lax.top_k(idx, topk)
        k_sel = jax.vmap(lambda kb, ib: kb[ib])(k_all, top_i)
        v_sel = jax.vmap(lambda vb, ib: vb[ib])(v_all, top_i)
        logit = jnp.einsum('bqhd,bqkhd->bqhk', qx, k_sel).astype(F32) * Dh ** -0.5
        rel = k_pos[top_i] - pos[None, :, None]
        bias = rel_bias[_t5_bucket(rel)]
        logit = logit + jnp.moveaxis(bias, -1, 2).astype(F32)
        logit = jnp.where(jnp.isfinite(top_s)[:, :, None, :], logit, -jnp.inf)
        p = jax.nn.softmax(logit, axis=-1)
        return jnp.einsum('bqhk,bqkhd->bqhd', p.astype(v_sel.dtype), v_sel)

    out = lax.map(block, (to_blocks(q), to_blocks(qi), to_blocks(wi), q_pos.reshape(nblk, qb)))
    return jnp.moveaxis(out, 0, 1).reshape(Bx, T, H, Dh)


def _even_mixer(x, shift_prev, wkv_prev, k_past, v_past, kidx_past,
                g_mix, w_in, mu, w0, w_du, a0, w_au, w_gu, k_k, k_a, r_k, gn_w, gn_b,
                qn_g, kn_g, rel_bias, w_out):
    Bx, T, _ = x.shape
    p = norm_linear(x.reshape(Bx * T, D_MODEL), g_mix, w_in).reshape(Bx, T, -1)
    pa, pb = p[..., :A_PROJ], p[..., A_PROJ:]
    pa_prev = jnp.concatenate([shift_prev[:, None, :].astype(pa.dtype), pa[:, :-1]], axis=1)
    xm = (pa + (pa_prev - pa) * mu).astype(F32)
    r, k, v, dw, da, dg = jnp.split(
        xm, [D_A, 2 * D_A, 3 * D_A, 3 * D_A + LORA_W, 3 * D_A + LORA_W + LORA_A], axis=-1)
    w_log = -jax.nn.softplus(-(w0 + jnp.tanh(dw) @ w_du)) - 0.5
    decay = jnp.exp(-jnp.exp(w_log))
    a = jax.nn.sigmoid(a0 + da @ w_au)
    gate = jax.nn.sigmoid(dg) @ w_gu
    kk = (k * k_k).reshape(Bx, T, H_A, DH_A)
    kk = kk / jnp.maximum(jnp.sqrt(jnp.sum(kk * kk, axis=-1, keepdims=True)), 1e-12)
    k = k * (1.0 + (a - 1.0) * k_a)
    r, k, v, decay, a = (t.reshape(Bx, T, H_A, DH_A) for t in (r, k, v, decay, a))
    o, wkv_new = _wkv_scan(r, decay, k, v, kk, a, wkv_prev)
    mo = jnp.mean(o, axis=-1, keepdims=True)
    vo = jnp.mean(jnp.square(o - mo), axis=-1, keepdims=True)
    o = ((o - mo) * lax.rsqrt(vo + GN_EPS)).reshape(Bx, T, D_A) * gn_w + gn_b
    o = o + (jnp.sum(r * k * r_k, axis=-1, keepdims=True) * v).reshape(Bx, T, D_A)
    o_a = (o * gate).astype(x.dtype)
    q, kb, vb, qi, ki, wi = jnp.split(
        pb, [D_B, 2 * D_B, 3 * D_B, 3 * D_B + H_I * D_IDX, 3 * D_B + H_I * D_IDX + D_IDX], axis=-1)
    q = _rmsnorm(q.reshape(Bx, T, H_B, DH_B), qn_g)
    kb = _rmsnorm(kb.reshape(Bx, T, H_B, DH_B), kn_g)
    vb = vb.reshape(Bx, T, H_B, DH_B)
    qi = qi.reshape(Bx, T, H_I, D_IDX)
    offset = k_past.shape[1]
    L = offset + T
    topk = min(TOPK_MAX, L // 4)
    k_all = jnp.concatenate([k_past.astype(kb.dtype), kb], axis=1)
    v_all = jnp.concatenate([v_past.astype(vb.dtype), vb], axis=1)
    ki_all = jnp.concatenate([kidx_past.astype(ki.dtype), ki], axis=1)
    q_pos = offset + jnp.arange(T, dtype=jnp.int32)
    k_pos = jnp.arange(L, dtype=jnp.int32)
    o_b = dsa_attention(q, k_all, v_all, qi, wi, ki_all, rel_bias, offset)
    cat = jnp.concatenate([o_a, o_b.astype(x.dtype)], axis=-1)
    x_new = linear_residual(cat.reshape(Bx * T, D_MODEL), w_out, None,
                            x.reshape(Bx * T, D_MODEL)).reshape(Bx, T, D_MODEL)
    return x_new, pa[:, -1], wkv_new, kb, vb, ki


def _conv_mixer(x, conv_prev, g_mix, w1, b1, w_dw, b_dw, ln_g, ln_b, w2, b2):
    Bx, T, _ = x.shape
    u = norm_linear(x.reshape(Bx * T, D_MODEL), g_mix, w1, b1).reshape(Bx, T, -1)
    glu = u[..., :D_CONV] * jax.nn.sigmoid(u[..., D_CONV:])
    padded = jnp.concatenate([conv_prev.astype(glu.dtype), glu], axis=1)
    y = lax.conv_general_dilated(padded, w_dw[:, None, :].astype(glu.dtype), (1,), 'VALID',
                                 dimension_numbers=('NWC', 'WIO', 'NWC'),
                                 feature_group_count=D_CONV) + b_dw
    yf = y.astype(F32)
    m = jnp.mean(yf, axis=-1, keepdims=True)
    var = jnp.mean(jnp.square(yf - m), axis=-1, keepdims=True)
    yn = jax.nn.silu((yf - m) * lax.rsqrt(var + 1e-5) * ln_g + ln_b)
    x_new = linear_residual(yn.reshape(Bx * T, D_CONV), w2, b2,
                            x.reshape(Bx * T, D_MODEL)).reshape(Bx, T, D_MODEL)
    return x_new, padded[:, -(CONV_W - 1):]


def _peer_both(xp, xs, g, wq, subkeys, u_tab, v_tab):
    np_ = xp.shape[0] * xp.shape[1]
    rows = jnp.concatenate([xp.reshape(np_, D_MODEL), xs.reshape(-1, D_MODEL)], axis=0)
    out = peer(rows, g, wq, subkeys, u_tab, v_tab)
    return out[:np_].reshape(xp.shape), out[np_:].reshape(xs.shape)


def kernel(x_prompt, x_sample, state_wkv, state_shift, cache_k, cache_v, cache_kidx, state_conv,
           norm_mix, norm_ffn, w_in, mu_shift, w0, w_decay_up, a0, w_iclr_up, w_gate_up,
           k_k, k_a, r_k, gn_w, gn_b, qn_g, kn_g, rel_bias, w_out,
           conv_w1, conv_b1, conv_dw, conv_bdw, conv_ln_g, conv_ln_b, conv_w2, conv_b2,
           peer_wq, peer_subkeys, peer_u, peer_v):
    xp, xs = x_prompt, x_sample
    Bp = xp.shape[0]
    dt = xp.dtype
    wkv_p, shift_p, k_p, v_p, kidx_p, conv_p = [], [], [], [], [], []
    wkv_s, shift_s, k_s, v_s, kidx_s, conv_s = [], [], [], [], [], []
    for li in range(DEPTH):
        if li % 2 == 0:
            e = li // 2
            prm = (norm_mix[li], w_in[e], mu_shift[e], w0[e], w_decay_up[e], a0[e], w_iclr_up[e],
                   w_gate_up[e], k_k[e], k_a[e], r_k[e], gn_w[e], gn_b[e], qn_g[e], kn_g[e],
                   rel_bias, w_out[e])
            xp, sh, wk, kb, vb, ki = _even_mixer(
                xp, jnp.zeros((Bp, A_PROJ), dt), jnp.zeros((Bp, H_A, DH_A, DH_A), dt),
                jnp.zeros((Bp, 0, H_B, DH_B), dt), jnp.zeros((Bp, 0, H_B, DH_B), dt),
                jnp.zeros((Bp, 0, D_IDX), dt), *prm)
            wkv_p.append(wk); shift_p.append(sh); k_p.append(kb); v_p.append(vb); kidx_p.append(ki)
            xs, sh, wk, kb, vb, ki = _even_mixer(
                xs, state_shift[e], state_wkv[e], cache_k[e], cache_v[e], cache_kidx[e], *prm)
            wkv_s.append(wk); shift_s.append(sh); k_s.append(kb); v_s.append(vb); kidx_s.append(ki)
        else:
            o = li // 2
            prm = (norm_mix[li], conv_w1[o], conv_b1[o], conv_dw[o], conv_bdw[o],
                   conv_ln_g[o], conv_ln_b[o], conv_w2[o], conv_b2[o])
            xp, cp = _conv_mixer(xp, jnp.zeros((Bp, CONV_W - 1, D_CONV), dt), *prm)
            xs, cs = _conv_mixer(xs, state_conv[o], *prm)
            conv_p.append(cp); conv_s.append(cs)
        pprm = (norm_ffn[li], peer_wq[li], peer_subkeys[li], peer_u[li], peer_v[li])
        xp, xs = _peer_both(xp, xs, *pprm)
    return (xp, xs,
            jnp.stack(wkv_p), jnp.stack(shift_p), jnp.stack(k_p), jnp.stack(v_p), jnp.stack(kidx_p), jnp.stack(conv_p),
            jnp.stack(wkv_s), jnp.stack(shift_s), jnp.stack(k_s), jnp.stack(v_s), jnp.stack(kidx_s), jnp.stack(conv_s))
```

```python
import functools
import math

import jax
import jax.numpy as jnp
from jax import lax
from jax.experimental import pallas as pl
from jax.experimental.pallas import tpu as pltpu

D_MODEL = 1024
DEPTH = 4
CHUNK = 64
NORM_EPS = 1e-6
D_A = D_MODEL // 2
DH_A = 64
H_A = D_A // DH_A
LORA_W = 64
LORA_A = 64
LORA_G = 128
A_PROJ = 3 * D_A + LORA_W + LORA_A + LORA_G
GN_EPS = 64e-5
D_B = D_MODEL // 2
DH_B = 64
H_B = D_B // DH_B
H_I = 4
D_IDX = 64
TOPK_MAX = 256
Q_BLOCK = 128
B_PROJ = 3 * D_B + H_I * D_IDX + D_IDX + H_I
REL_BUCKETS = 32
REL_MAX_DIST = 128
D_CONV = D_MODEL
CONV_W = 31
PEER_HEADS = 8
PEER_DK = 256
N_KEYS = 128
TOPK_HALF = 16
PEER_TOPK = 16
PEER_BLOCK = 512

LANES = 128
VMEM_LIMIT = 48 * 1024 * 1024
PEER_TABLE_VMEM_LIMIT = 56 * 1024 * 1024

F32 = jnp.float32
BF16 = jnp.bfloat16


def _round_up(n, m):
    return -(-n // m) * m


def _col_tile(m):
    for t in (1024, 896, 768, 640, 512, 384, 256, 128):
        if m % t == 0:
            return t
    raise ValueError(m)


def _norm_linear_kernel(x_ref, g_ref, w_ref, b_ref, o_ref, h_scr):
    @pl.when(pl.program_id(1) == 0)
    def _():
        x = x_ref[...]
        ms = jnp.mean(x * x, axis=-1, keepdims=True)
        h_scr[...] = (x * lax.rsqrt(ms + NORM_EPS) * g_ref[...]).astype(BF16)

    o_ref[...] = jnp.dot(h_scr[...], w_ref[...], preferred_element_type=F32) + b_ref[...]


def norm_linear(x, g, w, b=None, row_tile=512):
    n, d = x.shape
    m = w.shape[1]
    mp = _round_up(m, 2 * LANES)
    wb = jnp.pad(w.astype(BF16), ((0, 0), (0, mp - m)))
    bb = jnp.zeros((1, mp), F32) if b is None else jnp.pad(b.astype(F32), (0, mp - m))[None]
    tn = _col_tile(mp)
    tm = min(row_tile, n)
    assert n % tm == 0
    out = pl.pallas_call(
        _norm_linear_kernel,
        grid=(n // tm, mp // tn),
        in_specs=[
            pl.BlockSpec((tm, d), lambda i, j: (i, 0)),
            pl.BlockSpec((1, d), lambda i, j: (0, 0)),
            pl.BlockSpec((d, tn), lambda i, j: (0, j)),
            pl.BlockSpec((1, tn), lambda i, j: (0, j)),
        ],
        out_specs=pl.BlockSpec((tm, tn), lambda i, j: (i, j)),
        out_shape=jax.ShapeDtypeStruct((n, mp), F32),
        scratch_shapes=[pltpu.VMEM((tm, d), BF16)],
        compiler_params=pltpu.CompilerParams(
            dimension_semantics=("arbitrary", "arbitrary"), vmem_limit_bytes=VMEM_LIMIT),
    )(x, g[None].astype(F32), wb, bb)
    return out[:, :m] if mp != m else out


def _linear_residual_kernel(x_ref, w_ref, b_ref, r_ref, o_ref):
    o_ref[...] = (r_ref[...] + b_ref[...]
                  + jnp.dot(x_ref[...].astype(BF16), w_ref[...], preferred_element_type=F32))


def linear_residual(x, w, b, res, row_tile=512):
    n, k = x.shape
    m = w.shape[1]
    tm = min(row_tile, n)
    assert n % tm == 0 and m % LANES == 0
    bb = jnp.zeros((1, m), F32) if b is None else b.astype(F32)[None]
    return pl.pallas_call(
        _linear_residual_kernel,
        grid=(n // tm,),
        in_specs=[
            pl.BlockSpec((tm, k), lambda i: (i, 0)),
            pl.BlockSpec((k, m), lambda i: (0, 0)),
            pl.BlockSpec((1, m), lambda i: (0, 0)),
            pl.BlockSpec((tm, m), lambda i: (i, 0)),
        ],
        out_specs=pl.BlockSpec((tm, m), lambda i: (i, 0)),
        out_shape=jax.ShapeDtypeStruct((n, m), F32),
        compiler_params=pltpu.CompilerParams(
            dimension_semantics=("arbitrary",), vmem_limit_bytes=VMEM_LIMIT),
    )(x, w.astype(BF16), bb, res)


PEER_TB = 128
PEER_E = PEER_HEADS * PEER_TOPK
N_EXPERTS = N_KEYS * N_KEYS
ROW_WORDS = D_MODEL // 2
ROW_SUB = ROW_WORDS // LANES
PSTRIDE = PEER_E + 8
PEER_DOWN_CHUNK = 32
NEG_INF = float("-inf")


def _top_rows(s, k):
    n = s.shape[0]
    rows = lax.broadcasted_iota(jnp.int32, s.shape, 0).astype(F32)
    out_rows = lax.broadcasted_iota(jnp.int32, (k, s.shape[1]), 0)
    vals = jnp.zeros((k, s.shape[1]), F32)
    ids = jnp.zeros((k, s.shape[1]), F32)
    for it in range(k):
        m = jnp.max(s, axis=0, keepdims=True)
        first = jnp.min(jnp.where(s == m, rows, float(n)), axis=0, keepdims=True)
        vals = jnp.where(out_rows == it, m, vals)
        ids = jnp.where(out_rows == it, first, ids)
        s = jnp.where(rows == first, NEG_INF, s)
    return vals, ids


def _peer_select_kernel(x_ref, g_ref, wq_ref, sk_ref, h_ref, idx_ref, gate_ref):
    x = x_ref[...]
    ms = jnp.mean(x * x, axis=-1, keepdims=True)
    h = x * lax.rsqrt(ms + NORM_EPS) * g_ref[...]
    h_ref[...] = h
    q = jnp.dot(h.astype(BF16), wq_ref[...], preferred_element_type=F32).astype(BF16)
    half = PEER_DK // 2
    tb = x.shape[0]
    crow = lax.broadcasted_iota(jnp.int32, (TOPK_HALF * TOPK_HALF, tb), 0).astype(F32)
    orow = lax.broadcasted_iota(jnp.int32, (PEER_TOPK, tb), 0)
    ids = []
    for hd in range(PEER_HEADS):
        sv, si = [], []
        for p in range(2):
            c = (hd * 2 + p) * half
            s = lax.dot_general(sk_ref[hd * 2 + p], q[:, c:c + half],
                                (((1,), (1,)), ((), ())), preferred_element_type=F32)
            v, i = _top_rows(s, TOPK_HALF)
            sv.append(v)
            si.append(i)
        cand = jnp.concatenate([sv[0][i:i + 1] + sv[1] for i in range(TOPK_HALF)], axis=0)
        eid = jnp.concatenate([si[0][i:i + 1] * float(N_KEYS) + si[1] for i in range(TOPK_HALF)], axis=0)
        cs = jnp.zeros((PEER_TOPK, tb), F32)
        ce = jnp.zeros((PEER_TOPK, tb), F32)
        for it in range(PEER_TOPK):
            m = jnp.max(cand, axis=0, keepdims=True)
            first = jnp.min(jnp.where(cand == m, crow, float(crow.shape[0])), axis=0, keepdims=True)
            hit = crow == first
            e = jnp.max(jnp.where(hit, eid, -1.0), axis=0, keepdims=True)
            cs = jnp.where(orow == it, m, cs)
            ce = jnp.where(orow == it, e, ce)
            cand = jnp.where(hit, NEG_INF, cand)
        ex = jnp.exp(cs - cs[0:1])
        gates = ex / jnp.sum(ex, axis=0, keepdims=True)
        ids.append(ce)
        gate_ref[0, hd * PEER_TOPK:(hd + 1) * PEER_TOPK, :] = gates
    idx_ref[0] = jnp.concatenate(ids, axis=0).T.astype(jnp.int32)


def _unpack_row(row):
    lo = pltpu.bitcast(lax.shift_left(row, 16), F32)
    hi = pltpu.bitcast(jnp.bitwise_and(row, jnp.int32(-65536)), F32)
    return lo, hi


def _load_block_scalars(src_hbm, dst_smem, sem):
    cp = pltpu.make_async_copy(src_hbm.at[pl.program_id(0)], dst_smem, sem)
    cp.start()
    cp.wait()


def _load_table_once(tab_hbm, tab_vmem, sem):
    @pl.when(pl.program_id(0) == 0)
    def _():
        cp = pltpu.make_async_copy(tab_hbm, tab_vmem, sem)
        cp.start()
        cp.wait()


def _peer_up_kernel(idx_hbm, tab_hbm, h_ref, gate_ref, w_ref, tab, idx_s, pbuf_a, pbuf_b, act_s, sems):
    _load_table_once(tab_hbm, tab, sems.at[0])
    _load_block_scalars(idx_hbm, idx_s, sems.at[1])
    lane = lax.broadcasted_iota(jnp.int32, (PEER_E, PEER_TB), 1)
    act_s[...] = jnp.zeros((PEER_E, PEER_TB), F32)
    pbuf_b[...] = jnp.zeros(pbuf_b.shape, F32)

    def gather(t, pbuf):
        ht = h_ref[t]
        h_lo, h_hi = ht[0:ROW_SUB], ht[ROW_SUB:2 * ROW_SUB]
        base = t * PEER_E
        for e in range(PEER_E):
            lo, hi = _unpack_row(tab[idx_s[base + e]])
            pbuf[pl.ds(e, ROW_SUB, stride=PSTRIDE), :] = lo * h_lo + hi * h_hi

    def reduce(t, pbuf):
        acc = pbuf[0:PEER_E, :]
        for c in range(1, ROW_SUB):
            acc = acc + pbuf[c * PSTRIDE:c * PSTRIDE + PEER_E, :]
        col = jnp.sum(acc, axis=-1, keepdims=True)
        act_s[...] = jnp.where(lane == t, col, act_s[...])

    def token_pair(j, carry):
        reduce(2 * j - 1, pbuf_b)
        gather(2 * j, pbuf_a)
        reduce(2 * j, pbuf_a)
        gather(2 * j + 1, pbuf_b)
        return carry

    lax.fori_loop(0, PEER_TB // 2, token_pair, 0)
    reduce(PEER_TB - 1, pbuf_b)
    a = act_s[...]
    w_ref[0] = (gate_ref[0] * (0.5 * a * (1.0 + lax.erf(a * (2.0 ** -0.5))))).T


def _peer_down_kernel(idx_hbm, w_hbm, tab_hbm, x_ref, o_ref, tab, idx_s, w_s, sems):
    _load_table_once(tab_hbm, tab, sems.at[0])
    _load_block_scalars(idx_hbm, idx_s, sems.at[1])
    _load_block_scalars(w_hbm, w_s, sems.at[2])
    n_acc = 4
    zero = jnp.zeros((ROW_SUB, LANES), F32)

    def token(t, carry):
        def chunk(c, accs):
            acc_lo, acc_hi = list(accs[:n_acc]), list(accs[n_acc:])
            base = t * PEER_E + c * PEER_DOWN_CHUNK
            for j in range(PEER_DOWN_CHUNK):
                lo, hi = _unpack_row(tab[idx_s[base + j]])
                w = w_s[base + j]
                acc_lo[j % n_acc] = acc_lo[j % n_acc] + w * lo
                acc_hi[j % n_acc] = acc_hi[j % n_acc] + w * hi
            return tuple(acc_lo) + tuple(acc_hi)

        accs = lax.fori_loop(0, PEER_E // PEER_DOWN_CHUNK, chunk, (zero,) * (2 * n_acc))
        lo = (accs[0] + accs[1]) + (accs[2] + accs[3])
        hi = (accs[4] + accs[5]) + (accs[6] + accs[7])
        o_ref[t] = x_ref[t] + jnp.concatenate([lo, hi], axis=0)
        return carry

    lax.fori_loop(0, PEER_TB, token, 0)


def _pack_table(tab):
    n = tab.shape[0]
    bits = lax.bitcast_convert_type(tab.astype(jnp.bfloat16), jnp.uint16).astype(jnp.uint32)
    words = bits[:, :ROW_WORDS] | (bits[:, ROW_WORDS:] << 16)
    return lax.bitcast_convert_type(words, jnp.int32).reshape(n, ROW_SUB, LANES)


def peer(x, g, wq, subkeys, u_tab, v_tab):
    n, d = x.shape
    assert n % PEER_TB == 0 and d == D_MODEL
    nb = n // PEER_TB
    sk = subkeys.reshape(PEER_HEADS * 2, N_KEYS, PEER_DK // 2).astype(BF16)
    params = pltpu.CompilerParams(dimension_semantics=("arbitrary",), vmem_limit_bytes=VMEM_LIMIT)
    h, idx, gates = pl.pallas_call(
        _peer_select_kernel,
        grid=(nb,),
        in_specs=[
            pl.BlockSpec((PEER_TB, d), lambda i: (i, 0)),
            pl.BlockSpec((1, d), lambda i: (0, 0)),
            pl.BlockSpec((d, PEER_HEADS * PEER_DK), lambda i: (0, 0)),
            pl.BlockSpec((PEER_HEADS * 2, N_KEYS, PEER_DK // 2), lambda i: (0, 0, 0)),
        ],
        out_specs=[
            pl.BlockSpec((PEER_TB, d), lambda i: (i, 0)),
            pl.BlockSpec((1, PEER_TB, PEER_E), lambda i: (i, 0, 0)),
            pl.BlockSpec((1, PEER_E, PEER_TB), lambda i: (i, 0, 0)),
        ],
        out_shape=[
            jax.ShapeDtypeStruct((n, d), F32),
            jax.ShapeDtypeStruct((nb, PEER_TB, PEER_E), jnp.int32),
            jax.ShapeDtypeStruct((nb, PEER_E, PEER_TB), F32),
        ],
        compiler_params=params,
        name="peer_select",
    )(x, g[None].astype(F32), wq.astype(BF16), sk)

    table_params = pltpu.CompilerParams(dimension_semantics=("arbitrary",),
                                        vmem_limit_bytes=PEER_TABLE_VMEM_LIMIT)
    tok_tiles = (PEER_TB, d // LANES, LANES)
    w = pl.pallas_call(
        _peer_up_kernel,
        grid=(nb,),
        in_specs=[
            pl.BlockSpec(memory_space=pl.ANY),
            pl.BlockSpec(memory_space=pl.ANY),
            pl.BlockSpec(tok_tiles, lambda i: (i, 0, 0)),
            pl.BlockSpec((1, PEER_E, PEER_TB), lambda i: (i, 0, 0)),
        ],
        out_specs=pl.BlockSpec((1, PEER_TB, PEER_E), lambda i: (i, 0, 0)),
        out_shape=jax.ShapeDtypeStruct((nb, PEER_TB, PEER_E), F32),
        scratch_shapes=[
            pltpu.VMEM((N_EXPERTS, ROW_SUB, LANES), jnp.int32),
            pltpu.SMEM((PEER_TB * PEER_E,), jnp.int32),
            pltpu.VMEM((ROW_SUB * PSTRIDE, LANES), F32),
            pltpu.VMEM((ROW_SUB * PSTRIDE, LANES), F32),
            pltpu.VMEM((PEER_E, PEER_TB), F32),
            pltpu.SemaphoreType.DMA((2,)),
        ],
        compiler_params=table_params,
        name="peer_up",
    )(idx.reshape(nb, PEER_TB * PEER_E), _pack_table(u_tab), h.reshape(n, d // LANES, LANES), gates)

    out = pl.pallas_call(
        _peer_down_kernel,
        grid=(nb,),
        in_specs=[
            pl.BlockSpec(memory_space=pl.ANY),
            pl.BlockSpec(memory_space=pl.ANY),
            pl.BlockSpec(memory_space=pl.ANY),
            pl.BlockSpec(tok_tiles, lambda i: (i, 0, 0)),
        ],
        out_specs=pl.BlockSpec(tok_tiles, lambda i: (i, 0, 0)),
        out_shape=jax.ShapeDtypeStruct((n, d // LANES, LANES), F32),
        scratch_shapes=[
            pltpu.VMEM((N_EXPERTS, ROW_SUB, LANES), jnp.int32),
            pltpu.SMEM((PEER_TB * PEER_E,), jnp.int32),
            pltpu.SMEM((PEER_TB * PEER_E,), F32),
            pltpu.SemaphoreType.DMA((3,)),
        ],
        compiler_params=table_params,
        name="peer_down",
    )(idx.reshape(nb, PEER_TB * PEER_E), w.reshape(nb, PEER_TB * PEER_E), _pack_table(v_tab),
      x.reshape(n, d // LANES, LANES))
    return out.reshape(n, d)


DSA_KT = 256
DSA_NEAR = 3
DSA_POS_BITS = 14
CHUNK_SHIFT = CHUNK.bit_length() - 1
INT_MIN = -2 ** 31
KEY_NEG_INF = 0x807FFFFF - 2 ** 32
_NT = (((1,), (1,)), ((), ()))


def _sort_key(x):
    b = pltpu.bitcast(x, jnp.int32)
    return b ^ ((b >> 31) & 0x7FFFFFFF)


def _dsa_kernel(qpad_ref, qi_ref, wit_ref, k_ref, vt_ref, ki_ref, nb_ref, o_ref,
                keys_scr, m_scr, l_scr, acc_scr, *, offset, n_keys, topk, qb):
    kt_ = DSA_KT
    q0 = offset + pl.program_id(1) * qb
    qpos = q0 + lax.broadcasted_iota(jnp.int32, (1, qb), 1)
    vis_end = jnp.minimum((lax.shift_right_logical(qpos, CHUNK_SHIFT) + 1) * CHUNK, n_keys)
    blk_end = jnp.minimum(((q0 + qb - 1) // CHUNK + 1) * CHUNK, n_keys)
    n_tiles = (blk_end + kt_ - 1) // kt_
    row = lax.broadcasted_iota(jnp.int32, (kt_, qb), 0)

    qi = qi_ref[0]
    qis = [qi[:, h * D_IDX:(h + 1) * D_IDX] for h in range(H_I)]
    wit = wit_ref[0] * (H_I ** -0.5)

    def score_tile(t, c):
        k0 = pl.multiple_of(t * kt_, kt_)
        kit = ki_ref[0, pl.ds(k0, kt_), :]
        idx = jnp.zeros((kt_, qb), F32)
        for h in range(H_I):
            s = lax.dot_general(kit, qis[h], _NT, preferred_element_type=F32)
            idx = idx + jnp.maximum(s * (D_IDX ** -0.5), 0.0) * wit[h:h + 1]
        idx = jnp.where(idx == 0.0, 0.0, idx)
        keys_scr[pl.ds(k0, kt_), :] = jnp.where(row + k0 < vis_end, _sort_key(idx), KEY_NEG_INF)
        return c

    lax.fori_loop(0, n_tiles, score_tile, 0)

    def count(pred):
        def body(t, acc):
            k0 = pl.multiple_of(t * kt_, kt_)
            hit = pred(keys_scr[pl.ds(k0, kt_), :], row + k0)
            return acc + jnp.sum(jnp.where(hit, 1.0, 0.0).reshape(kt_ // 8, 8, qb), axis=0)
        acc = lax.fori_loop(0, n_tiles, body, jnp.zeros((8, qb), F32))
        return jnp.sum(acc, axis=0, keepdims=True)

    kf = float(topk)
    c0 = count(lambda kt, kp: kt >= 0)
    thr0 = jnp.where(c0 >= kf, 0, INT_MIN).astype(jnp.int32)

    def thr_bit(j, thr):
        cand = thr | lax.shift_left(jnp.int32(1), 30 - j)
        c = count(lambda kt, kp: kt >= cand)
        return jnp.where(c >= kf, cand, thr)

    thr = lax.fori_loop(0, 31, thr_bit, thr0)
    need = kf - count(lambda kt, kp: kt > thr)

    def pos_bit(j, lo):
        cand = lo + lax.shift_left(jnp.int32(1), DSA_POS_BITS - 1 - j)
        c = count(lambda kt, kp: (kt == thr) & (kp < cand))
        return jnp.where(c < need, cand, lo)

    cut = lax.fori_loop(0, DSA_POS_BITS, pos_bit, jnp.zeros((1, qb), jnp.int32)) + 1
    select_all = vis_end <= topk
    thr = jnp.where(select_all, KEY_NEG_INF, thr)
    cut = jnp.where(select_all, 0, cut)

    m_scr[...] = jnp.full(m_scr.shape, NEG_INF, F32)
    l_scr[...] = jnp.zeros(l_scr.shape, F32)
    acc_scr[...] = jnp.zeros(acc_scr.shape, F32)

    def attend_tile(t, c):
        k0 = pl.multiple_of(t * kt_, kt_)
        keys = keys_scr[pl.ds(k0, kt_), :]
        sel = (keys > thr) | ((keys == thr) & (row + k0 < cut))
        step = (k0 - q0 + (DSA_NEAR - 1) * LANES) // LANES
        nidx = jnp.where(step < 0, DSA_NEAR, step)
        for h in range(H_B):
            kh = k_ref[0, pl.ds(k0, kt_), (h // 2) * LANES:(h // 2 + 1) * LANES]
            qh = qpad_ref[0, :, h * LANES:(h + 1) * LANES]
            s = lax.dot_general(kh, qh, _NT, preferred_element_type=F32) * (DH_B ** -0.5)
            s = jnp.where(sel, s + nb_ref[nidx, h], NEG_INF)
            m_old = m_scr[h:h + 1]
            m_new = jnp.maximum(m_old, jnp.max(s, axis=0, keepdims=True))
            m_safe = jnp.where(m_new == NEG_INF, 0.0, m_new)
            p = jnp.exp(s - m_safe)
            alpha = jnp.exp(m_old - m_safe)
            m_scr[h:h + 1] = m_new
            l_scr[h:h + 1] = alpha * l_scr[h:h + 1] + jnp.sum(p, axis=0, keepdims=True)
            vth = vt_ref[0, h * DH_B:(h + 1) * DH_B, pl.ds(k0, kt_)]
            acc_scr[h * DH_B:(h + 1) * DH_B] = (alpha * acc_scr[h * DH_B:(h + 1) * DH_B]
                                                + jnp.dot(vth, p.astype(BF16), preferred_element_type=F32))
        return c

    lax.fori_loop(0, n_tiles, attend_tile, 0)
    for h in range(H_B):
        o_ref[0, h * DH_B:(h + 1) * DH_B] = acc_scr[h * DH_B:(h + 1) * DH_B] / l_scr[h:h + 1]


def _near_bias_tiles(rel_bias, qb):
    k = jnp.arange(DSA_KT, dtype=jnp.int32)[:, None]
    q = jnp.arange(qb, dtype=jnp.int32)[None, :]
    rels = [(j - (DSA_NEAR - 1)) * LANES + k - q for j in range(DSA_NEAR)]
    rels.append(jnp.full((DSA_KT, qb), -(DSA_NEAR * LANES + DSA_KT), jnp.int32))
    tiles = rel_bias[_t5_bucket(jnp.stack(rels))]
    return jnp.moveaxis(tiles, -1, 1).astype(F32)


def dsa_attention(q, k_all, v_all, qi, wi, ki_all, rel_bias, offset):
    bx, t, _, _ = q.shape
    n_keys = k_all.shape[1]
    topk = min(TOPK_MAX, n_keys // 4)
    qb = Q_BLOCK if t % Q_BLOCK == 0 else t
    lp = _round_up(n_keys, DSA_KT)
    assert lp <= 2 ** DSA_POS_BITS and offset % DSA_KT == 0 and (qb == Q_BLOCK or t == qb)
    pad = ((0, 0), (0, lp - n_keys), (0, 0))
    half = jax.nn.one_hot(jnp.arange(H_B) % 2, 2, dtype=q.dtype)
    qpad = (q[:, :, :, None, :] * half[None, None, :, :, None]).reshape(bx, t, H_B * LANES).astype(BF16)
    kk = jnp.pad(k_all.reshape(bx, n_keys, D_B), pad).astype(BF16)
    vt = jnp.pad(v_all.reshape(bx, n_keys, D_B), pad).astype(BF16).transpose(0, 2, 1)
    ki = jnp.pad(ki_all, pad).astype(BF16)
    wit = jnp.pad(wi.astype(F32).transpose(0, 2, 1), ((0, 0), (0, 8 - H_I), (0, 0)))
    nb = _near_bias_tiles(rel_bias, qb)
    kern = functools.partial(_dsa_kernel, offset=offset, n_keys=n_keys, topk=topk, qb=qb)
    ot = pl.pallas_call(
        kern,
        grid=(bx, t // qb),
        in_specs=[
            pl.BlockSpec((1, qb, H_B * LANES), lambda b, i: (b, i, 0)),
            pl.BlockSpec((1, qb, H_I * D_IDX), lambda b, i: (b, i, 0)),
            pl.BlockSpec((1, 8, qb), lambda b, i: (b, 0, i)),
            pl.BlockSpec((1, lp, D_B), lambda b, i: (b, 0, 0)),
            pl.BlockSpec((1, D_B, lp), lambda b, i: (b, 0, 0)),
            pl.BlockSpec((1, lp, D_IDX), lambda b, i: (b, 0, 0)),
            pl.BlockSpec((DSA_NEAR + 1, H_B, DSA_KT, qb), lambda b, i: (0, 0, 0, 0)),
        ],
        out_specs=pl.BlockSpec((1, D_B, qb), lambda b, i: (b, 0, i)),
        out_shape=jax.ShapeDtypeStruct((bx, D_B, t), F32),
        scratch_shapes=[
            pltpu.VMEM((lp, qb), jnp.int32),
            pltpu.VMEM((8, qb), F32),
            pltpu.VMEM((8, qb), F32),
            pltpu.VMEM((D_B, qb), F32),
        ],
        compiler_params=pltpu.CompilerParams(
            dimension_semantics=("arbitrary", "arbitrary"), vmem_limit_bytes=PEER_TABLE_VMEM_LIMIT),
        name="dsa_attention",
    )(qpad, qi.reshape(bx, t, H_I * D_IDX).astype(BF16), wit, kk, vt, ki, nb)
    return ot.transpose(0, 2, 1)


WKV_CHUNK = 64
WKV_BLOCK = 256
_TN = (((0,), (0,)), ((), ()))


def _wkv_kernel(r_ref, lw_ref, k_ref, v_ref, kk_ref, a_ref, s0_ref, o_ref, st_ref, s_scr, *, chunk, n_chunks):
    c = chunk

    @pl.when(pl.program_id(1) == 0)
    def _():
        s_scr[...] = s0_ref[0]

    row = lax.broadcasted_iota(jnp.int32, (c, c), 0)
    col = lax.broadcasted_iota(jnp.int32, (c, c), 1)
    strict, incl = row > col, row >= col
    tri = jnp.where(incl, 1.0, 0.0)
    eye = jnp.where(row == col, 1.0, 0.0)
    bdot = lambda x, y: jnp.dot(x.astype(BF16), y.astype(BF16), preferred_element_type=F32)
    bdot_nt = lambda x, y: lax.dot_general(x.astype(BF16), y.astype(BF16), _NT, preferred_element_type=F32)
    bdot_tn = lambda x, y: lax.dot_general(x.astype(BF16), y.astype(BF16), _TN, preferred_element_type=F32)

    def solve_chunk(ci, carry):
        c0 = pl.multiple_of(ci * c, c)
        rows = pl.ds(c0, c)
        outs = []
        for h in range(H_A):
            hs = slice(h * DH_A, (h + 1) * DH_A)
            r, lw, k, v = r_ref[0, rows, hs], lw_ref[0, rows, hs], k_ref[0, rows, hs], v_ref[0, rows, hs]
            kap, a = kk_ref[0, rows, hs], a_ref[0, rows, hs]
            cum = jnp.dot(tri, lw, preferred_element_type=F32, precision=lax.Precision.HIGHEST)
            p, p_inv, p_prev = jnp.exp(cum), jnp.exp(-cum), jnp.exp(cum - lw)
            kap_t, r_t, k_b, b_b = kap * p_prev, r * p, k * p_inv, kap * a * p_inv
            g = bdot_nt(jnp.concatenate([kap_t, r_t], axis=0), jnp.concatenate([k_b, b_b], axis=0))
            a_kk, a_kb, a_rk, a_rb = g[:c, :c], g[:c, c:], g[c:, :c], g[c:, c:]
            nil = jnp.where(strict, -a_kb, 0.0)
            inv, power = eye + nil, nil
            for _ in range(c.bit_length() - 2):
                power = bdot(power, power)
                inv = inv + bdot(inv, power)
            s_prev = s_scr[h]
            u = bdot(inv, bdot_nt(kap_t, s_prev) + bdot(jnp.where(strict, a_kk, 0.0), v))
            outs.append(bdot_nt(r_t, s_prev) + bdot(jnp.where(incl, a_rk, 0.0), v)
                        - bdot(jnp.where(incl, a_rb, 0.0), u))
            s_scr[h] = (s_prev + bdot_tn(v, k_b) - bdot_tn(u, b_b)) * p[c - 1:c, :]
        o_ref[0, rows, :] = jnp.concatenate(outs, axis=1)
        return carry

    lax.fori_loop(0, n_chunks, solve_chunk, 0)

    @pl.when(pl.program_id(1) == pl.num_programs(1) - 1)
    def _():
        st_ref[0] = s_scr[...]


def wkv_chunked(r, lw, k, v, kk, a, s0):
    bx, t, _ = r.shape
    chunk = min(WKV_CHUNK, t)
    blk = min(WKV_BLOCK, t)
    assert t % blk == 0 and blk % chunk == 0 and chunk & (chunk - 1) == 0
    tok = pl.BlockSpec((1, blk, D_A), lambda b, i: (b, i, 0))
    state = pl.BlockSpec((1, H_A, DH_A, DH_A), lambda b, i: (b, 0, 0, 0))
    return pl.pallas_call(
        functools.partial(_wkv_kernel, chunk=chunk, n_chunks=blk // chunk),
        grid=(bx, t // blk),
        in_specs=[tok] * 6 + [state],
        out_specs=[tok, state],
        out_shape=[jax.ShapeDtypeStruct((bx, t, D_A), F32),
                   jax.ShapeDtypeStruct((bx, H_A, DH_A, DH_A), F32)],
        scratch_shapes=[pltpu.VMEM((H_A, DH_A, DH_A), F32)],
        compiler_params=pltpu.CompilerParams(
            dimension_semantics=("arbitrary", "arbitrary"), vmem_limit_bytes=VMEM_LIMIT),
        name="wkv_chunked",
    )(r, lw, k, v, kk, a, s0.astype(F32))


def _rmsnorm(x, g):
    xf = x.astype(F32)
    y = xf * lax.rsqrt(jnp.mean(xf * xf, axis=-1, keepdims=True) + NORM_EPS)
    return (y * g).astype(x.dtype)


def _t5_bucket(rel):
    nb = REL_BUCKETS // 2
    ret = jnp.where(rel > 0, nb, 0)
    n = jnp.abs(rel)
    max_exact = nb // 2
    nf = jnp.maximum(n, 1).astype(F32)
    large = max_exact + (jnp.log(nf / max_exact) / math.log(REL_MAX_DIST / max_exact)
                         * (nb - max_exact)).astype(jnp.int32)
    large = jnp.minimum(large, nb - 1)
    return ret + jnp.where(n < max_exact, n, large)


def _wkv_scan(r, w, k, v, kk, a, s0):
    xs = tuple(jnp.moveaxis(t.astype(F32), 1, 0) for t in (r, w, k, v, kk, a))

    def step(S, inp):
        r_t, w_t, k_t, v_t, kk_t, a_t = inp
        sa = jnp.einsum('bhvk,bhk->bhv', S, -kk_t)
        S = (S * w_t[:, :, None, :] + sa[..., None] * (kk_t * a_t)[:, :, None, :]
             + v_t[..., None] * k_t[:, :, None, :])
        return S, jnp.einsum('bhvk,bhk->bhv', S, r_t)

    S, o = lax.scan(step, s0.astype(F32), xs)
    return jnp.moveaxis(o, 0, 1), S.astype(s0.dtype)


def _dsa_attend(q, k_all, v_all, qi, wi, ki_all, q_pos, k_pos, rel_bias, topk):
    Bx, T, H, Dh = q.shape
    qb = Q_BLOCK if T % Q_BLOCK == 0 else T
    nblk = T // qb

    def to_blocks(t):
        return jnp.moveaxis(t.reshape(Bx, nblk, qb, *t.shape[2:]), 1, 0)

    k_chunk = k_pos // CHUNK

    def block(args):
        qx, qix, wix, pos = args
        sc = jax.nn.relu(jnp.einsum('bqhd,bsd->bqhs', qix, ki_all).astype(F32) * D_IDX ** -0.5)
        idx = jnp.einsum('bqhs,bqh->bqs', sc, wix.astype(F32) * H_I ** -0.5)
        vis = k_chunk[None, :] <= (pos // CHUNK)[:, None]
        idx = jnp.where(vis[None], idx, -jnp.inf)
        top_s, top_i = lax.top_k(idx, topk)
        k_sel = jax.vmap(lambda kb, ib: kb[ib])(k_all, top_i)
        v_sel = jax.vmap(lambda vb, ib: vb[ib])(v_all, top_i)
        logit = jnp.einsum('bqhd,bqkhd->bqhk', qx, k_sel).astype(F32) * Dh ** -0.5
        rel = k_pos[top_i] - pos[None, :, None]
        bias = rel_bias[_t5_bucket(rel)]
        logit = logit + jnp.moveaxis(bias, -1, 2).astype(F32)
        logit = jnp.where(jnp.isfinite(top_s)[:, :, None, :], logit, -jnp.inf)
        p = jax.nn.softmax(logit, axis=-1)
        return jnp.einsum('bqhk,bqkhd->bqhd', p.astype(v_sel.dtype), v_sel)

    out = lax.map(block, (to_blocks(q), to_blocks(qi), to_blocks(wi), q_pos.reshape(nblk, qb)))
    return jnp.moveaxis(out, 0, 1).reshape(Bx, T, H, Dh)


def _even_mixer(x, shift_prev, wkv_prev, k_past, v_past, kidx_past,
                g_mix, w_in, mu, w0, w_du, a0, w_au, w_gu, k_k, k_a, r_k, gn_w, gn_b,
                qn_g, kn_g, rel_bias, w_out):
    Bx, T, _ = x.shape
    p = norm_linear(x.reshape(Bx * T, D_MODEL), g_mix, w_in).reshape(Bx, T, -1)
    pa, pb = p[..., :A_PROJ], p[..., A_PROJ:]
    pa_prev = jnp.concatenate([shift_prev[:, None, :].astype(pa.dtype), pa[:, :-1]], axis=1)
    xm = (pa + (pa_prev - pa) * mu).astype(F32)
    r, k, v, dw, da, dg = jnp.split(
        xm, [D_A, 2 * D_A, 3 * D_A, 3 * D_A + LORA_W, 3 * D_A + LORA_W + LORA_A], axis=-1)
    w_log = -jax.nn.softplus(-(w0 + jnp.tanh(dw) @ w_du)) - 0.5
    log_decay = -jnp.exp(w_log)
    a = jax.nn.sigmoid(a0 + da @ w_au)
    gate = jax.nn.sigmoid(dg) @ w_gu
    kk = (k * k_k).reshape(Bx, T, H_A, DH_A)
    kk = kk / jnp.maximum(jnp.sqrt(jnp.sum(kk * kk, axis=-1, keepdims=True)), 1e-12)
    k = k * (1.0 + (a - 1.0) * k_a)
    o, wkv_new = wkv_chunked(r, log_decay, k, v, kk.reshape(Bx, T, D_A), a, wkv_prev)
    r, k, v, o = (t.reshape(Bx, T, H_A, DH_A) for t in (r, k, v, o))
    mo = jnp.mean(o, axis=-1, keepdims=True)
    vo = jnp.mean(jnp.square(o - mo), axis=-1, keepdims=True)
    o = ((o - mo) * lax.rsqrt(vo + GN_EPS)).reshape(Bx, T, D_A) * gn_w + gn_b
    o = o + (jnp.sum(r * k * r_k, axis=-1, keepdims=True) * v).reshape(Bx, T, D_A)
    o_a = (o * gate).astype(x.dtype)
    q, kb, vb, qi, ki, wi = jnp.split(
        pb, [D_B, 2 * D_B, 3 * D_B, 3 * D_B + H_I * D_IDX, 3 * D_B + H_I * D_IDX + D_IDX], axis=-1)
    q = _rmsnorm(q.reshape(Bx, T, H_B, DH_B), qn_g)
    kb = _rmsnorm(kb.reshape(Bx, T, H_B, DH_B), kn_g)
    vb = vb.reshape(Bx, T, H_B, DH_B)
    qi = qi.reshape(Bx, T, H_I, D_IDX)
    offset = k_past.shape[1]
    L = offset + T
    topk = min(TOPK_MAX, L // 4)
    k_all = jnp.concatenate([k_past.astype(kb.dtype), kb], axis=1)
    v_all = jnp.concatenate([v_past.astype(vb.dtype), vb], axis=1)
    ki_all = jnp.concatenate([kidx_past.astype(ki.dtype), ki], axis=1)
    q_pos = offset + jnp.arange(T, dtype=jnp.int32)
    k_pos = jnp.arange(L, dtype=jnp.int32)
    o_b = dsa_attention(q, k_all, v_all, qi, wi, ki_all, rel_bias, offset)
    cat = jnp.concatenate([o_a, o_b.astype(x.dtype)], axis=-1)
    x_new = linear_residual(cat.reshape(Bx * T, D_MODEL), w_out, None,
                            x.reshape(Bx * T, D_MODEL)).reshape(Bx, T, D_MODEL)
    return x_new, pa[:, -1], wkv_new, kb, vb, ki


def _conv_mixer(x, conv_prev, g_mix, w1, b1, w_dw, b_dw, ln_g, ln_b, w2, b2):
    Bx, T, _ = x.shape
    u = norm_linear(x.reshape(Bx * T, D_MODEL), g_mix, w1, b1).reshape(Bx, T, -1)
    glu = u[..., :D_CONV] * jax.nn.sigmoid(u[..., D_CONV:])
    padded = jnp.concatenate([conv_prev.astype(glu.dtype), glu], axis=1)
    y = lax.conv_general_dilated(padded, w_dw[:, None, :].astype(glu.dtype), (1,), 'VALID',
                                 dimension_numbers=('NWC', 'WIO', 'NWC'),
                                 feature_group_count=D_CONV) + b_dw
    yf = y.astype(F32)
    m = jnp.mean(yf, axis=-1, keepdims=True)
    var = jnp.mean(jnp.square(yf - m), axis=-1, keepdims=True)
    yn = jax.nn.silu((yf - m) * lax.rsqrt(var + 1e-5) * ln_g + ln_b)
    x_new = linear_residual(yn.reshape(Bx * T, D_CONV), w2, b2,
                            x.reshape(Bx * T, D_MODEL)).reshape(Bx, T, D_MODEL)
    return x_new, padded[:, -(CONV_W - 1):]


def _peer_both(xp, xs, g, wq, subkeys, u_tab, v_tab):
    np_ = xp.shape[0] * xp.shape[1]
    rows = jnp.concatenate([xp.reshape(np_, D_MODEL), xs.reshape(-1, D_MODEL)], axis=0)
    out = peer(rows, g, wq, subkeys, u_tab, v_tab)
    return out[:np_].reshape(xp.shape), out[np_:].reshape(xs.shape)


def kernel(x_prompt, x_sample, state_wkv, state_shift, cache_k, cache_v, cache_kidx, state_conv,
           norm_mix, norm_ffn, w_in, mu_shift, w0, w_decay_up, a0, w_iclr_up, w_gate_up,
           k_k, k_a, r_k, gn_w, gn_b, qn_g, kn_g, rel_bias, w_out,
           conv_w1, conv_b1, conv_dw, conv_bdw, conv_ln_g, conv_ln_b, conv_w2, conv_b2,
           peer_wq, peer_subkeys, peer_u, peer_v):
    xp, xs = x_prompt, x_sample
    Bp = xp.shape[0]
    dt = xp.dtype
    wkv_p, shift_p, k_p, v_p, kidx_p, conv_p = [], [], [], [], [], []
    wkv_s, shift_s, k_s, v_s, kidx_s, conv_s = [], [], [], [], [], []
    for li in range(DEPTH):
        if li % 2 == 0:
            e = li // 2
            prm = (norm_mix[li], w_in[e], mu_shift[e], w0[e], w_decay_up[e], a0[e], w_iclr_up[e],
                   w_gate_up[e], k_k[e], k_a[e], r_k[e], gn_w[e], gn_b[e], qn_g[e], kn_g[e],
                   rel_bias, w_out[e])
            xp, sh, wk, kb, vb, ki = _even_mixer(
                xp, jnp.zeros((Bp, A_PROJ), dt), jnp.zeros((Bp, H_A, DH_A, DH_A), dt),
                jnp.zeros((Bp, 0, H_B, DH_B), dt), jnp.zeros((Bp, 0, H_B, DH_B), dt),
                jnp.zeros((Bp, 0, D_IDX), dt), *prm)
            wkv_p.append(wk); shift_p.append(sh); k_p.append(kb); v_p.append(vb); kidx_p.append(ki)
            xs, sh, wk, kb, vb, ki = _even_mixer(
                xs, state_shift[e], state_wkv[e], cache_k[e], cache_v[e], cache_kidx[e], *prm)
            wkv_s.append(wk); shift_s.append(sh); k_s.append(kb); v_s.append(vb); kidx_s.append(ki)
        else:
            o = li // 2
            prm = (norm_mix[li], conv_w1[o], conv_b1[o], conv_dw[o], conv_bdw[o],
                   conv_ln_g[o], conv_ln_b[o], conv_w2[o], conv_b2[o])
            xp, cp = _conv_mixer(xp, jnp.zeros((Bp, CONV_W - 1, D_CONV), dt), *prm)
            xs, cs = _conv_mixer(xs, state_conv[o], *prm)
            conv_p.append(cp); conv_s.append(cs)
        pprm = (norm_ffn[li], peer_wq[li], peer_subkeys[li], peer_u[li], peer_v[li])
        xp, xs = _peer_both(xp, xs, *pprm)
    return (xp, xs,
            jnp.stack(wkv_p), jnp.stack(shift_p), jnp.stack(k_p), jnp.stack(v_p), jnp.stack(kidx_p), jnp.stack(conv_p),
            jnp.stack(wkv_s), jnp.stack(shift_s), jnp.stack(k_s), jnp.stack(v_s), jnp.stack(kidx_s), jnp.stack(conv_s))
```

```python
import functools
import math

import jax
import jax.numpy as jnp
from jax import lax
from jax.experimental import pallas as pl
from jax.experimental.pallas import tpu as pltpu

D_MODEL = 1024
DEPTH = 4
CHUNK = 64
NORM_EPS = 1e-6
D_A = D_MODEL // 2
DH_A = 64
H_A = D_A // DH_A
LORA_W = 64
LORA_A = 64
LORA_G = 128
A_PROJ = 3 * D_A + LORA_W + LORA_A + LORA_G
GN_EPS = 64e-5
D_B = D_MODEL // 2
DH_B = 64
H_B = D_B // DH_B
H_I = 4
D_IDX = 64
TOPK_MAX = 256
Q_BLOCK = 128
B_PROJ = 3 * D_B + H_I * D_IDX + D_IDX + H_I
REL_BUCKETS = 32
REL_MAX_DIST = 128
D_CONV = D_MODEL
CONV_W = 31
PEER_HEADS = 8
PEER_DK = 256
N_KEYS = 128
TOPK_HALF = 16
PEER_TOPK = 16
PEER_BLOCK = 512

LANES = 128
VMEM_LIMIT = 48 * 1024 * 1024
PEER_TABLE_VMEM_LIMIT = 56 * 1024 * 1024

F32 = jnp.float32
BF16 = jnp.bfloat16


def _round_up(n, m):
    return -(-n // m) * m


def _col_tile(m):
    for t in (1024, 896, 768, 640, 512, 384, 256, 128):
        if m % t == 0:
            return t
    raise ValueError(m)


def _norm_linear_kernel(x_ref, g_ref, w_ref, b_ref, o_ref, h_scr):
    @pl.when(pl.program_id(1) == 0)
    def _():
        x = x_ref[...]
        ms = jnp.mean(x * x, axis=-1, keepdims=True)
        h_scr[...] = (x * lax.rsqrt(ms + NORM_EPS) * g_ref[...]).astype(BF16)

    o_ref[...] = jnp.dot(h_scr[...], w_ref[...], preferred_element_type=F32) + b_ref[...]


def norm_linear(x, g, w, b=None, row_tile=512):
    n, d = x.shape
    m = w.shape[1]
    mp = _round_up(m, 2 * LANES)
    wb = jnp.pad(w.astype(BF16), ((0, 0), (0, mp - m)))
    bb = jnp.zeros((1, mp), F32) if b is None else jnp.pad(b.astype(F32), (0, mp - m))[None]
    tn = _col_tile(mp)
    tm = min(row_tile, n)
    assert n % tm == 0
    out = pl.pallas_call(
        _norm_linear_kernel,
        grid=(n // tm, mp // tn),
        in_specs=[
            pl.BlockSpec((tm, d), lambda i, j: (i, 0)),
            pl.BlockSpec((1, d), lambda i, j: (0, 0)),
            pl.BlockSpec((d, tn), lambda i, j: (0, j)),
            pl.BlockSpec((1, tn), lambda i, j: (0, j)),
        ],
        out_specs=pl.BlockSpec((tm, tn), lambda i, j: (i, j)),
        out_shape=jax.ShapeDtypeStruct((n, mp), F32),
        scratch_shapes=[pltpu.VMEM((tm, d), BF16)],
        compiler_params=pltpu.CompilerParams(
            dimension_semantics=("arbitrary", "arbitrary"), vmem_limit_bytes=VMEM_LIMIT),
    )(x, g[None].astype(F32), wb, bb)
    return out[:, :m] if mp != m else out


def _linear_residual_kernel(x_ref, w_ref, b_ref, r_ref, o_ref):
    o_ref[...] = (r_ref[...] + b_ref[...]
                  + jnp.dot(x_ref[...].astype(BF16), w_ref[...], preferred_element_type=F32))


def linear_residual(x, w, b, res, row_tile=512):
    n, k = x.shape
    m = w.shape[1]
    tm = min(row_tile, n)
    assert n % tm == 0 and m % LANES == 0
    bb = jnp.zeros((1, m), F32) if b is None else b.astype(F32)[None]
    return pl.pallas_call(
        _linear_residual_kernel,
        grid=(n // tm,),
        in_specs=[
            pl.BlockSpec((tm, k), lambda i: (i, 0)),
            pl.BlockSpec((k, m), lambda i: (0, 0)),
            pl.BlockSpec((1, m), lambda i: (0, 0)),
            pl.BlockSpec((tm, m), lambda i: (i, 0)),
        ],
        out_specs=pl.BlockSpec((tm, m), lambda i: (i, 0)),
        out_shape=jax.ShapeDtypeStruct((n, m), F32),
        compiler_params=pltpu.CompilerParams(
            dimension_semantics=("arbitrary",), vmem_limit_bytes=VMEM_LIMIT),
    )(x, w.astype(BF16), bb, res)


PEER_TB = 128
PEER_E = PEER_HEADS * PEER_TOPK
N_EXPERTS = N_KEYS * N_KEYS
ROW_WORDS = D_MODEL // 2
ROW_SUB = ROW_WORDS // LANES
PSTRIDE = PEER_E + 8
PEER_DOWN_CHUNK = 32
NEG_INF = float("-inf")


def _top_rows(s, k):
    n = s.shape[0]
    rows = lax.broadcasted_iota(jnp.int32, s.shape, 0).astype(F32)
    out_rows = lax.broadcasted_iota(jnp.int32, (k, s.shape[1]), 0)
    vals = jnp.zeros((k, s.shape[1]), F32)
    ids = jnp.zeros((k, s.shape[1]), F32)
    for it in range(k):
        m = jnp.max(s, axis=0, keepdims=True)
        first = jnp.min(jnp.where(s == m, rows, float(n)), axis=0, keepdims=True)
        vals = jnp.where(out_rows == it, m, vals)
        ids = jnp.where(out_rows == it, first, ids)
        s = jnp.where(rows == first, NEG_INF, s)
    return vals, ids


def _peer_select_kernel(x_ref, g_ref, wq_ref, sk_ref, h_ref, idx_ref, gate_ref):
    x = x_ref[...]
    ms = jnp.mean(x * x, axis=-1, keepdims=True)
    h = x * lax.rsqrt(ms + NORM_EPS) * g_ref[...]
    h_ref[...] = h
    q = jnp.dot(h.astype(BF16), wq_ref[...], preferred_element_type=F32).astype(BF16)
    half = PEER_DK // 2
    tb = x.shape[0]
    crow = lax.broadcasted_iota(jnp.int32, (TOPK_HALF * TOPK_HALF, tb), 0).astype(F32)
    orow = lax.broadcasted_iota(jnp.int32, (PEER_TOPK, tb), 0)
    ids = []
    for hd in range(PEER_HEADS):
        sv, si = [], []
        for p in range(2):
            c = (hd * 2 + p) * half
            s = lax.dot_general(sk_ref[hd * 2 + p], q[:, c:c + half],
                                (((1,), (1,)), ((), ())), preferred_element_type=F32)
            v, i = _top_rows(s, TOPK_HALF)
            sv.append(v)
            si.append(i)
        cand = jnp.concatenate([sv[0][i:i + 1] + sv[1] for i in range(TOPK_HALF)], axis=0)
        eid = jnp.concatenate([si[0][i:i + 1] * float(N_KEYS) + si[1] for i in range(TOPK_HALF)], axis=0)
        cs = jnp.zeros((PEER_TOPK, tb), F32)
        ce = jnp.zeros((PEER_TOPK, tb), F32)
        for it in range(PEER_TOPK):
            m = jnp.max(cand, axis=0, keepdims=True)
            first = jnp.min(jnp.where(cand == m, crow, float(crow.shape[0])), axis=0, keepdims=True)
            hit = crow == first
            e = jnp.max(jnp.where(hit, eid, -1.0), axis=0, keepdims=True)
            cs = jnp.where(orow == it, m, cs)
            ce = jnp.where(orow == it, e, ce)
            cand = jnp.where(hit, NEG_INF, cand)
        ex = jnp.exp(cs - cs[0:1])
        gates = ex / jnp.sum(ex, axis=0, keepdims=True)
        ids.append(ce * float(ROW_SUB))
        gate_ref[0, hd * PEER_TOPK:(hd + 1) * PEER_TOPK, :] = gates
    idx_ref[0] = jnp.concatenate(ids, axis=0).T.astype(jnp.int32)


def _table_row(tab, first_row):
    return tab[pl.ds(pl.multiple_of(first_row, ROW_SUB), ROW_SUB), :]


def _unpack_row(row):
    lo = pltpu.bitcast(lax.shift_left(row, 16), F32)
    hi = pltpu.bitcast(jnp.bitwise_and(row, jnp.int32(-65536)), F32)
    return lo, hi


def _load_block_scalars(src_hbm, dst_smem, sem):
    cp = pltpu.make_async_copy(src_hbm.at[pl.program_id(0)], dst_smem, sem)
    cp.start()
    cp.wait()


def _load_table_once(tab_hbm, tab_vmem, sem):
    @pl.when(pl.program_id(0) == 0)
    def _():
        cp = pltpu.make_async_copy(tab_hbm, tab_vmem, sem)
        cp.start()
        cp.wait()


def _peer_up_kernel(idx_hbm, tab_hbm, h_ref, gate_ref, w_ref, tab, idx_s, pbuf_a, pbuf_b, act_s, sems):
    _load_table_once(tab_hbm, tab, sems.at[0])
    _load_block_scalars(idx_hbm, idx_s, sems.at[1])
    lane = lax.broadcasted_iota(jnp.int32, (PEER_E, PEER_TB), 1)
    act_s[...] = jnp.zeros((PEER_E, PEER_TB), F32)
    pbuf_b[...] = jnp.zeros(pbuf_b.shape, F32)

    def gather(t, pbuf):
        ht = h_ref[t]
        h_lo, h_hi = ht[0:ROW_SUB], ht[ROW_SUB:2 * ROW_SUB]
        base = t * PEER_E
        for e in range(PEER_E):
            lo, hi = _unpack_row(_table_row(tab, idx_s[base + e]))
            pbuf[pl.ds(e, ROW_SUB, stride=PSTRIDE), :] = lo * h_lo + hi * h_hi

    def reduce(t, pbuf):
        acc = pbuf[0:PEER_E, :]
        for c in range(1, ROW_SUB):
            acc = acc + pbuf[c * PSTRIDE:c * PSTRIDE + PEER_E, :]
        col = jnp.sum(acc, axis=-1, keepdims=True)
        act_s[...] = jnp.where(lane == t, col, act_s[...])

    def token_pair(j, carry):
        reduce(2 * j - 1, pbuf_b)
        gather(2 * j, pbuf_a)
        reduce(2 * j, pbuf_a)
        gather(2 * j + 1, pbuf_b)
        return carry

    lax.fori_loop(0, PEER_TB // 2, token_pair, 0)
    reduce(PEER_TB - 1, pbuf_b)
    a = act_s[...]
    w_ref[0] = (gate_ref[0] * (0.5 * a * (1.0 + lax.erf(a * (2.0 ** -0.5))))).T


def _peer_down_kernel(idx_hbm, w_hbm, tab_hbm, x_ref, o_ref, tab, idx_s, w_s, sems):
    _load_table_once(tab_hbm, tab, sems.at[0])
    _load_block_scalars(idx_hbm, idx_s, sems.at[1])
    _load_block_scalars(w_hbm, w_s, sems.at[2])
    n_acc = 4
    zero = jnp.zeros((ROW_SUB, LANES), F32)

    def token(t, carry):
        def chunk(c, accs):
            acc_lo, acc_hi = list(accs[:n_acc]), list(accs[n_acc:])
            base = t * PEER_E + c * PEER_DOWN_CHUNK
            for j in range(PEER_DOWN_CHUNK):
                lo, hi = _unpack_row(_table_row(tab, idx_s[base + j]))
                w = w_s[base + j]
                acc_lo[j % n_acc] = acc_lo[j % n_acc] + w * lo
                acc_hi[j % n_acc] = acc_hi[j % n_acc] + w * hi
            return tuple(acc_lo) + tuple(acc_hi)

        accs = lax.fori_loop(0, PEER_E // PEER_DOWN_CHUNK, chunk, (zero,) * (2 * n_acc))
        lo = (accs[0] + accs[1]) + (accs[2] + accs[3])
        hi = (accs[4] + accs[5]) + (accs[6] + accs[7])
        o_ref[t] = x_ref[t] + jnp.concatenate([lo, hi], axis=0)
        return carry

    lax.fori_loop(0, PEER_TB, token, 0)


def _pack_table(tab):
    n = tab.shape[0]
    bits = lax.bitcast_convert_type(tab.astype(jnp.bfloat16), jnp.uint16).astype(jnp.uint32)
    words = bits[:, :ROW_WORDS] | (bits[:, ROW_WORDS:] << 16)
    return lax.bitcast_convert_type(words, jnp.int32).reshape(n * ROW_SUB, LANES)


def peer(x, g, wq, subkeys, u_tab, v_tab):
    n, d = x.shape
    assert n % PEER_TB == 0 and d == D_MODEL
    nb = n // PEER_TB
    sk = subkeys.reshape(PEER_HEADS * 2, N_KEYS, PEER_DK // 2).astype(BF16)
    params = pltpu.CompilerParams(dimension_semantics=("arbitrary",), vmem_limit_bytes=VMEM_LIMIT)
    h, idx, gates = pl.pallas_call(
        _peer_select_kernel,
        grid=(nb,),
        in_specs=[
            pl.BlockSpec((PEER_TB, d), lambda i: (i, 0)),
            pl.BlockSpec((1, d), lambda i: (0, 0)),
            pl.BlockSpec((d, PEER_HEADS * PEER_DK), lambda i: (0, 0)),
            pl.BlockSpec((PEER_HEADS * 2, N_KEYS, PEER_DK // 2), lambda i: (0, 0, 0)),
        ],
        out_specs=[
            pl.BlockSpec((PEER_TB, d), lambda i: (i, 0)),
            pl.BlockSpec((1, PEER_TB, PEER_E), lambda i: (i, 0, 0)),
            pl.BlockSpec((1, PEER_E, PEER_TB), lambda i: (i, 0, 0)),
        ],
        out_shape=[
            jax.ShapeDtypeStruct((n, d), F32),
            jax.ShapeDtypeStruct((nb, PEER_TB, PEER_E), jnp.int32),
            jax.ShapeDtypeStruct((nb, PEER_E, PEER_TB), F32),
        ],
        compiler_params=params,
        name="peer_select",
    )(x, g[None].astype(F32), wq.astype(BF16), sk)

    table_params = pltpu.CompilerParams(dimension_semantics=("arbitrary",),
                                        vmem_limit_bytes=PEER_TABLE_VMEM_LIMIT)
    tok_tiles = (PEER_TB, d // LANES, LANES)
    w = pl.pallas_call(
        _peer_up_kernel,
        grid=(nb,),
        in_specs=[
            pl.BlockSpec(memory_space=pl.ANY),
            pl.BlockSpec(memory_space=pl.ANY),
            pl.BlockSpec(tok_tiles, lambda i: (i, 0, 0)),
            pl.BlockSpec((1, PEER_E, PEER_TB), lambda i: (i, 0, 0)),
        ],
        out_specs=pl.BlockSpec((1, PEER_TB, PEER_E), lambda i: (i, 0, 0)),
        out_shape=jax.ShapeDtypeStruct((nb, PEER_TB, PEER_E), F32),
        scratch_shapes=[
            pltpu.VMEM((N_EXPERTS * ROW_SUB, LANES), jnp.int32),
            pltpu.SMEM((PEER_TB * PEER_E,), jnp.int32),
            pltpu.VMEM((ROW_SUB * PSTRIDE, LANES), F32),
            pltpu.VMEM((ROW_SUB * PSTRIDE, LANES), F32),
            pltpu.VMEM((PEER_E, PEER_TB), F32),
            pltpu.SemaphoreType.DMA((2,)),
        ],
        compiler_params=table_params,
        name="peer_up",
    )(idx.reshape(nb, PEER_TB * PEER_E), _pack_table(u_tab), h.reshape(n, d // LANES, LANES), gates)

    out = pl.pallas_call(
        _peer_down_kernel,
        grid=(nb,),
        in_specs=[
            pl.BlockSpec(memory_space=pl.ANY),
            pl.BlockSpec(memory_space=pl.ANY),
            pl.BlockSpec(memory_space=pl.ANY),
            pl.BlockSpec(tok_tiles, lambda i: (i, 0, 0)),
        ],
        out_specs=pl.BlockSpec(tok_tiles, lambda i: (i, 0, 0)),
        out_shape=jax.ShapeDtypeStruct((n, d // LANES, LANES), F32),
        scratch_shapes=[
            pltpu.VMEM((N_EXPERTS * ROW_SUB, LANES), jnp.int32),
            pltpu.SMEM((PEER_TB * PEER_E,), jnp.int32),
            pltpu.SMEM((PEER_TB * PEER_E,), F32),
            pltpu.SemaphoreType.DMA((3,)),
        ],
        compiler_params=table_params,
        name="peer_down",
    )(idx.reshape(nb, PEER_TB * PEER_E), w.reshape(nb, PEER_TB * PEER_E), _pack_table(v_tab),
      x.reshape(n, d // LANES, LANES))
    return out.reshape(n, d)


DSA_KT = 256
DSA_NEAR = 3
DSA_POS_BITS = 14
DSA_COUNT_TILES = 4
CHUNK_SHIFT = CHUNK.bit_length() - 1
INT_MIN = -2 ** 31
KEY_NEG_INF = 0x807FFFFF - 2 ** 32
_NT = (((1,), (1,)), ((), ()))


def _sort_key(x):
    b = pltpu.bitcast(x, jnp.int32)
    return b ^ ((b >> 31) & 0x7FFFFFFF)


def _dsa_kernel(qpad_ref, qi_ref, wit_ref, k_ref, vt_ref, ki_ref, nb_ref, o_ref,
                keys_scr, s_scr, p_scr, *acc_refs, offset, n_keys, topk, qb):
    kt_ = DSA_KT
    q0 = offset + pl.program_id(1) * qb
    qpos = q0 + lax.broadcasted_iota(jnp.int32, (1, qb), 1)
    vis_end = jnp.minimum((lax.shift_right_logical(qpos, CHUNK_SHIFT) + 1) * CHUNK, n_keys)
    blk_end = jnp.minimum(((q0 + qb - 1) // CHUNK + 1) * CHUNK, n_keys)
    n_tiles = (blk_end + kt_ - 1) // kt_
    row = lax.broadcasted_iota(jnp.int32, (kt_, qb), 0)

    qi = qi_ref[0]
    qis = [qi[:, h * D_IDX:(h + 1) * D_IDX] for h in range(H_I)]
    wit = wit_ref[0] * (H_I ** -0.5)

    def score_tile(t, c):
        k0 = pl.multiple_of(t * kt_, kt_)
        kit = ki_ref[0, pl.ds(k0, kt_), :]
        idx = jnp.zeros((kt_, qb), F32)
        for h in range(H_I):
            s = lax.dot_general(kit, qis[h], _NT, preferred_element_type=F32)
            idx = idx + jnp.maximum(s * (D_IDX ** -0.5), 0.0) * wit[h:h + 1]
        idx = jnp.where(idx == 0.0, 0.0, idx)
        keys_scr[pl.ds(k0, kt_), :] = jnp.where(row + k0 < vis_end, _sort_key(idx), KEY_NEG_INF)
        return c

    lax.fori_loop(0, n_tiles, score_tile, 0)

    n_steps = (n_tiles + DSA_COUNT_TILES - 1) // DSA_COUNT_TILES

    def pad_tile(t, c):
        keys_scr[pl.ds(pl.multiple_of(t * kt_, kt_), kt_), :] = jnp.full((kt_, qb), KEY_NEG_INF, jnp.int32)
        return c

    lax.fori_loop(n_tiles, n_steps * DSA_COUNT_TILES, pad_tile, 0)

    def count(preds, n_out):
        def body(s, accs):
            accs = list(accs)
            for u in range(DSA_COUNT_TILES):
                k0 = pl.multiple_of((s * DSA_COUNT_TILES + u) * kt_, kt_)
                hits = preds(keys_scr[pl.ds(k0, kt_), :], row + k0)
                for i in range(n_out):
                    accs[i] = accs[i] + jnp.sum(hits[i].reshape(kt_ // 8, 8, qb), axis=0)
            return tuple(accs)
        accs = lax.fori_loop(0, n_steps, body, (jnp.zeros((8, qb), F32),) * n_out)
        return [jnp.sum(acc, axis=0, keepdims=True) for acc in accs]

    one = lambda hit: jnp.where(hit, 1.0, 0.0)
    bit = lambda n: lax.shift_left(jnp.int32(1), n)
    kf = float(topk)
    c0, = count(lambda kt, kp: (one(kt >= 0),), 1)
    thr0 = jnp.where(c0 >= kf, 0, INT_MIN).astype(jnp.int32)

    def thr_bits(j, thr):
        hi, lo = bit(30 - 2 * j), bit(29 - 2 * j)
        c_hl, c_h, c_l = count(lambda kt, kp: (one(kt >= (thr | hi | lo)), one(kt >= (thr | hi)),
                                                one(kt >= (thr | lo))), 3)
        take_hi = c_h >= kf
        take_lo = jnp.where(take_hi, c_hl, c_l) >= kf
        return thr | jnp.where(take_hi, hi, 0) | jnp.where(take_lo, lo, 0)

    thr = lax.fori_loop(0, 15, thr_bits, thr0)
    c_last, = count(lambda kt, kp: (one(kt >= (thr | 1)),), 1)
    thr = jnp.where(c_last >= kf, thr | 1, thr)
    c_gt, c_eq = count(lambda kt, kp: (one(kt > thr), one(kt == thr)), 2)
    need = kf - c_gt
    select_all = vis_end <= topk

    def tie_cut():
        def pos_bits(j, lo_pos):
            hi, lo = bit(DSA_POS_BITS - 1 - 2 * j), bit(DSA_POS_BITS - 2 - 2 * j)
            tied = lambda kt, kp, bound: jnp.where(kt == thr, one(kp < bound), 0.0)
            f_hl, f_h, f_l = count(lambda kt, kp: (tied(kt, kp, lo_pos + hi + lo), tied(kt, kp, lo_pos + hi),
                                                    tied(kt, kp, lo_pos + lo)), 3)
            take_hi = f_h < need
            take_lo = jnp.where(take_hi, f_hl, f_l) < need
            return lo_pos + jnp.where(take_hi, hi, 0) + jnp.where(take_lo, lo, 0)
        return lax.fori_loop(0, DSA_POS_BITS // 2, pos_bits, jnp.zeros((1, qb), jnp.int32)) + 1

    surplus = jnp.max(jnp.where((c_eq > need) & jnp.logical_not(select_all), 1.0, 0.0)) > 0.0
    cut = lax.cond(surplus, tie_cut, lambda: jnp.full((1, qb), 2 ** DSA_POS_BITS, jnp.int32))
    thr = jnp.where(select_all, KEY_NEG_INF, thr)
    cut = jnp.where(select_all, 0, cut)

    for acc in acc_refs:
        acc[...] = jnp.zeros(acc.shape, F32)

    def attend_tile(t, carry):
        m_all, l_all = carry
        k0 = pl.multiple_of(t * kt_, kt_)
        keys = keys_scr[pl.ds(k0, kt_), :]
        sel = (keys - jnp.where(row + k0 < cut, 0, 1)) >= thr
        step = (k0 - q0 + (DSA_NEAR - 1) * LANES) // LANES
        nidx = jnp.where(step < 0, DSA_NEAR, step)
        for h in range(H_B):
            kh = k_ref[0, pl.ds(k0, kt_), (h // 2) * LANES:(h // 2 + 1) * LANES]
            qh = qpad_ref[0, :, h * LANES:(h + 1) * LANES]
            s_scr[h] = lax.dot_general(kh, qh, _NT, preferred_element_type=F32)
        m_rows, l_rows, alphas = [], [], []
        for h in range(H_B):
            s = jnp.where(sel, s_scr[h] * (DH_B ** -0.5) + nb_ref[nidx, h], NEG_INF)
            m_old = m_all[h:h + 1]
            m_new = jnp.maximum(m_old, jnp.max(s, axis=0, keepdims=True))
            m_safe = jnp.where(m_new == NEG_INF, 0.0, m_new)
            p = jnp.exp(s - m_safe)
            alpha = jnp.exp(m_old - m_safe)
            p_scr[h] = p.astype(BF16)
            m_rows.append(m_new)
            l_rows.append(alpha * l_all[h:h + 1] + jnp.sum(p, axis=0, keepdims=True))
            alphas.append(alpha)
        for h in range(H_B):
            vth = vt_ref[0, h * DH_B:(h + 1) * DH_B, pl.ds(k0, kt_)]
            acc_refs[h][...] = (alphas[h] * acc_refs[h][...]
                                + jnp.dot(vth, p_scr[h], preferred_element_type=F32))
        return jnp.concatenate(m_rows, axis=0), jnp.concatenate(l_rows, axis=0)

    _, l_all = lax.fori_loop(0, n_tiles, attend_tile,
                             (jnp.full((H_B, qb), NEG_INF, F32), jnp.zeros((H_B, qb), F32)))
    for h in range(H_B):
        o_ref[0, h * DH_B:(h + 1) * DH_B] = acc_refs[h][...] / l_all[h:h + 1]


def _near_bias_tiles(rel_bias, qb):
    k = jnp.arange(DSA_KT, dtype=jnp.int32)[:, None]
    q = jnp.arange(qb, dtype=jnp.int32)[None, :]
    rels = [(j - (DSA_NEAR - 1)) * LANES + k - q for j in range(DSA_NEAR)]
    rels.append(jnp.full((DSA_KT, qb), -(DSA_NEAR * LANES + DSA_KT), jnp.int32))
    tiles = rel_bias[_t5_bucket(jnp.stack(rels))]
    return jnp.moveaxis(tiles, -1, 1).astype(F32)


def dsa_attention(q, k_all, v_all, qi, wi, ki_all, rel_bias, offset):
    bx, t, _, _ = q.shape
    n_keys = k_all.shape[1]
    topk = min(TOPK_MAX, n_keys // 4)
    qb = Q_BLOCK if t % Q_BLOCK == 0 else t
    lp = _round_up(n_keys, DSA_KT)
    assert lp <= 2 ** DSA_POS_BITS and offset % DSA_KT == 0 and (qb == Q_BLOCK or t == qb)
    pad = ((0, 0), (0, lp - n_keys), (0, 0))
    half = jax.nn.one_hot(jnp.arange(H_B) % 2, 2, dtype=q.dtype)
    qpad = (q[:, :, :, None, :] * half[None, None, :, :, None]).reshape(bx, t, H_B * LANES).astype(BF16)
    kk = jnp.pad(k_all.reshape(bx, n_keys, D_B), pad).astype(BF16)
    vt = jnp.pad(v_all.reshape(bx, n_keys, D_B), pad).astype(BF16).transpose(0, 2, 1)
    ki = jnp.pad(ki_all, pad).astype(BF16)
    wit = jnp.pad(wi.astype(F32).transpose(0, 2, 1), ((0, 0), (0, 8 - H_I), (0, 0)))
    nb = _near_bias_tiles(rel_bias, qb)
    kern = functools.partial(_dsa_kernel, offset=offset, n_keys=n_keys, topk=topk, qb=qb)
    ot = pl.pallas_call(
        kern,
        grid=(bx, t // qb),
        in_specs=[
            pl.BlockSpec((1, qb, H_B * LANES), lambda b, i: (b, i, 0)),
            pl.BlockSpec((1, qb, H_I * D_IDX), lambda b, i: (b, i, 0)),
            pl.BlockSpec((1, 8, qb), lambda b, i: (b, 0, i)),
            pl.BlockSpec((1, lp, D_B), lambda b, i: (b, 0, 0)),
            pl.BlockSpec((1, D_B, lp), lambda b, i: (b, 0, 0)),
            pl.BlockSpec((1, lp, D_IDX), lambda b, i: (b, 0, 0)),
            pl.BlockSpec((DSA_NEAR + 1, H_B, DSA_KT, qb), lambda b, i: (0, 0, 0, 0)),
        ],
        out_specs=pl.BlockSpec((1, D_B, qb), lambda b, i: (b, 0, i)),
        out_shape=jax.ShapeDtypeStruct((bx, D_B, t), F32),
        scratch_shapes=[pltpu.VMEM((_round_up(n_keys, DSA_KT * DSA_COUNT_TILES), qb), jnp.int32),
                        pltpu.VMEM((H_B, DSA_KT, qb), F32),
                        pltpu.VMEM((H_B, DSA_KT, qb), BF16)]
        + [pltpu.VMEM((DH_B, qb), F32)] * H_B,
        compiler_params=pltpu.CompilerParams(
            dimension_semantics=("arbitrary", "arbitrary"), vmem_limit_bytes=PEER_TABLE_VMEM_LIMIT),
        name="dsa_attention",
    )(qpad, qi.reshape(bx, t, H_I * D_IDX).astype(BF16), wit, kk, vt, ki, nb)
    return ot.transpose(0, 2, 1)


WKV_CHUNK = 64
WKV_BLOCK = 256
_TN = (((0,), (0,)), ((), ()))


def _wkv_kernel(r_ref, lw_ref, k_ref, v_ref, kk_ref, a_ref, s0_ref, o_ref, st_ref, s_scr, *, chunk, n_chunks):
    c = chunk

    @pl.when(pl.program_id(1) == 0)
    def _():
        s_scr[...] = s0_ref[0]

    row = lax.broadcasted_iota(jnp.int32, (c, c), 0)
    col = lax.broadcasted_iota(jnp.int32, (c, c), 1)
    strict, incl = row > col, row >= col
    tri = jnp.where(incl, 1.0, 0.0)
    eye = jnp.where(row == col, 1.0, 0.0)
    bdot = lambda x, y: jnp.dot(x.astype(BF16), y.astype(BF16), preferred_element_type=F32)
    bdot_nt = lambda x, y: lax.dot_general(x.astype(BF16), y.astype(BF16), _NT, preferred_element_type=F32)
    bdot_tn = lambda x, y: lax.dot_general(x.astype(BF16), y.astype(BF16), _TN, preferred_element_type=F32)

    def solve_chunk(ci, carry):
        c0 = pl.multiple_of(ci * c, c)
        rows = pl.ds(c0, c)
        outs = []
        for h in range(H_A):
            hs = slice(h * DH_A, (h + 1) * DH_A)
            r, lw, k, v = r_ref[0, rows, hs], lw_ref[0, rows, hs], k_ref[0, rows, hs], v_ref[0, rows, hs]
            kap, a = kk_ref[0, rows, hs], a_ref[0, rows, hs]
            cum = jnp.dot(tri, lw, preferred_element_type=F32, precision=lax.Precision.HIGHEST)
            p, p_inv, p_prev = jnp.exp(cum), jnp.exp(-cum), jnp.exp(cum - lw)
            kap_t, r_t, k_b, b_b = kap * p_prev, r * p, k * p_inv, kap * a * p_inv
            g = bdot_nt(jnp.concatenate([kap_t, r_t], axis=0), jnp.concatenate([k_b, b_b], axis=0))
            a_kk, a_kb, a_rk, a_rb = g[:c, :c], g[:c, c:], g[c:, :c], g[c:, c:]
            nil = jnp.where(strict, -a_kb, 0.0)
            inv, power = eye + nil, nil
            for _ in range(c.bit_length() - 2):
                power = bdot(power, power)
                inv = inv + bdot(inv, power)
            s_prev = s_scr[h]
            u = bdot(inv, bdot_nt(kap_t, s_prev) + bdot(jnp.where(strict, a_kk, 0.0), v))
            outs.append(bdot_nt(r_t, s_prev) + bdot(jnp.where(incl, a_rk, 0.0), v)
                        - bdot(jnp.where(incl, a_rb, 0.0), u))
            s_scr[h] = (s_prev + bdot_tn(v, k_b) - bdot_tn(u, b_b)) * p[c - 1:c, :]
        o_ref[0, rows, :] = jnp.concatenate(outs, axis=1)
        return carry

    lax.fori_loop(0, n_chunks, solve_chunk, 0)

    @pl.when(pl.program_id(1) == pl.num_programs(1) - 1)
    def _():
        st_ref[0] = s_scr[...]


def wkv_chunked(r, lw, k, v, kk, a, s0):
    bx, t, _ = r.shape
    chunk = min(WKV_CHUNK, t)
    blk = min(WKV_BLOCK, t)
    assert t % blk == 0 and blk % chunk == 0 and chunk & (chunk - 1) == 0
    tok = pl.BlockSpec((1, blk, D_A), lambda b, i: (b, i, 0))
    state = pl.BlockSpec((1, H_A, DH_A, DH_A), lambda b, i: (b, 0, 0, 0))
    return pl.pallas_call(
        functools.partial(_wkv_kernel, chunk=chunk, n_chunks=blk // chunk),
        grid=(bx, t // blk),
        in_specs=[tok] * 6 + [state],
        out_specs=[tok, state],
        out_shape=[jax.ShapeDtypeStruct((bx, t, D_A), F32),
                   jax.ShapeDtypeStruct((bx, H_A, DH_A, DH_A), F32)],
        scratch_shapes=[pltpu.VMEM((H_A, DH_A, DH_A), F32)],
        compiler_params=pltpu.CompilerParams(
            dimension_semantics=("arbitrary", "arbitrary"), vmem_limit_bytes=VMEM_LIMIT),
        name="wkv_chunked",
    )(r, lw, k, v, kk, a, s0.astype(F32))


def _rmsnorm(x, g):
    xf = x.astype(F32)
    y = xf * lax.rsqrt(jnp.mean(xf * xf, axis=-1, keepdims=True) + NORM_EPS)
    return (y * g).astype(x.dtype)


def _t5_bucket(rel):
    nb = REL_BUCKETS // 2
    ret = jnp.where(rel > 0, nb, 0)
    n = jnp.abs(rel)
    max_exact = nb // 2
    nf = jnp.maximum(n, 1).astype(F32)
    large = max_exact + (jnp.log(nf / max_exact) / math.log(REL_MAX_DIST / max_exact)
                         * (nb - max_exact)).astype(jnp.int32)
    large = jnp.minimum(large, nb - 1)
    return ret + jnp.where(n < max_exact, n, large)


def _wkv_scan(r, w, k, v, kk, a, s0):
    xs = tuple(jnp.moveaxis(t.astype(F32), 1, 0) for t in (r, w, k, v, kk, a))

    def step(S, inp):
        r_t, w_t, k_t, v_t, kk_t, a_t = inp
        sa = jnp.einsum('bhvk,bhk->bhv', S, -kk_t)
        S = (S * w_t[:, :, None, :] + sa[..., None] * (kk_t * a_t)[:, :, None, :]
             + v_t[..., None] * k_t[:, :, None, :])
        return S, jnp.einsum('bhvk,bhk->bhv', S, r_t)

    S, o = lax.scan(step, s0.astype(F32), xs)
    return jnp.moveaxis(o, 0, 1), S.astype(s0.dtype)


def _dsa_attend(q, k_all, v_all, qi, wi, ki_all, q_pos, k_pos, rel_bias, topk):
    Bx, T, H, Dh = q.shape
    qb = Q_BLOCK if T % Q_BLOCK == 0 else T
    nblk = T // qb

    def to_blocks(t):
        return jnp.moveaxis(t.reshape(Bx, nblk, qb, *t.shape[2:]), 1, 0)

    k_chunk = k_pos // CHUNK

    def block(args):
        qx, qix, wix, pos = args
        sc = jax.nn.relu(jnp.einsum('bqhd,bsd->bqhs', qix, ki_all).astype(F32) * D_IDX ** -0.5)
        idx = jnp.einsum('bqhs,bqh->bqs', sc, wix.astype(F32) * H_I ** -0.5)
        vis = k_chunk[None, :] <= (pos // CHUNK)[:, None]
        idx = jnp.where(vis[None], idx, -jnp.inf)
        top_s, top_i = lax.top_k(idx, topk)
        k_sel = jax.vmap(lambda kb, ib: kb[ib])(k_all, top_i)
        v_sel = jax.vmap(lambda vb, ib: vb[ib])(v_all, top_i)
        logit = jnp.einsum('bqhd,bqkhd->bqhk', qx, k_sel).astype(F32) * Dh ** -0.5
        rel = k_pos[top_i] - pos[None, :, None]
        bias = rel_bias[_t5_bucket(rel)]
        logit = logit + jnp.moveaxis(bias, -1, 2).astype(F32)
        logit = jnp.where(jnp.isfinite(top_s)[:, :, None, :], logit, -jnp.inf)
        p = jax.nn.softmax(logit, axis=-1)
        return jnp.einsum('bqhk,bqkhd->bqhd', p.astype(v_sel.dtype), v_sel)

    out = lax.map(block, (to_blocks(q), to_blocks(qi), to_blocks(wi), q_pos.reshape(nblk, qb)))
    return jnp.moveaxis(out, 0, 1).reshape(Bx, T, H, Dh)


def _even_mixer(x, shift_prev, wkv_prev, k_past, v_past, kidx_past,
                g_mix, w_in, mu, w0, w_du, a0, w_au, w_gu, k_k, k_a, r_k, gn_w, gn_b,
                qn_g, kn_g, rel_bias, w_out):
    Bx, T, _ = x.shape
    p = norm_linear(x.reshape(Bx * T, D_MODEL), g_mix, w_in).reshape(Bx, T, -1)
    pa, pb = p[..., :A_PROJ], p[..., A_PROJ:]
    pa_prev = jnp.concatenate([shift_prev[:, None, :].astype(pa.dtype), pa[:, :-1]], axis=1)
    xm = (pa + (pa_prev - pa) * mu).astype(F32)
    r, k, v, dw, da, dg = jnp.split(
        xm, [D_A, 2 * D_A, 3 * D_A, 3 * D_A + LORA_W, 3 * D_A + LORA_W + LORA_A], axis=-1)
    w_log = -jax.nn.softplus(-(w0 + jnp.tanh(dw) @ w_du)) - 0.5
    log_decay = -jnp.exp(w_log)
    a = jax.nn.sigmoid(a0 + da @ w_au)
    gate = jax.nn.sigmoid(dg) @ w_gu
    kk = (k * k_k).reshape(Bx, T, H_A, DH_A)
    kk = kk / jnp.maximum(jnp.sqrt(jnp.sum(kk * kk, axis=-1, keepdims=True)), 1e-12)
    k = k * (1.0 + (a - 1.0) * k_a)
    o, wkv_new = wkv_chunked(r, log_decay, k, v, kk.reshape(Bx, T, D_A), a, wkv_prev)
    r, k, v, o = (t.reshape(Bx, T, H_A, DH_A) for t in (r, k, v, o))
    mo = jnp.mean(o, axis=-1, keepdims=True)
    vo = jnp.mean(jnp.square(o - mo), axis=-1, keepdims=True)
    o = ((o - mo) * lax.rsqrt(vo + GN_EPS)).reshape(Bx, T, D_A) * gn_w + gn_b
    o = o + (jnp.sum(r * k * r_k, axis=-1, keepdims=True) * v).reshape(Bx, T, D_A)
    o_a = (o * gate).astype(x.dtype)
    q, kb, vb, qi, ki, wi = jnp.split(
        pb, [D_B, 2 * D_B, 3 * D_B, 3 * D_B + H_I * D_IDX, 3 * D_B + H_I * D_IDX + D_IDX], axis=-1)
    q = _rmsnorm(q.reshape(Bx, T, H_B, DH_B), qn_g)
    kb = _rmsnorm(kb.reshape(Bx, T, H_B, DH_B), kn_g)
    vb = vb.reshape(Bx, T, H_B, DH_B)
    qi = qi.reshape(Bx, T, H_I, D_IDX)
    offset = k_past.shape[1]
    L = offset + T
    topk = min(TOPK_MAX, L // 4)
    k_all = jnp.concatenate([k_past.astype(kb.dtype), kb], axis=1)
    v_all = jnp.concatenate([v_past.astype(vb.dtype), vb], axis=1)
    ki_all = jnp.concatenate([kidx_past.astype(ki.dtype), ki], axis=1)
    q_pos = offset + jnp.arange(T, dtype=jnp.int32)
    k_pos = jnp.arange(L, dtype=jnp.int32)
    o_b = dsa_attention(q, k_all, v_all, qi, wi, ki_all, rel_bias, offset)
    cat = jnp.concatenate([o_a, o_b.astype(x.dtype)], axis=-1)
    x_new = linear_residual(cat.reshape(Bx * T, D_MODEL), w_out, None,
                            x.reshape(Bx * T, D_MODEL)).reshape(Bx, T, D_MODEL)
    return x_new, pa[:, -1], wkv_new, kb, vb, ki


def _conv_mixer(x, conv_prev, g_mix, w1, b1, w_dw, b_dw, ln_g, ln_b, w2, b2):
    Bx, T, _ = x.shape
    u = norm_linear(x.reshape(Bx * T, D_MODEL), g_mix, w1, b1).reshape(Bx, T, -1)
    glu = u[..., :D_CONV] * jax.nn.sigmoid(u[..., D_CONV:])
    padded = jnp.concatenate([conv_prev.astype(glu.dtype), glu], axis=1)
    y = lax.conv_general_dilated(padded, w_dw[:, None, :].astype(glu.dtype), (1,), 'VALID',
                                 dimension_numbers=('NWC', 'WIO', 'NWC'),
                                 feature_group_count=D_CONV) + b_dw
    yf = y.astype(F32)
    m = jnp.mean(yf, axis=-1, keepdims=True)
    var = jnp.mean(jnp.square(yf - m), axis=-1, keepdims=True)
    yn = jax.nn.silu((yf - m) * lax.rsqrt(var + 1e-5) * ln_g + ln_b)
    x_new = linear_residual(yn.reshape(Bx * T, D_CONV), w2, b2,
                            x.reshape(Bx * T, D_MODEL)).reshape(Bx, T, D_MODEL)
    return x_new, padded[:, -(CONV_W - 1):]


def _peer_both(xp, xs, g, wq, subkeys, u_tab, v_tab):
    np_ = xp.shape[0] * xp.shape[1]
    rows = jnp.concatenate([xp.reshape(np_, D_MODEL), xs.reshape(-1, D_MODEL)], axis=0)
    out = peer(rows, g, wq, subkeys, u_tab, v_tab)
    return out[:np_].reshape(xp.shape), out[np_:].reshape(xs.shape)


def kernel(x_prompt, x_sample, state_wkv, state_shift, cache_k, cache_v, cache_kidx, state_conv,
           norm_mix, norm_ffn, w_in, mu_shift, w0, w_decay_up, a0, w_iclr_up, w_gate_up,
           k_k, k_a, r_k, gn_w, gn_b, qn_g, kn_g, rel_bias, w_out,
           conv_w1, conv_b1, conv_dw, conv_bdw, conv_ln_g, conv_ln_b, conv_w2, conv_b2,
           peer_wq, peer_subkeys, peer_u, peer_v):
    xp, xs = x_prompt, x_sample
    Bp = xp.shape[0]
    dt = xp.dtype
    wkv_p, shift_p, k_p, v_p, kidx_p, conv_p = [], [], [], [], [], []
    wkv_s, shift_s, k_s, v_s, kidx_s, conv_s = [], [], [], [], [], []
    for li in range(DEPTH):
        if li % 2 == 0:
            e = li // 2
            prm = (norm_mix[li], w_in[e], mu_shift[e], w0[e], w_decay_up[e], a0[e], w_iclr_up[e],
                   w_gate_up[e], k_k[e], k_a[e], r_k[e], gn_w[e], gn_b[e], qn_g[e], kn_g[e],
                   rel_bias, w_out[e])
            xp, sh, wk, kb, vb, ki = _even_mixer(
                xp, jnp.zeros((Bp, A_PROJ), dt), jnp.zeros((Bp, H_A, DH_A, DH_A), dt),
                jnp.zeros((Bp, 0, H_B, DH_B), dt), jnp.zeros((Bp, 0, H_B, DH_B), dt),
                jnp.zeros((Bp, 0, D_IDX), dt), *prm)
            wkv_p.append(wk); shift_p.append(sh); k_p.append(kb); v_p.append(vb); kidx_p.append(ki)
            xs, sh, wk, kb, vb, ki = _even_mixer(
                xs, state_shift[e], state_wkv[e], cache_k[e], cache_v[e], cache_kidx[e], *prm)
            wkv_s.append(wk); shift_s.append(sh); k_s.append(kb); v_s.append(vb); kidx_s.append(ki)
        else:
            o = li // 2
            prm = (norm_mix[li], conv_w1[o], conv_b1[o], conv_dw[o], conv_bdw[o],
                   conv_ln_g[o], conv_ln_b[o], conv_w2[o], conv_b2[o])
            xp, cp = _conv_mixer(xp, jnp.zeros((Bp, CONV_W - 1, D_CONV), dt), *prm)
            xs, cs = _conv_mixer(xs, state_conv[o], *prm)
            conv_p.append(cp); conv_s.append(cs)
        pprm = (norm_ffn[li], peer_wq[li], peer_subkeys[li], peer_u[li], peer_v[li])
        xp, xs = _peer_both(xp, xs, *pprm)
    return (xp, xs,
            jnp.stack(wkv_p), jnp.stack(shift_p), jnp.stack(k_p), jnp.stack(v_p), jnp.stack(kidx_p), jnp.stack(conv_p),
            jnp.stack(wkv_s), jnp.stack(shift_s), jnp.stack(k_s), jnp.stack(v_s), jnp.stack(kidx_s), jnp.stack(conv_s))
```

```python
import functools
import math

import jax
import jax.numpy as jnp
from jax import lax
from jax.experimental import pallas as pl
from jax.experimental.pallas import tpu as pltpu

D_MODEL = 1024
DEPTH = 4
CHUNK = 64
NORM_EPS = 1e-6
D_A = D_MODEL // 2
DH_A = 64
H_A = D_A // DH_A
LORA_W = 64
LORA_A = 64
LORA_G = 128
A_PROJ = 3 * D_A + LORA_W + LORA_A + LORA_G
GN_EPS = 64e-5
D_B = D_MODEL // 2
DH_B = 64
H_B = D_B // DH_B
H_I = 4
D_IDX = 64
TOPK_MAX = 256
Q_BLOCK = 128
B_PROJ = 3 * D_B + H_I * D_IDX + D_IDX + H_I
REL_BUCKETS = 32
REL_MAX_DIST = 128
D_CONV = D_MODEL
CONV_W = 31
PEER_HEADS = 8
PEER_DK = 256
N_KEYS = 128
TOPK_HALF = 16
PEER_TOPK = 16
PEER_BLOCK = 512

LANES = 128
VMEM_LIMIT = 48 * 1024 * 1024
PEER_TABLE_VMEM_LIMIT = 56 * 1024 * 1024

F32 = jnp.float32
BF16 = jnp.bfloat16


def _round_up(n, m):
    return -(-n // m) * m


def _col_tile(m):
    for t in (1024, 896, 768, 640, 512, 384, 256, 128):
        if m % t == 0:
            return t
    raise ValueError(m)


def _norm_linear_kernel(x_ref, g_ref, w_ref, b_ref, o_ref, h_scr):
    @pl.when(pl.program_id(1) == 0)
    def _():
        x = x_ref[...]
        ms = jnp.mean(x * x, axis=-1, keepdims=True)
        h_scr[...] = (x * lax.rsqrt(ms + NORM_EPS) * g_ref[...]).astype(BF16)

    o_ref[...] = jnp.dot(h_scr[...], w_ref[...], preferred_element_type=F32) + b_ref[...]


def norm_linear(x, g, w, b=None, row_tile=512, keep_padding=False):
    n, d = x.shape
    m = w.shape[1]
    mp = _round_up(m, 2 * LANES)
    wb = jnp.pad(w.astype(BF16), ((0, 0), (0, mp - m)))
    bb = jnp.zeros((1, mp), F32) if b is None else jnp.pad(b.astype(F32), (0, mp - m))[None]
    tn = _col_tile(mp)
    tm = min(row_tile, n)
    assert n % tm == 0
    out = pl.pallas_call(
        _norm_linear_kernel,
        grid=(n // tm, mp // tn),
        in_specs=[
            pl.BlockSpec((tm, d), lambda i, j: (i, 0)),
            pl.BlockSpec((1, d), lambda i, j: (0, 0)),
            pl.BlockSpec((d, tn), lambda i, j: (0, j)),
            pl.BlockSpec((1, tn), lambda i, j: (0, j)),
        ],
        out_specs=pl.BlockSpec((tm, tn), lambda i, j: (i, j)),
        out_shape=jax.ShapeDtypeStruct((n, mp), F32),
        scratch_shapes=[pltpu.VMEM((tm, d), BF16)],
        compiler_params=pltpu.CompilerParams(
            dimension_semantics=("arbitrary", "arbitrary"), vmem_limit_bytes=VMEM_LIMIT),
    )(x, g[None].astype(F32), wb, bb)
    return out if (keep_padding or mp == m) else out[:, :m]


def _linear_residual_kernel(x_ref, w_ref, b_ref, r_ref, o_ref):
    o_ref[...] = (r_ref[...] + b_ref[...]
                  + jnp.dot(x_ref[...].astype(BF16), w_ref[...], preferred_element_type=F32))


def linear_residual(x, w, b, res, row_tile=512):
    n, k = x.shape
    m = w.shape[1]
    tm = min(row_tile, n)
    assert n % tm == 0 and m % LANES == 0
    bb = jnp.zeros((1, m), F32) if b is None else b.astype(F32)[None]
    return pl.pallas_call(
        _linear_residual_kernel,
        grid=(n // tm,),
        in_specs=[
            pl.BlockSpec((tm, k), lambda i: (i, 0)),
            pl.BlockSpec((k, m), lambda i: (0, 0)),
            pl.BlockSpec((1, m), lambda i: (0, 0)),
            pl.BlockSpec((tm, m), lambda i: (i, 0)),
        ],
        out_specs=pl.BlockSpec((tm, m), lambda i: (i, 0)),
        out_shape=jax.ShapeDtypeStruct((n, m), F32),
        compiler_params=pltpu.CompilerParams(
            dimension_semantics=("arbitrary",), vmem_limit_bytes=VMEM_LIMIT),
    )(x, w.astype(BF16), bb, res)


PEER_TB = 128
PEER_E = PEER_HEADS * PEER_TOPK
N_EXPERTS = N_KEYS * N_KEYS
ROW_WORDS = D_MODEL // 2
ROW_SUB = ROW_WORDS // LANES
PSTRIDE = PEER_E + 8
PEER_DOWN_CHUNK = 32
NEG_INF = float("-inf")


def _top_rows(s, k):
    n = s.shape[0]
    rows = lax.broadcasted_iota(jnp.int32, s.shape, 0).astype(F32)
    out_rows = lax.broadcasted_iota(jnp.int32, (k, s.shape[1]), 0)
    vals = jnp.zeros((k, s.shape[1]), F32)
    ids = jnp.zeros((k, s.shape[1]), F32)
    for it in range(k):
        m = jnp.max(s, axis=0, keepdims=True)
        first = jnp.min(jnp.where(s == m, rows, float(n)), axis=0, keepdims=True)
        vals = jnp.where(out_rows == it, m, vals)
        ids = jnp.where(out_rows == it, first, ids)
        s = jnp.where(rows == first, NEG_INF, s)
    return vals, ids


def _peer_select_kernel(x_ref, g_ref, wq_ref, sk_ref, h_ref, idx_ref, gate_ref):
    x = x_ref[...]
    ms = jnp.mean(x * x, axis=-1, keepdims=True)
    h = x * lax.rsqrt(ms + NORM_EPS) * g_ref[...]
    h_ref[...] = h
    q = jnp.dot(h.astype(BF16), wq_ref[...], preferred_element_type=F32).astype(BF16)
    half = PEER_DK // 2
    tb = x.shape[0]
    crow = lax.broadcasted_iota(jnp.int32, (TOPK_HALF * TOPK_HALF, tb), 0).astype(F32)
    orow = lax.broadcasted_iota(jnp.int32, (PEER_TOPK, tb), 0)
    ids = []
    for hd in range(PEER_HEADS):
        sv, si = [], []
        for p in range(2):
            c = (hd * 2 + p) * half
            s = lax.dot_general(sk_ref[hd * 2 + p], q[:, c:c + half],
                                (((1,), (1,)), ((), ())), preferred_element_type=F32)
            v, i = _top_rows(s, TOPK_HALF)
            sv.append(v)
            si.append(i)
        cand = jnp.concatenate([sv[0][i:i + 1] + sv[1] for i in range(TOPK_HALF)], axis=0)
        eid = jnp.concatenate([si[0][i:i + 1] * float(N_KEYS) + si[1] for i in range(TOPK_HALF)], axis=0)
        cs = jnp.zeros((PEER_TOPK, tb), F32)
        ce = jnp.zeros((PEER_TOPK, tb), F32)
        for it in range(PEER_TOPK):
            m = jnp.max(cand, axis=0, keepdims=True)
            first = jnp.min(jnp.where(cand == m, crow, float(crow.shape[0])), axis=0, keepdims=True)
            hit = crow == first
            e = jnp.max(jnp.where(hit, eid, -1.0), axis=0, keepdims=True)
            cs = jnp.where(orow == it, m, cs)
            ce = jnp.where(orow == it, e, ce)
            cand = jnp.where(hit, NEG_INF, cand)
        ex = jnp.exp(cs - cs[0:1])
        gates = ex / jnp.sum(ex, axis=0, keepdims=True)
        ids.append(ce * float(ROW_SUB))
        gate_ref[0, hd * PEER_TOPK:(hd + 1) * PEER_TOPK, :] = gates
    idx_ref[0] = jnp.concatenate(ids, axis=0).T.astype(jnp.int32)


def _table_row(tab, first_row):
    return tab[pl.ds(pl.multiple_of(first_row, ROW_SUB), ROW_SUB), :]


def _unpack_row(row):
    lo = pltpu.bitcast(lax.shift_left(row, 16), F32)
    hi = pltpu.bitcast(jnp.bitwise_and(row, jnp.int32(-65536)), F32)
    return lo, hi


def _load_block_scalars(src_hbm, dst_smem, sem):
    cp = pltpu.make_async_copy(src_hbm.at[pl.program_id(0)], dst_smem, sem)
    cp.start()
    cp.wait()


def _load_table_once(tab_hbm, tab_vmem, sem):
    @pl.when(pl.program_id(0) == 0)
    def _():
        cp = pltpu.make_async_copy(tab_hbm, tab_vmem, sem)
        cp.start()
        cp.wait()


def _peer_up_kernel(idx_hbm, tab_hbm, h_ref, gate_ref, w_ref, tab, idx_s, pbuf_a, pbuf_b, act_s, sems):
    _load_table_once(tab_hbm, tab, sems.at[0])
    _load_block_scalars(idx_hbm, idx_s, sems.at[1])
    lane = lax.broadcasted_iota(jnp.int32, (PEER_E, PEER_TB), 1)
    act_s[...] = jnp.zeros((PEER_E, PEER_TB), F32)
    pbuf_b[...] = jnp.zeros(pbuf_b.shape, F32)

    def gather(t, pbuf):
        ht = h_ref[t]
        h_lo, h_hi = ht[0:ROW_SUB], ht[ROW_SUB:2 * ROW_SUB]
        base = t * PEER_E
        for e in range(PEER_E):
            lo, hi = _unpack_row(_table_row(tab, idx_s[base + e]))
            pbuf[pl.ds(e, ROW_SUB, stride=PSTRIDE), :] = lo * h_lo + hi * h_hi

    def reduce(t, pbuf):
        acc = pbuf[0:PEER_E, :]
        for c in range(1, ROW_SUB):
            acc = acc + pbuf[c * PSTRIDE:c * PSTRIDE + PEER_E, :]
        col = jnp.sum(acc, axis=-1, keepdims=True)
        act_s[...] = jnp.where(lane == t, col, act_s[...])

    def token_pair(j, carry):
        reduce(2 * j - 1, pbuf_b)
        gather(2 * j, pbuf_a)
        reduce(2 * j, pbuf_a)
        gather(2 * j + 1, pbuf_b)
        return carry

    lax.fori_loop(0, PEER_TB // 2, token_pair, 0)
    reduce(PEER_TB - 1, pbuf_b)
    a = act_s[...]
    w_ref[0] = (gate_ref[0] * (0.5 * a * (1.0 + lax.erf(a * (2.0 ** -0.5))))).T


def _peer_down_kernel(idx_hbm, w_hbm, tab_hbm, x_ref, o_ref, tab, idx_s, w_s, sems):
    _load_table_once(tab_hbm, tab, sems.at[0])
    _load_block_scalars(idx_hbm, idx_s, sems.at[1])
    _load_block_scalars(w_hbm, w_s, sems.at[2])
    n_acc = 4
    zero = jnp.zeros((ROW_SUB, LANES), F32)

    def token(t, carry):
        def chunk(c, accs):
            acc_lo, acc_hi = list(accs[:n_acc]), list(accs[n_acc:])
            base = t * PEER_E + c * PEER_DOWN_CHUNK
            for j in range(PEER_DOWN_CHUNK):
                lo, hi = _unpack_row(_table_row(tab, idx_s[base + j]))
                w = w_s[base + j]
                acc_lo[j % n_acc] = acc_lo[j % n_acc] + w * lo
                acc_hi[j % n_acc] = acc_hi[j % n_acc] + w * hi
            return tuple(acc_lo) + tuple(acc_hi)

        accs = lax.fori_loop(0, PEER_E // PEER_DOWN_CHUNK, chunk, (zero,) * (2 * n_acc))
        lo = (accs[0] + accs[1]) + (accs[2] + accs[3])
        hi = (accs[4] + accs[5]) + (accs[6] + accs[7])
        o_ref[t] = x_ref[t] + jnp.concatenate([lo, hi], axis=0)
        return carry

    lax.fori_loop(0, PEER_TB, token, 0)


def _pack_table(tab):
    n = tab.shape[0]
    bits = lax.bitcast_convert_type(tab.astype(jnp.bfloat16), jnp.uint16).astype(jnp.uint32)
    words = bits[:, :ROW_WORDS] | (bits[:, ROW_WORDS:] << 16)
    return lax.bitcast_convert_type(words, jnp.int32).reshape(n * ROW_SUB, LANES)


def peer(x, g, wq, subkeys, u_tab, v_tab):
    n, d = x.shape
    assert n % PEER_TB == 0 and d == D_MODEL
    nb = n // PEER_TB
    sk = subkeys.reshape(PEER_HEADS * 2, N_KEYS, PEER_DK // 2).astype(BF16)
    params = pltpu.CompilerParams(dimension_semantics=("arbitrary",), vmem_limit_bytes=VMEM_LIMIT)
    h, idx, gates = pl.pallas_call(
        _peer_select_kernel,
        grid=(nb,),
        in_specs=[
            pl.BlockSpec((PEER_TB, d), lambda i: (i, 0)),
            pl.BlockSpec((1, d), lambda i: (0, 0)),
            pl.BlockSpec((d, PEER_HEADS * PEER_DK), lambda i: (0, 0)),
            pl.BlockSpec((PEER_HEADS * 2, N_KEYS, PEER_DK // 2), lambda i: (0, 0, 0)),
        ],
        out_specs=[
            pl.BlockSpec((PEER_TB, d), lambda i: (i, 0)),
            pl.BlockSpec((1, PEER_TB, PEER_E), lambda i: (i, 0, 0)),
            pl.BlockSpec((1, PEER_E, PEER_TB), lambda i: (i, 0, 0)),
        ],
        out_shape=[
            jax.ShapeDtypeStruct((n, d), F32),
            jax.ShapeDtypeStruct((nb, PEER_TB, PEER_E), jnp.int32),
            jax.ShapeDtypeStruct((nb, PEER_E, PEER_TB), F32),
        ],
        compiler_params=params,
        name="peer_select",
    )(x, g[None].astype(F32), wq.astype(BF16), sk)

    table_params = pltpu.CompilerParams(dimension_semantics=("arbitrary",),
                                        vmem_limit_bytes=PEER_TABLE_VMEM_LIMIT)
    tok_tiles = (PEER_TB, d // LANES, LANES)
    w = pl.pallas_call(
        _peer_up_kernel,
        grid=(nb,),
        in_specs=[
            pl.BlockSpec(memory_space=pl.ANY),
            pl.BlockSpec(memory_space=pl.ANY),
            pl.BlockSpec(tok_tiles, lambda i: (i, 0, 0)),
            pl.BlockSpec((1, PEER_E, PEER_TB), lambda i: (i, 0, 0)),
        ],
        out_specs=pl.BlockSpec((1, PEER_TB, PEER_E), lambda i: (i, 0, 0)),
        out_shape=jax.ShapeDtypeStruct((nb, PEER_TB, PEER_E), F32),
        scratch_shapes=[
            pltpu.VMEM((N_EXPERTS * ROW_SUB, LANES), jnp.int32),
            pltpu.SMEM((PEER_TB * PEER_E,), jnp.int32),
            pltpu.VMEM((ROW_SUB * PSTRIDE, LANES), F32),
            pltpu.VMEM((ROW_SUB * PSTRIDE, LANES), F32),
            pltpu.VMEM((PEER_E, PEER_TB), F32),
            pltpu.SemaphoreType.DMA((2,)),
        ],
        compiler_params=table_params,
        name="peer_up",
    )(idx.reshape(nb, PEER_TB * PEER_E), _pack_table(u_tab), h.reshape(n, d // LANES, LANES), gates)

    out = pl.pallas_call(
        _peer_down_kernel,
        grid=(nb,),
        in_specs=[
            pl.BlockSpec(memory_space=pl.ANY),
            pl.BlockSpec(memory_space=pl.ANY),
            pl.BlockSpec(memory_space=pl.ANY),
            pl.BlockSpec(tok_tiles, lambda i: (i, 0, 0)),
        ],
        out_specs=pl.BlockSpec(tok_tiles, lambda i: (i, 0, 0)),
        out_shape=jax.ShapeDtypeStruct((n, d // LANES, LANES), F32),
        scratch_shapes=[
            pltpu.VMEM((N_EXPERTS * ROW_SUB, LANES), jnp.int32),
            pltpu.SMEM((PEER_TB * PEER_E,), jnp.int32),
            pltpu.SMEM((PEER_TB * PEER_E,), F32),
            pltpu.SemaphoreType.DMA((3,)),
        ],
        compiler_params=table_params,
        name="peer_down",
    )(idx.reshape(nb, PEER_TB * PEER_E), w.reshape(nb, PEER_TB * PEER_E), _pack_table(v_tab),
      x.reshape(n, d // LANES, LANES))
    return out.reshape(n, d)


DSA_KT = 256
DSA_NEAR = 3
DSA_POS_BITS = 14
DSA_COUNT_TILES = 4
CHUNK_SHIFT = CHUNK.bit_length() - 1
INT_MIN = -2 ** 31
KEY_NEG_INF = 0x807FFFFF - 2 ** 32
_NT = (((1,), (1,)), ((), ()))


def _sort_key(x):
    b = pltpu.bitcast(x, jnp.int32)
    return b ^ ((b >> 31) & 0x7FFFFFFF)


def _dsa_kernel(qpad_ref, qi_ref, wit_ref, k_ref, vt_ref, ki_ref, nb_ref, o_ref,
                keys_scr, s_scr, p_scr, *acc_refs, offset, n_keys, topk, qb):
    kt_ = DSA_KT
    q0 = offset + pl.program_id(1) * qb
    qpos = q0 + lax.broadcasted_iota(jnp.int32, (1, qb), 1)
    vis_end = jnp.minimum((lax.shift_right_logical(qpos, CHUNK_SHIFT) + 1) * CHUNK, n_keys)
    blk_end = jnp.minimum(((q0 + qb - 1) // CHUNK + 1) * CHUNK, n_keys)
    n_tiles = (blk_end + kt_ - 1) // kt_
    row = lax.broadcasted_iota(jnp.int32, (kt_, qb), 0)

    qi = qi_ref[0]
    qis = [qi[:, h * D_IDX:(h + 1) * D_IDX] for h in range(H_I)]
    wit = wit_ref[0] * (H_I ** -0.5)

    def score_tile(t, c):
        k0 = pl.multiple_of(t * kt_, kt_)
        kit = ki_ref[0, pl.ds(k0, kt_), :]
        idx = jnp.zeros((kt_, qb), F32)
        for h in range(H_I):
            s = lax.dot_general(kit, qis[h], _NT, preferred_element_type=F32)
            idx = idx + jnp.maximum(s * (D_IDX ** -0.5), 0.0) * wit[h:h + 1]
        idx = jnp.where(idx == 0.0, 0.0, idx)
        keys_scr[pl.ds(k0, kt_), :] = jnp.where(row + k0 < vis_end, _sort_key(idx), KEY_NEG_INF)
        return c

    lax.fori_loop(0, n_tiles, score_tile, 0)

    n_steps = (n_tiles + DSA_COUNT_TILES - 1) // DSA_COUNT_TILES

    def pad_tile(t, c):
        keys_scr[pl.ds(pl.multiple_of(t * kt_, kt_), kt_), :] = jnp.full((kt_, qb), KEY_NEG_INF, jnp.int32)
        return c

    lax.fori_loop(n_tiles, n_steps * DSA_COUNT_TILES, pad_tile, 0)

    def count(preds, n_out):
        def body(s, accs):
            accs = list(accs)
            for u in range(DSA_COUNT_TILES):
                k0 = pl.multiple_of((s * DSA_COUNT_TILES + u) * kt_, kt_)
                hits = preds(keys_scr[pl.ds(k0, kt_), :], row + k0)
                for i in range(n_out):
                    accs[i] = accs[i] + jnp.sum(hits[i].reshape(kt_ // 8, 8, qb), axis=0)
            return tuple(accs)
        accs = lax.fori_loop(0, n_steps, body, (jnp.zeros((8, qb), F32),) * n_out)
        return [jnp.sum(acc, axis=0, keepdims=True) for acc in accs]

    one = lambda hit: jnp.where(hit, 1.0, 0.0)
    bit = lambda n: lax.shift_left(jnp.int32(1), n)
    kf = float(topk)
    c0, = count(lambda kt, kp: (one(kt >= 0),), 1)
    thr0 = jnp.where(c0 >= kf, 0, INT_MIN).astype(jnp.int32)

    def thr_bits(j, thr):
        hi, lo = bit(30 - 2 * j), bit(29 - 2 * j)
        c_hl, c_h, c_l = count(lambda kt, kp: (one(kt >= (thr | hi | lo)), one(kt >= (thr | hi)),
                                                one(kt >= (thr | lo))), 3)
        take_hi = c_h >= kf
        take_lo = jnp.where(take_hi, c_hl, c_l) >= kf
        return thr | jnp.where(take_hi, hi, 0) | jnp.where(take_lo, lo, 0)

    thr = lax.fori_loop(0, 15, thr_bits, thr0)
    c_last, = count(lambda kt, kp: (one(kt >= (thr | 1)),), 1)
    thr = jnp.where(c_last >= kf, thr | 1, thr)
    c_gt, c_eq = count(lambda kt, kp: (one(kt > thr), one(kt == thr)), 2)
    need = kf - c_gt
    select_all = vis_end <= topk

    def tie_cut():
        def pos_bits(j, lo_pos):
            hi, lo = bit(DSA_POS_BITS - 1 - 2 * j), bit(DSA_POS_BITS - 2 - 2 * j)
            tied = lambda kt, kp, bound: jnp.where(kt == thr, one(kp < bound), 0.0)
            f_hl, f_h, f_l = count(lambda kt, kp: (tied(kt, kp, lo_pos + hi + lo), tied(kt, kp, lo_pos + hi),
                                                    tied(kt, kp, lo_pos + lo)), 3)
            take_hi = f_h < need
            take_lo = jnp.where(take_hi, f_hl, f_l) < need
            return lo_pos + jnp.where(take_hi, hi, 0) + jnp.where(take_lo, lo, 0)
        return lax.fori_loop(0, DSA_POS_BITS // 2, pos_bits, jnp.zeros((1, qb), jnp.int32)) + 1

    surplus = jnp.max(jnp.where((c_eq > need) & jnp.logical_not(select_all), 1.0, 0.0)) > 0.0
    cut = lax.cond(surplus, tie_cut, lambda: jnp.full((1, qb), 2 ** DSA_POS_BITS, jnp.int32))
    thr = jnp.where(select_all, KEY_NEG_INF, thr)
    cut = jnp.where(select_all, 0, cut)

    for acc in acc_refs:
        acc[...] = jnp.zeros(acc.shape, F32)

    def attend_tile(t, carry):
        m_all, l_all = carry
        k0 = pl.multiple_of(t * kt_, kt_)
        keys = keys_scr[pl.ds(k0, kt_), :]
        sel = (keys - jnp.where(row + k0 < cut, 0, 1)) >= thr
        step = (k0 - q0 + (DSA_NEAR - 1) * LANES) // LANES
        nidx = jnp.where(step < 0, DSA_NEAR, step)
        for h in range(H_B):
            kh = k_ref[0, pl.ds(k0, kt_), (h // 2) * LANES:(h // 2 + 1) * LANES]
            qh = qpad_ref[0, :, h * LANES:(h + 1) * LANES]
            s_scr[h] = lax.dot_general(kh, qh, _NT, preferred_element_type=F32)
        m_rows, l_rows, alphas = [], [], []
        for h in range(H_B):
            s = jnp.where(sel, s_scr[h] * (DH_B ** -0.5) + nb_ref[nidx, h], NEG_INF)
            m_old = m_all[h:h + 1]
            m_new = jnp.maximum(m_old, jnp.max(s, axis=0, keepdims=True))
            m_safe = jnp.where(m_new == NEG_INF, 0.0, m_new)
            p = jnp.exp(s - m_safe)
            alpha = jnp.exp(m_old - m_safe)
            p_scr[h] = p.astype(BF16)
            m_rows.append(m_new)
            l_rows.append(alpha * l_all[h:h + 1] + jnp.sum(p, axis=0, keepdims=True))
            alphas.append(alpha)
        for h in range(H_B):
            vth = vt_ref[0, h * DH_B:(h + 1) * DH_B, pl.ds(k0, kt_)]
            acc_refs[h][...] = (alphas[h] * acc_refs[h][...]
                                + jnp.dot(vth, p_scr[h], preferred_element_type=F32))
        return jnp.concatenate(m_rows, axis=0), jnp.concatenate(l_rows, axis=0)

    _, l_all = lax.fori_loop(0, n_tiles, attend_tile,
                             (jnp.full((H_B, qb), NEG_INF, F32), jnp.zeros((H_B, qb), F32)))
    for h in range(H_B):
        o_ref[0, h * DH_B:(h + 1) * DH_B] = acc_refs[h][...] / l_all[h:h + 1]


def _near_bias_tiles(rel_bias, qb):
    k = jnp.arange(DSA_KT, dtype=jnp.int32)[:, None]
    q = jnp.arange(qb, dtype=jnp.int32)[None, :]
    rels = [(j - (DSA_NEAR - 1)) * LANES + k - q for j in range(DSA_NEAR)]
    rels.append(jnp.full((DSA_KT, qb), -(DSA_NEAR * LANES + DSA_KT), jnp.int32))
    tiles = rel_bias[_t5_bucket(jnp.stack(rels))]
    return jnp.moveaxis(tiles, -1, 1).astype(F32)


def dsa_attention(qpad, k_all, v_all, qi, wi, ki_all, rel_bias, offset):
    bx, t, _ = qpad.shape
    n_keys = k_all.shape[1]
    topk = min(TOPK_MAX, n_keys // 4)
    qb = Q_BLOCK if t % Q_BLOCK == 0 else t
    lp = _round_up(n_keys, DSA_KT)
    assert lp <= 2 ** DSA_POS_BITS and offset % DSA_KT == 0 and (qb == Q_BLOCK or t == qb)
    pad = ((0, 0), (0, lp - n_keys), (0, 0))
    kk = jnp.pad(k_all, pad).astype(BF16)
    vt = jnp.pad(v_all, pad).astype(BF16).transpose(0, 2, 1)
    ki = jnp.pad(ki_all, pad).astype(BF16)
    wit = jnp.pad(wi.astype(F32).transpose(0, 2, 1), ((0, 0), (0, 8 - H_I), (0, 0)))
    nb = _near_bias_tiles(rel_bias, qb)
    kern = functools.partial(_dsa_kernel, offset=offset, n_keys=n_keys, topk=topk, qb=qb)
    ot = pl.pallas_call(
        kern,
        grid=(bx, t // qb),
        in_specs=[
            pl.BlockSpec((1, qb, H_B * LANES), lambda b, i: (b, i, 0)),
            pl.BlockSpec((1, qb, H_I * D_IDX), lambda b, i: (b, i, 0)),
            pl.BlockSpec((1, 8, qb), lambda b, i: (b, 0, i)),
            pl.BlockSpec((1, lp, D_B), lambda b, i: (b, 0, 0)),
            pl.BlockSpec((1, D_B, lp), lambda b, i: (b, 0, 0)),
            pl.BlockSpec((1, lp, D_IDX), lambda b, i: (b, 0, 0)),
            pl.BlockSpec((DSA_NEAR + 1, H_B, DSA_KT, qb), lambda b, i: (0, 0, 0, 0)),
        ],
        out_specs=pl.BlockSpec((1, D_B, qb), lambda b, i: (b, 0, i)),
        out_shape=jax.ShapeDtypeStruct((bx, D_B, t), F32),
        scratch_shapes=[pltpu.VMEM((_round_up(n_keys, DSA_KT * DSA_COUNT_TILES), qb), jnp.int32),
                        pltpu.VMEM((H_B, DSA_KT, qb), F32),
                        pltpu.VMEM((H_B, DSA_KT, qb), BF16)]
        + [pltpu.VMEM((DH_B, qb), F32)] * H_B,
        compiler_params=pltpu.CompilerParams(
            dimension_semantics=("arbitrary", "arbitrary"), vmem_limit_bytes=PEER_TABLE_VMEM_LIMIT),
        name="dsa_attention",
    )(qpad, qi.astype(BF16), wit, kk, vt, ki, nb)
    return ot.transpose(0, 2, 1)


WKV_CHUNK = 64
WKV_BLOCK = 256
_TN = (((0,), (0,)), ((), ()))


def _wkv_kernel(r_ref, lw_ref, k_ref, v_ref, kk_ref, a_ref, s0_ref, o_ref, st_ref, s_scr, *, chunk, n_chunks):
    c = chunk

    @pl.when(pl.program_id(1) == 0)
    def _():
        s_scr[...] = s0_ref[0]

    row = lax.broadcasted_iota(jnp.int32, (c, c), 0)
    col = lax.broadcasted_iota(jnp.int32, (c, c), 1)
    strict, incl = row > col, row >= col
    tri = jnp.where(incl, 1.0, 0.0)
    eye = jnp.where(row == col, 1.0, 0.0)
    bdot = lambda x, y: jnp.dot(x.astype(BF16), y.astype(BF16), preferred_element_type=F32)
    bdot_nt = lambda x, y: lax.dot_general(x.astype(BF16), y.astype(BF16), _NT, preferred_element_type=F32)
    bdot_tn = lambda x, y: lax.dot_general(x.astype(BF16), y.astype(BF16), _TN, preferred_element_type=F32)

    def solve_chunk(ci, carry):
        c0 = pl.multiple_of(ci * c, c)
        rows = pl.ds(c0, c)
        outs = []
        for h in range(H_A):
            hs = slice(h * DH_A, (h + 1) * DH_A)
            r, lw, k, v = r_ref[0, rows, hs], lw_ref[0, rows, hs], k_ref[0, rows, hs], v_ref[0, rows, hs]
            kap, a = kk_ref[0, rows, hs], a_ref[0, rows, hs]
            cum = jnp.dot(tri, lw, preferred_element_type=F32, precision=lax.Precision.HIGHEST)
            p, p_inv, p_prev = jnp.exp(cum), jnp.exp(-cum), jnp.exp(cum - lw)
            kap_t, r_t, k_b, b_b = kap * p_prev, r * p, k * p_inv, kap * a * p_inv
            g = bdot_nt(jnp.concatenate([kap_t, r_t], axis=0), jnp.concatenate([k_b, b_b], axis=0))
            a_kk, a_kb, a_rk, a_rb = g[:c, :c], g[:c, c:], g[c:, :c], g[c:, c:]
            nil = jnp.where(strict, -a_kb, 0.0)
            inv, power = eye + nil, nil
            for _ in range(c.bit_length() - 2):
                power = bdot(power, power)
                inv = inv + bdot(inv, power)
            s_prev = s_scr[h]
            u = bdot(inv, bdot_nt(kap_t, s_prev) + bdot(jnp.where(strict, a_kk, 0.0), v))
            outs.append(bdot_nt(r_t, s_prev) + bdot(jnp.where(incl, a_rk, 0.0), v)
                        - bdot(jnp.where(incl, a_rb, 0.0), u))
            s_scr[h] = (s_prev + bdot_tn(v, k_b) - bdot_tn(u, b_b)) * p[c - 1:c, :]
        o_ref[0, rows, :] = jnp.concatenate(outs, axis=1)
        return carry

    lax.fori_loop(0, n_chunks, solve_chunk, 0)

    @pl.when(pl.program_id(1) == pl.num_programs(1) - 1)
    def _():
        st_ref[0] = s_scr[...]


def wkv_chunked(r, lw, k, v, kk, a, s0):
    bx, t, _ = r.shape
    chunk = min(WKV_CHUNK, t)
    blk = min(WKV_BLOCK, t)
    assert t % blk == 0 and blk % chunk == 0 and chunk & (chunk - 1) == 0
    tok = pl.BlockSpec((1, blk, D_A), lambda b, i: (b, i, 0))
    state = pl.BlockSpec((1, H_A, DH_A, DH_A), lambda b, i: (b, 0, 0, 0))
    return pl.pallas_call(
        functools.partial(_wkv_kernel, chunk=chunk, n_chunks=blk // chunk),
        grid=(bx, t // blk),
        in_specs=[tok] * 6 + [state],
        out_specs=[tok, state],
        out_shape=[jax.ShapeDtypeStruct((bx, t, D_A), F32),
                   jax.ShapeDtypeStruct((bx, H_A, DH_A, DH_A), F32)],
        scratch_shapes=[pltpu.VMEM((H_A, DH_A, DH_A), F32)],
        compiler_params=pltpu.CompilerParams(
            dimension_semantics=("arbitrary", "arbitrary"), vmem_limit_bytes=VMEM_LIMIT),
        name="wkv_chunked",
    )(r, lw, k, v, kk, a, s0.astype(F32))


CONV_HALO = 32
CONV_BLOCK = 256


def _conv_kernel(u_ref, prev_ref, w_ref, b_ref, g_ref, beta_ref, y_ref, state_ref, pad_scr, *, blk):
    lead = CONV_HALO - (CONV_W - 1)

    @pl.when(pl.program_id(1) == 0)
    def _():
        pad_scr[0:lead, :] = jnp.zeros((lead, D_CONV), F32)
        pad_scr[lead:CONV_HALO, :] = prev_ref[0]

    @pl.when(pl.program_id(1) > 0)
    def _():
        pad_scr[0:CONV_HALO, :] = pad_scr[blk:blk + CONV_HALO, :]

    u = u_ref[0]
    pad_scr[CONV_HALO:CONV_HALO + blk, :] = u[:, :D_CONV] * jax.nn.sigmoid(u[:, D_CONV:])
    y = jnp.zeros((blk, D_CONV), F32) + b_ref[...]
    for j in range(CONV_W):
        y = y + w_ref[j:j + 1, :] * pad_scr[lead + j:lead + j + blk, :]
    m = jnp.mean(y, axis=-1, keepdims=True)
    var = jnp.mean(jnp.square(y - m), axis=-1, keepdims=True)
    z = (y - m) * lax.rsqrt(var + 1e-5) * g_ref[...] + beta_ref[...]
    y_ref[0] = z * jax.nn.sigmoid(z)

    @pl.when(pl.program_id(1) == pl.num_programs(1) - 1)
    def _():
        state_ref[0] = pad_scr[blk + lead:blk + CONV_HALO, :]


def conv_module(u, conv_prev, w_dw, b_dw, ln_g, ln_b):
    bx, t, _ = u.shape
    blk = min(CONV_BLOCK, t)
    assert t % blk == 0 and blk % 8 == 0
    vec = pl.BlockSpec((1, D_CONV), lambda b, i: (0, 0))
    return pl.pallas_call(
        functools.partial(_conv_kernel, blk=blk),
        grid=(bx, t // blk),
        in_specs=[
            pl.BlockSpec((1, blk, 2 * D_CONV), lambda b, i: (b, i, 0)),
            pl.BlockSpec((1, CONV_W - 1, D_CONV), lambda b, i: (b, 0, 0)),
            pl.BlockSpec((CONV_W, D_CONV), lambda b, i: (0, 0)),
            vec, vec, vec,
        ],
        out_specs=[
            pl.BlockSpec((1, blk, D_CONV), lambda b, i: (b, i, 0)),
            pl.BlockSpec((1, CONV_W - 1, D_CONV), lambda b, i: (b, 0, 0)),
        ],
        out_shape=[jax.ShapeDtypeStruct((bx, t, D_CONV), F32),
                   jax.ShapeDtypeStruct((bx, CONV_W - 1, D_CONV), F32)],
        scratch_shapes=[pltpu.VMEM((CONV_HALO + blk, D_CONV), F32)],
        compiler_params=pltpu.CompilerParams(
            dimension_semantics=("arbitrary", "arbitrary"), vmem_limit_bytes=VMEM_LIMIT),
        name="conv_module",
    )(u, conv_prev.astype(F32), w_dw.astype(F32), b_dw[None].astype(F32), ln_g[None].astype(F32),
      ln_b[None].astype(F32))


EVEN_BLOCK = 256
_HI = lax.Precision.HIGHEST


def _head_sum(x, hs):
    return jnp.dot(x, hs, preferred_element_type=F32, precision=_HI)


def _even_prep_kernel(p_ref, shift_ref, mu_ref, w0_ref, a0_ref, wdu_ref, wau_ref, wgu_ref, kk_ref, ka_ref,
                      qn_ref, kn_ref, hs_ref,
                      r_o, lw_o, k_o, v_o, kap_o, a_o, gate_o, qpad_o, kn_o, last_scr, *, blk):
    @pl.when(pl.program_id(1) == 0)
    def _():
        last_scr[0:1, :] = shift_ref[0]

    p = p_ref[0]
    pa = p[:, :A_PROJ]
    first = lax.broadcasted_iota(jnp.int32, pa.shape, 0) == 0
    prev = jnp.where(first, last_scr[0:1, :], pltpu.roll(pa, 1, axis=0))
    last_scr[0:1, :] = pa[blk - 1:blk, :]
    xm = pa + (prev - pa) * mu_ref[...]
    r, k, v = xm[:, :D_A], xm[:, D_A:2 * D_A], xm[:, 2 * D_A:3 * D_A]
    c = 3 * D_A
    dw, da, dg = xm[:, c:c + LORA_W], xm[:, c + LORA_W:c + LORA_W + LORA_A], xm[:, c + LORA_W + LORA_A:A_PROJ]
    bdot = lambda x, w: jnp.dot(x.astype(BF16), w[...], preferred_element_type=F32)
    z = -(w0_ref[...] + bdot(jnp.tanh(dw), wdu_ref))
    w_log = -(jnp.maximum(z, 0.0) + jnp.log(1.0 + jnp.exp(-jnp.abs(z)))) - 0.5
    a = jax.nn.sigmoid(a0_ref[...] + bdot(da, wau_ref))
    kap = k * kk_ref[...]
    hs = hs_ref[...]
    kap = kap / jnp.maximum(jnp.sqrt(_head_sum(kap * kap, hs)), 1e-12)
    r_o[0] = r
    lw_o[0] = -jnp.exp(w_log)
    k_o[0] = k * (1.0 + (a - 1.0) * ka_ref[...])
    v_o[0] = v
    kap_o[0] = kap
    a_o[0] = a
    gate_o[0] = bdot(jax.nn.sigmoid(dg), wgu_ref)

    q = p[:, A_PROJ:A_PROJ + D_B]
    kb = p[:, A_PROJ + D_B:A_PROJ + 2 * D_B]
    qn = q * lax.rsqrt(_head_sum(q * q, hs) * (1.0 / DH_B) + NORM_EPS) * qn_ref[...]
    kn_o[0] = kb * lax.rsqrt(_head_sum(kb * kb, hs) * (1.0 / DH_B) + NORM_EPS) * kn_ref[...]
    upper = lax.broadcasted_iota(jnp.int32, (blk, LANES), 1) >= DH_B
    chunks = []
    for h in range(H_B):
        pair = qn[:, (h // 2) * LANES:(h // 2 + 1) * LANES]
        chunks.append(jnp.where(upper == (h % 2 == 1), pair, 0.0))
    qpad_o[0] = jnp.concatenate(chunks, axis=1).astype(BF16)


def even_prep(p, shift_prev, mu, w0, w_du, a0, w_au, w_gu, k_k, k_a, qn_g, kn_g):
    bx, t, width = p.shape
    blk = min(EVEN_BLOCK, t)
    assert t % blk == 0
    head = jnp.arange(D_A) // DH_A
    hs = (head[:, None] == head[None, :]).astype(F32)
    row = lambda v: v.reshape(1, -1).astype(F32)
    full = lambda a: pl.BlockSpec(a.shape, lambda b, i: (0,) * a.ndim)
    consts = (row(mu), row(w0), row(a0), w_du.astype(BF16), w_au.astype(BF16), w_gu.astype(BF16),
              row(k_k), row(k_a), row(jnp.tile(qn_g, H_B)), row(jnp.tile(kn_g, H_B)), hs)
    tok = pl.BlockSpec((1, blk, D_A), lambda b, i: (b, i, 0))
    return pl.pallas_call(
        functools.partial(_even_prep_kernel, blk=blk),
        grid=(bx, t // blk),
        in_specs=[pl.BlockSpec((1, blk, width), lambda b, i: (b, i, 0)),
                  pl.BlockSpec((1, 1, A_PROJ), lambda b, i: (b, 0, 0))] + [full(c) for c in consts],
        out_specs=[tok] * 7 + [pl.BlockSpec((1, blk, H_B * LANES), lambda b, i: (b, i, 0)), tok],
        out_shape=[jax.ShapeDtypeStruct((bx, t, D_A), F32)] * 7
        + [jax.ShapeDtypeStruct((bx, t, H_B * LANES), BF16), jax.ShapeDtypeStruct((bx, t, D_B), F32)],
        scratch_shapes=[pltpu.VMEM((8, A_PROJ), F32)],
        compiler_params=pltpu.CompilerParams(
            dimension_semantics=("arbitrary", "arbitrary"), vmem_limit_bytes=VMEM_LIMIT),
        name="even_prep",
    )(p, shift_prev[:, None, :].astype(F32), *consts)


def _rwkv_post_kernel(o_ref, r_ref, k_ref, v_ref, gate_ref, rk_ref, gw_ref, gb_ref, hs_ref, out_ref):
    hs = hs_ref[...]
    o = o_ref[...]
    d = o - _head_sum(o, hs) * (1.0 / DH_A)
    var = _head_sum(d * d, hs) * (1.0 / DH_A)
    on = d * lax.rsqrt(var + GN_EPS) * gw_ref[...] + gb_ref[...]
    bonus = _head_sum(r_ref[...] * k_ref[...] * rk_ref[...], hs) * v_ref[...]
    out_ref[...] = (on + bonus) * gate_ref[...]


def rwkv_post(o, r, k, v, gate, r_k, gn_w, gn_b):
    n = o.shape[0]
    blk = min(512, n)
    assert n % blk == 0
    head = jnp.arange(D_A) // DH_A
    hs = (head[:, None] == head[None, :]).astype(F32)
    row = lambda v: v.reshape(1, -1).astype(F32)
    tok = pl.BlockSpec((blk, D_A), lambda i: (i, 0))
    vec = pl.BlockSpec((1, D_A), lambda i: (0, 0))
    return pl.pallas_call(
        _rwkv_post_kernel,
        grid=(n // blk,),
        in_specs=[tok] * 5 + [vec] * 3 + [pl.BlockSpec((D_A, D_A), lambda i: (0, 0))],
        out_specs=tok,
        out_shape=jax.ShapeDtypeStruct((n, D_A), F32),
        compiler_params=pltpu.CompilerParams(
            dimension_semantics=("arbitrary",), vmem_limit_bytes=VMEM_LIMIT),
        name="rwkv_post",
    )(o, r, k, v, gate, row(r_k), row(gn_w), row(gn_b), hs)


def _linear2_residual_kernel(xa_ref, xb_ref, wa_ref, wb_ref, r_ref, o_ref):
    o_ref[...] = (r_ref[...]
                  + jnp.dot(xa_ref[...].astype(BF16), wa_ref[...], preferred_element_type=F32)
                  + jnp.dot(xb_ref[...].astype(BF16), wb_ref[...], preferred_element_type=F32))


def linear2_residual(xa, xb, w, res, row_tile=512):
    n, ka = xa.shape
    kb = xb.shape[1]
    m = w.shape[1]
    tm = min(row_tile, n)
    assert n % tm == 0 and w.shape[0] == ka + kb
    wb16 = w.astype(BF16)
    return pl.pallas_call(
        _linear2_residual_kernel,
        grid=(n // tm,),
        in_specs=[
            pl.BlockSpec((tm, ka), lambda i: (i, 0)),
            pl.BlockSpec((tm, kb), lambda i: (i, 0)),
            pl.BlockSpec((ka, m), lambda i: (0, 0)),
            pl.BlockSpec((kb, m), lambda i: (0, 0)),
            pl.BlockSpec((tm, m), lambda i: (i, 0)),
        ],
        out_specs=pl.BlockSpec((tm, m), lambda i: (i, 0)),
        out_shape=jax.ShapeDtypeStruct((n, m), F32),
        compiler_params=pltpu.CompilerParams(
            dimension_semantics=("arbitrary",), vmem_limit_bytes=VMEM_LIMIT),
        name="linear2_residual",
    )(xa, xb, wb16[:ka], wb16[ka:], res)


def _t5_bucket(rel):
    nb = REL_BUCKETS // 2
    ret = jnp.where(rel > 0, nb, 0)
    n = jnp.abs(rel)
    max_exact = nb // 2
    nf = jnp.maximum(n, 1).astype(F32)
    large = max_exact + (jnp.log(nf / max_exact) / math.log(REL_MAX_DIST / max_exact)
                         * (nb - max_exact)).astype(jnp.int32)
    large = jnp.minimum(large, nb - 1)
    return ret + jnp.where(n < max_exact, n, large)


def _even_mixer(x, shift_prev, wkv_prev, k_past, v_past, kidx_past,
                g_mix, w_in, mu, w0, w_du, a0, w_au, w_gu, k_k, k_a, r_k, gn_w, gn_b,
                qn_g, kn_g, rel_bias, w_out):
    Bx, T, _ = x.shape
    n = Bx * T
    p = norm_linear(x.reshape(n, D_MODEL), g_mix, w_in, keep_padding=True).reshape(Bx, T, -1)
    r, log_decay, k, v, kk, a, gate, qpad, kb = even_prep(
        p, shift_prev, mu, w0, w_du, a0, w_au, w_gu, k_k, k_a, qn_g, kn_g)
    o, wkv_new = wkv_chunked(r, log_decay, k, v, kk, a, wkv_prev)
    rows = lambda t: t.reshape(n, D_A)
    o_a = rwkv_post(rows(o), rows(r), rows(k), rows(v), rows(gate), r_k, gn_w, gn_b)
    c = A_PROJ + 2 * D_B
    vb = p[..., c:c + D_B]
    qi = p[..., c + D_B:c + D_B + H_I * D_IDX]
    ki = p[..., c + D_B + H_I * D_IDX:c + D_B + H_I * D_IDX + D_IDX]
    wi = p[..., c + D_B + H_I * D_IDX + D_IDX:A_PROJ + B_PROJ]
    offset = k_past.shape[1]
    k_all = jnp.concatenate([k_past.reshape(Bx, offset, D_B).astype(F32), kb], axis=1)
    v_all = jnp.concatenate([v_past.reshape(Bx, offset, D_B).astype(F32), vb], axis=1)
    ki_all = jnp.concatenate([kidx_past.astype(F32), ki], axis=1)
    o_b = dsa_attention(qpad, k_all, v_all, qi, wi, ki_all, rel_bias, offset)
    x_new = linear2_residual(o_a, o_b.reshape(n, D_B), w_out, x.reshape(n, D_MODEL)).reshape(Bx, T, D_MODEL)
    heads = lambda t: t.reshape(Bx, T, H_B, DH_B)
    return x_new, p[:, -1, :A_PROJ], wkv_new, heads(kb), heads(vb), ki


def _conv_mixer(x, conv_prev, g_mix, w1, b1, w_dw, b_dw, ln_g, ln_b, w2, b2):
    Bx, T, _ = x.shape
    u = norm_linear(x.reshape(Bx * T, D_MODEL), g_mix, w1, b1).reshape(Bx, T, -1)
    yn, conv_new = conv_module(u, conv_prev, w_dw, b_dw, ln_g, ln_b)
    x_new = linear_residual(yn.reshape(Bx * T, D_CONV), w2, b2,
                            x.reshape(Bx * T, D_MODEL)).reshape(Bx, T, D_MODEL)
    return x_new, conv_new


def _peer_both(xp, xs, g, wq, subkeys, u_tab, v_tab):
    np_ = xp.shape[0] * xp.shape[1]
    rows = jnp.concatenate([xp.reshape(np_, D_MODEL), xs.reshape(-1, D_MODEL)], axis=0)
    out = peer(rows, g, wq, subkeys, u_tab, v_tab)
    return out[:np_].reshape(xp.shape), out[np_:].reshape(xs.shape)


def kernel(x_prompt, x_sample, state_wkv, state_shift, cache_k, cache_v, cache_kidx, state_conv,
           norm_mix, norm_ffn, w_in, mu_shift, w0, w_decay_up, a0, w_iclr_up, w_gate_up,
           k_k, k_a, r_k, gn_w, gn_b, qn_g, kn_g, rel_bias, w_out,
           conv_w1, conv_b1, conv_dw, conv_bdw, conv_ln_g, conv_ln_b, conv_w2, conv_b2,
           peer_wq, peer_subkeys, peer_u, peer_v):
    xp, xs = x_prompt, x_sample
    Bp = xp.shape[0]
    dt = xp.dtype
    wkv_p, shift_p, k_p, v_p, kidx_p, conv_p = [], [], [], [], [], []
    wkv_s, shift_s, k_s, v_s, kidx_s, conv_s = [], [], [], [], [], []
    for li in range(DEPTH):
        if li % 2 == 0:
            e = li // 2
            prm = (norm_mix[li], w_in[e], mu_shift[e], w0[e], w_decay_up[e], a0[e], w_iclr_up[e],
                   w_gate_up[e], k_k[e], k_a[e], r_k[e], gn_w[e], gn_b[e], qn_g[e], kn_g[e],
                   rel_bias, w_out[e])
            xp, sh, wk, kb, vb, ki = _even_mixer(
                xp, jnp.zeros((Bp, A_PROJ), dt), jnp.zeros((Bp, H_A, DH_A, DH_A), dt),
                jnp.zeros((Bp, 0, H_B, DH_B), dt), jnp.zeros((Bp, 0, H_B, DH_B), dt),
                jnp.zeros((Bp, 0, D_IDX), dt), *prm)
            wkv_p.append(wk); shift_p.append(sh); k_p.append(kb); v_p.append(vb); kidx_p.append(ki)
            xs, sh, wk, kb, vb, ki = _even_mixer(
                xs, state_shift[e], state_wkv[e], cache_k[e], cache_v[e], cache_kidx[e], *prm)
            wkv_s.append(wk); shift_s.append(sh); k_s.append(kb); v_s.append(vb); kidx_s.append(ki)
        else:
            o = li // 2
            prm = (norm_mix[li], conv_w1[o], conv_b1[o], conv_dw[o], conv_bdw[o],
                   conv_ln_g[o], conv_ln_b[o], conv_w2[o], conv_b2[o])
            xp, cp = _conv_mixer(xp, jnp.zeros((Bp, CONV_W - 1, D_CONV), dt), *prm)
            xs, cs = _conv_mixer(xs, state_conv[o], *prm)
            conv_p.append(cp); conv_s.append(cs)
        pprm = (norm_ffn[li], peer_wq[li], peer_subkeys[li], peer_u[li], peer_v[li])
        xp, xs = _peer_both(xp, xs, *pprm)
    return (xp, xs,
            jnp.stack(wkv_p), jnp.stack(shift_p), jnp.stack(k_p), jnp.stack(v_p), jnp.stack(kidx_p), jnp.stack(conv_p),
            jnp.stack(wkv_s), jnp.stack(shift_s), jnp.stack(k_s), jnp.stack(v_s), jnp.stack(kidx_s), jnp.stack(conv_s))
```

```python
import functools
import math

import jax
import jax.numpy as jnp
from jax import lax
from jax.experimental import pallas as pl
from jax.experimental.pallas import tpu as pltpu

D_MODEL = 1024
DEPTH = 4
CHUNK = 64
NORM_EPS = 1e-6
D_A = D_MODEL // 2
DH_A = 64
H_A = D_A // DH_A
LORA_W = 64
LORA_A = 64
LORA_G = 128
A_PROJ = 3 * D_A + LORA_W + LORA_A + LORA_G
GN_EPS = 64e-5
D_B = D_MODEL // 2
DH_B = 64
H_B = D_B // DH_B
H_I = 4
D_IDX = 64
TOPK_MAX = 256
Q_BLOCK = 128
B_PROJ = 3 * D_B + H_I * D_IDX + D_IDX + H_I
REL_BUCKETS = 32
REL_MAX_DIST = 128
D_CONV = D_MODEL
CONV_W = 31
PEER_HEADS = 8
PEER_DK = 256
N_KEYS = 128
TOPK_HALF = 16
PEER_TOPK = 16
PEER_BLOCK = 512

LANES = 128
VMEM_LIMIT = 48 * 1024 * 1024
PEER_TABLE_VMEM_LIMIT = 56 * 1024 * 1024

F32 = jnp.float32
BF16 = jnp.bfloat16


def _round_up(n, m):
    return -(-n // m) * m


def _col_tile(m):
    for t in (1024, 896, 768, 640, 512, 384, 256, 128):
        if m % t == 0:
            return t
    raise ValueError(m)


def _norm_linear_kernel(x_ref, g_ref, w_ref, b_ref, o_ref, h_scr):
    @pl.when(pl.program_id(1) == 0)
    def _():
        x = x_ref[...]
        ms = jnp.mean(x * x, axis=-1, keepdims=True)
        h_scr[...] = (x * lax.rsqrt(ms + NORM_EPS) * g_ref[...]).astype(BF16)

    o_ref[...] = jnp.dot(h_scr[...], w_ref[...], preferred_element_type=F32) + b_ref[...]


def norm_linear(x, g, w, b=None, row_tile=512, keep_padding=False):
    n, d = x.shape
    m = w.shape[1]
    mp = _round_up(m, 2 * LANES)
    wb = jnp.pad(w.astype(BF16), ((0, 0), (0, mp - m)))
    bb = jnp.zeros((1, mp), F32) if b is None else jnp.pad(b.astype(F32), (0, mp - m))[None]
    tn = _col_tile(mp)
    tm = min(row_tile, n)
    assert n % tm == 0
    out = pl.pallas_call(
        _norm_linear_kernel,
        grid=(n // tm, mp // tn),
        in_specs=[
            pl.BlockSpec((tm, d), lambda i, j: (i, 0)),
            pl.BlockSpec((1, d), lambda i, j: (0, 0)),
            pl.BlockSpec((d, tn), lambda i, j: (0, j)),
            pl.BlockSpec((1, tn), lambda i, j: (0, j)),
        ],
        out_specs=pl.BlockSpec((tm, tn), lambda i, j: (i, j)),
        out_shape=jax.ShapeDtypeStruct((n, mp), F32),
        scratch_shapes=[pltpu.VMEM((tm, d), BF16)],
        compiler_params=pltpu.CompilerParams(
            dimension_semantics=("arbitrary", "arbitrary"), vmem_limit_bytes=VMEM_LIMIT),
    )(x, g[None].astype(F32), wb, bb)
    return out if (keep_padding or mp == m) else out[:, :m]


def _linear_residual_kernel(x_ref, w_ref, b_ref, r_ref, o_ref):
    o_ref[...] = (r_ref[...] + b_ref[...]
                  + jnp.dot(x_ref[...].astype(BF16), w_ref[...], preferred_element_type=F32))


def linear_residual(x, w, b, res, row_tile=512):
    n, k = x.shape
    m = w.shape[1]
    tm = min(row_tile, n)
    assert n % tm == 0 and m % LANES == 0
    bb = jnp.zeros((1, m), F32) if b is None else b.astype(F32)[None]
    return pl.pallas_call(
        _linear_residual_kernel,
        grid=(n // tm,),
        in_specs=[
            pl.BlockSpec((tm, k), lambda i: (i, 0)),
            pl.BlockSpec((k, m), lambda i: (0, 0)),
            pl.BlockSpec((1, m), lambda i: (0, 0)),
            pl.BlockSpec((tm, m), lambda i: (i, 0)),
        ],
        out_specs=pl.BlockSpec((tm, m), lambda i: (i, 0)),
        out_shape=jax.ShapeDtypeStruct((n, m), F32),
        compiler_params=pltpu.CompilerParams(
            dimension_semantics=("arbitrary",), vmem_limit_bytes=VMEM_LIMIT),
    )(x, w.astype(BF16), bb, res)


PEER_TB = 128
PEER_E = PEER_HEADS * PEER_TOPK
N_EXPERTS = N_KEYS * N_KEYS
ROW_WORDS = D_MODEL // 2
ROW_SUB = ROW_WORDS // LANES
PSTRIDE = PEER_E + 8
PEER_DOWN_CHUNK = 32
PEER_CAND_ROWS = 16 + 7 * 8 + 8
NEG_INF = float("-inf")


def _top_rows(s, k):
    n = s.shape[0]
    rows = lax.broadcasted_iota(jnp.int32, s.shape, 0).astype(F32)
    out_rows = lax.broadcasted_iota(jnp.int32, (k, s.shape[1]), 0)
    vals = jnp.zeros((k, s.shape[1]), F32)
    ids = jnp.zeros((k, s.shape[1]), F32)
    for it in range(k):
        m = jnp.max(s, axis=0, keepdims=True)
        first = jnp.min(jnp.where(s == m, rows, float(n)), axis=0, keepdims=True)
        vals = jnp.where(out_rows == it, m, vals)
        ids = jnp.where(out_rows == it, first, ids)
        s = jnp.where(rows == first, NEG_INF, s)
    return vals, ids


def _peer_select_kernel(x_ref, g_ref, wq_ref, sk_ref, h_ref, idx_ref, gate_ref):
    x = x_ref[...]
    ms = jnp.mean(x * x, axis=-1, keepdims=True)
    h = x * lax.rsqrt(ms + NORM_EPS) * g_ref[...]
    h_ref[...] = h
    q = jnp.dot(h.astype(BF16), wq_ref[...], preferred_element_type=F32).astype(BF16)
    half = PEER_DK // 2
    tb = x.shape[0]
    assert TOPK_HALF == 16 and PEER_TOPK == 16
    crow = lax.broadcasted_iota(jnp.int32, (PEER_CAND_ROWS, tb), 0).astype(F32)
    sub8 = lax.broadcasted_iota(jnp.int32, (8, tb), 0)
    orow = lax.broadcasted_iota(jnp.int32, (PEER_TOPK, tb), 0)
    ids = []
    for hd in range(PEER_HEADS):
        sv, si = [], []
        for p in range(2):
            c = (hd * 2 + p) * half
            s = lax.dot_general(sk_ref[hd * 2 + p], q[:, c:c + half],
                                (((1,), (1,)), ((), ())), preferred_element_type=F32)
            v, i = _top_rows(s, TOPK_HALF)
            sv.append(v)
            si.append(i)
        cand = [sv[0][0:1] + sv[1]]
        eid = [si[0][0:1] * float(N_KEYS) + si[1]]
        for i in range(1, 8):
            keep = sub8 < PEER_TOPK // (i + 1)
            cand.append(jnp.where(keep, sv[0][i:i + 1] + sv[1][0:8], NEG_INF))
            eid.append(si[0][i:i + 1] * float(N_KEYS) + si[1][0:8])
        cand.append(sv[0][8:16] + sv[1][0:1])
        eid.append(si[0][8:16] * float(N_KEYS) + si[1][0:1])
        cand = jnp.concatenate(cand, axis=0)
        eid = jnp.concatenate(eid, axis=0)
        cs = jnp.zeros((PEER_TOPK, tb), F32)
        ce = jnp.zeros((PEER_TOPK, tb), F32)
        for it in range(PEER_TOPK):
            m = jnp.max(cand, axis=0, keepdims=True)
            first = jnp.min(jnp.where(cand == m, crow, float(crow.shape[0])), axis=0, keepdims=True)
            hit = crow == first
            e = jnp.max(jnp.where(hit, eid, -1.0), axis=0, keepdims=True)
            cs = jnp.where(orow == it, m, cs)
            ce = jnp.where(orow == it, e, ce)
            cand = jnp.where(hit, NEG_INF, cand)
        ex = jnp.exp(cs - cs[0:1])
        gates = ex / jnp.sum(ex, axis=0, keepdims=True)
        ids.append(ce * float(ROW_SUB))
        gate_ref[0, hd * PEER_TOPK:(hd + 1) * PEER_TOPK, :] = gates
    idx_ref[0] = jnp.concatenate(ids, axis=0).T.astype(jnp.int32)


def _table_row(tab, first_row):
    return tab[pl.ds(pl.multiple_of(first_row, ROW_SUB), ROW_SUB), :]


def _unpack_row(row):
    lo = pltpu.bitcast(lax.shift_left(row, 16), F32)
    hi = pltpu.bitcast(jnp.bitwise_and(row, jnp.int32(-65536)), F32)
    return lo, hi


def _load_block_scalars(src_hbm, dst_smem, sem):
    cp = pltpu.make_async_copy(src_hbm.at[pl.program_id(0)], dst_smem, sem)
    cp.start()
    cp.wait()


def _load_table_once(tab_hbm, tab_vmem, sem):
    @pl.when(pl.program_id(0) == 0)
    def _():
        cp = pltpu.make_async_copy(tab_hbm, tab_vmem, sem)
        cp.start()
        cp.wait()


def _peer_up_kernel(idx_hbm, tab_hbm, h_ref, gate_ref, w_ref, tab, idx_s, pbuf_a, pbuf_b, act_s, sems):
    _load_table_once(tab_hbm, tab, sems.at[0])
    _load_block_scalars(idx_hbm, idx_s, sems.at[1])
    lane = lax.broadcasted_iota(jnp.int32, (PEER_E, PEER_TB), 1)
    act_s[...] = jnp.zeros((PEER_E, PEER_TB), F32)
    pbuf_b[...] = jnp.zeros(pbuf_b.shape, F32)

    def gather(t, pbuf):
        ht = h_ref[t]
        h_lo, h_hi = ht[0:ROW_SUB], ht[ROW_SUB:2 * ROW_SUB]
        base = t * PEER_E
        for e in range(PEER_E):
            lo, hi = _unpack_row(_table_row(tab, idx_s[base + e]))
            pbuf[pl.ds(e, ROW_SUB, stride=PSTRIDE), :] = lo * h_lo + hi * h_hi

    def reduce(t, pbuf):
        acc = pbuf[0:PEER_E, :]
        for c in range(1, ROW_SUB):
            acc = acc + pbuf[c * PSTRIDE:c * PSTRIDE + PEER_E, :]
        col = jnp.sum(acc, axis=-1, keepdims=True)
        act_s[...] = jnp.where(lane == t, col, act_s[...])

    def token_pair(j, carry):
        reduce(2 * j - 1, pbuf_b)
        gather(2 * j, pbuf_a)
        reduce(2 * j, pbuf_a)
        gather(2 * j + 1, pbuf_b)
        return carry

    lax.fori_loop(0, PEER_TB // 2, token_pair, 0)
    reduce(PEER_TB - 1, pbuf_b)
    a = act_s[...]
    w_ref[0] = (gate_ref[0] * (0.5 * a * (1.0 + lax.erf(a * (2.0 ** -0.5))))).T


def _peer_down_kernel(idx_hbm, w_hbm, tab_hbm, x_ref, o_ref, tab, idx_s, w_s, sems):
    _load_table_once(tab_hbm, tab, sems.at[0])
    _load_block_scalars(idx_hbm, idx_s, sems.at[1])
    _load_block_scalars(w_hbm, w_s, sems.at[2])
    n_acc = 4
    zero = jnp.zeros((ROW_SUB, LANES), F32)

    def token(t, carry):
        def chunk(c, accs):
            acc_lo, acc_hi = list(accs[:n_acc]), list(accs[n_acc:])
            base = t * PEER_E + c * PEER_DOWN_CHUNK
            for j in range(PEER_DOWN_CHUNK):
                lo, hi = _unpack_row(_table_row(tab, idx_s[base + j]))
                w = w_s[base + j]
                acc_lo[j % n_acc] = acc_lo[j % n_acc] + w * lo
                acc_hi[j % n_acc] = acc_hi[j % n_acc] + w * hi
            return tuple(acc_lo) + tuple(acc_hi)

        accs = lax.fori_loop(0, PEER_E // PEER_DOWN_CHUNK, chunk, (zero,) * (2 * n_acc))
        lo = (accs[0] + accs[1]) + (accs[2] + accs[3])
        hi = (accs[4] + accs[5]) + (accs[6] + accs[7])
        o_ref[t] = x_ref[t] + jnp.concatenate([lo, hi], axis=0)
        return carry

    lax.fori_loop(0, PEER_TB, token, 0)


def _pack_table(tab):
    n = tab.shape[0]
    bits = lax.bitcast_convert_type(tab.astype(jnp.bfloat16), jnp.uint16).astype(jnp.uint32)
    words = bits[:, :ROW_WORDS] | (bits[:, ROW_WORDS:] << 16)
    return lax.bitcast_convert_type(words, jnp.int32).reshape(n * ROW_SUB, LANES)


def peer(x, g, wq, subkeys, u_tab, v_tab):
    n, d = x.shape
    assert n % PEER_TB == 0 and d == D_MODEL
    nb = n // PEER_TB
    sk = subkeys.reshape(PEER_HEADS * 2, N_KEYS, PEER_DK // 2).astype(BF16)
    params = pltpu.CompilerParams(dimension_semantics=("arbitrary",), vmem_limit_bytes=VMEM_LIMIT)
    h, idx, gates = pl.pallas_call(
        _peer_select_kernel,
        grid=(nb,),
        in_specs=[
            pl.BlockSpec((PEER_TB, d), lambda i: (i, 0)),
            pl.BlockSpec((1, d), lambda i: (0, 0)),
            pl.BlockSpec((d, PEER_HEADS * PEER_DK), lambda i: (0, 0)),
            pl.BlockSpec((PEER_HEADS * 2, N_KEYS, PEER_DK // 2), lambda i: (0, 0, 0)),
        ],
        out_specs=[
            pl.BlockSpec((PEER_TB, d), lambda i: (i, 0)),
            pl.BlockSpec((1, PEER_TB, PEER_E), lambda i: (i, 0, 0)),
            pl.BlockSpec((1, PEER_E, PEER_TB), lambda i: (i, 0, 0)),
        ],
        out_shape=[
            jax.ShapeDtypeStruct((n, d), F32),
            jax.ShapeDtypeStruct((nb, PEER_TB, PEER_E), jnp.int32),
            jax.ShapeDtypeStruct((nb, PEER_E, PEER_TB), F32),
        ],
        compiler_params=params,
        name="peer_select",
    )(x, g[None].astype(F32), wq.astype(BF16), sk)

    table_params = pltpu.CompilerParams(dimension_semantics=("arbitrary",),
                                        vmem_limit_bytes=PEER_TABLE_VMEM_LIMIT)
    tok_tiles = (PEER_TB, d // LANES, LANES)
    w = pl.pallas_call(
        _peer_up_kernel,
        grid=(nb,),
        in_specs=[
            pl.BlockSpec(memory_space=pl.ANY),
            pl.BlockSpec(memory_space=pl.ANY),
            pl.BlockSpec(tok_tiles, lambda i: (i, 0, 0)),
            pl.BlockSpec((1, PEER_E, PEER_TB), lambda i: (i, 0, 0)),
        ],
        out_specs=pl.BlockSpec((1, PEER_TB, PEER_E), lambda i: (i, 0, 0)),
        out_shape=jax.ShapeDtypeStruct((nb, PEER_TB, PEER_E), F32),
        scratch_shapes=[
            pltpu.VMEM((N_EXPERTS * ROW_SUB, LANES), jnp.int32),
            pltpu.SMEM((PEER_TB * PEER_E,), jnp.int32),
            pltpu.VMEM((ROW_SUB * PSTRIDE, LANES), F32),
            pltpu.VMEM((ROW_SUB * PSTRIDE, LANES), F32),
            pltpu.VMEM((PEER_E, PEER_TB), F32),
            pltpu.SemaphoreType.DMA((2,)),
        ],
        compiler_params=table_params,
        name="peer_up",
    )(idx.reshape(nb, PEER_TB * PEER_E), _pack_table(u_tab), h.reshape(n, d // LANES, LANES), gates)

    out = pl.pallas_call(
        _peer_down_kernel,
        grid=(nb,),
        in_specs=[
            pl.BlockSpec(memory_space=pl.ANY),
            pl.BlockSpec(memory_space=pl.ANY),
            pl.BlockSpec(memory_space=pl.ANY),
            pl.BlockSpec(tok_tiles, lambda i: (i, 0, 0)),
        ],
        out_specs=pl.BlockSpec(tok_tiles, lambda i: (i, 0, 0)),
        out_shape=jax.ShapeDtypeStruct((n, d // LANES, LANES), F32),
        scratch_shapes=[
            pltpu.VMEM((N_EXPERTS * ROW_SUB, LANES), jnp.int32),
            pltpu.SMEM((PEER_TB * PEER_E,), jnp.int32),
            pltpu.SMEM((PEER_TB * PEER_E,), F32),
            pltpu.SemaphoreType.DMA((3,)),
        ],
        compiler_params=table_params,
        name="peer_down",
    )(idx.reshape(nb, PEER_TB * PEER_E), w.reshape(nb, PEER_TB * PEER_E), _pack_table(v_tab),
      x.reshape(n, d // LANES, LANES))
    return out.reshape(n, d)


DSA_KT = 256
DSA_NEAR = 3
DSA_POS_BITS = 14
DSA_COUNT_TILES = 4
CHUNK_SHIFT = CHUNK.bit_length() - 1
INT_MIN = -2 ** 31
KEY_NEG_INF = 0x807FFFFF - 2 ** 32
_NT = (((1,), (1,)), ((), ()))


def _sort_key(x):
    b = pltpu.bitcast(x, jnp.int32)
    return b ^ ((b >> 31) & 0x7FFFFFFF)


def _dsa_kernel(qpad_ref, qi_ref, wit_ref, k_ref, vt_ref, ki_ref, nb_ref, o_ref,
                keys_scr, s_scr, p_scr, *acc_refs, offset, n_keys, topk, qb):
    kt_ = DSA_KT
    q0 = offset + pl.program_id(1) * qb
    qpos = q0 + lax.broadcasted_iota(jnp.int32, (1, qb), 1)
    vis_end = jnp.minimum((lax.shift_right_logical(qpos, CHUNK_SHIFT) + 1) * CHUNK, n_keys)
    blk_end = jnp.minimum(((q0 + qb - 1) // CHUNK + 1) * CHUNK, n_keys)
    n_tiles = (blk_end + kt_ - 1) // kt_
    row = lax.broadcasted_iota(jnp.int32, (kt_, qb), 0)

    qi = qi_ref[0]
    qis = [qi[:, h * D_IDX:(h + 1) * D_IDX] for h in range(H_I)]
    wit = wit_ref[0] * (H_I ** -0.5)

    def score_tile(t, c):
        k0 = pl.multiple_of(t * kt_, kt_)
        kit = ki_ref[0, pl.ds(k0, kt_), :]
        scores = [lax.dot_general(kit, qis[h], _NT, preferred_element_type=F32) for h in range(H_I)]
        idx = jnp.zeros((kt_, qb), F32)
        for h in range(H_I):
            idx = idx + jnp.maximum(scores[h] * (D_IDX ** -0.5), 0.0) * wit[h:h + 1]
        idx = jnp.where(idx == 0.0, 0.0, idx)
        keys_scr[pl.ds(k0, kt_), :] = jnp.where(row + k0 < vis_end, _sort_key(idx), KEY_NEG_INF)
        return c

    lax.fori_loop(0, n_tiles, score_tile, 0)

    n_steps = (n_tiles + DSA_COUNT_TILES - 1) // DSA_COUNT_TILES

    def pad_tile(t, c):
        keys_scr[pl.ds(pl.multiple_of(t * kt_, kt_), kt_), :] = jnp.full((kt_, qb), KEY_NEG_INF, jnp.int32)
        return c

    lax.fori_loop(n_tiles, n_steps * DSA_COUNT_TILES, pad_tile, 0)

    def count(preds, n_out):
        def body(s, accs):
            accs = list(accs)
            for u in range(DSA_COUNT_TILES):
                k0 = pl.multiple_of((s * DSA_COUNT_TILES + u) * kt_, kt_)
                hits = preds(keys_scr[pl.ds(k0, kt_), :], row + k0)
                for i in range(n_out):
                    accs[i] = accs[i] + jnp.sum(hits[i].reshape(kt_ // 8, 8, qb), axis=0)
            return tuple(accs)
        accs = lax.fori_loop(0, n_steps, body, (jnp.zeros((8, qb), F32),) * n_out)
        return [jnp.sum(acc, axis=0, keepdims=True) for acc in accs]

    one = lambda hit: jnp.where(hit, 1.0, 0.0)
    bit = lambda n: lax.shift_left(jnp.int32(1), n)
    kf = float(topk)
    c0, = count(lambda kt, kp: (one(kt >= 0),), 1)
    thr0 = jnp.where(c0 >= kf, 0, INT_MIN).astype(jnp.int32)

    def thr_bits(j, thr):
        hi, lo = bit(30 - 2 * j), bit(29 - 2 * j)
        c_hl, c_h, c_l = count(lambda kt, kp: (one(kt >= (thr | hi | lo)), one(kt >= (thr | hi)),
                                                one(kt >= (thr | lo))), 3)
        take_hi = c_h >= kf
        take_lo = jnp.where(take_hi, c_hl, c_l) >= kf
        return thr | jnp.where(take_hi, hi, 0) | jnp.where(take_lo, lo, 0)

    thr = lax.fori_loop(0, 15, thr_bits, thr0)
    c_last, = count(lambda kt, kp: (one(kt >= (thr | 1)),), 1)
    thr = jnp.where(c_last >= kf, thr | 1, thr)
    c_gt, c_eq = count(lambda kt, kp: (one(kt > thr), one(kt == thr)), 2)
    need = kf - c_gt
    select_all = vis_end <= topk

    def tie_cut():
        def pos_bits(j, lo_pos):
            hi, lo = bit(DSA_POS_BITS - 1 - 2 * j), bit(DSA_POS_BITS - 2 - 2 * j)
            tied = lambda kt, kp, bound: jnp.where(kt == thr, one(kp < bound), 0.0)
            f_hl, f_h, f_l = count(lambda kt, kp: (tied(kt, kp, lo_pos + hi + lo), tied(kt, kp, lo_pos + hi),
                                                    tied(kt, kp, lo_pos + lo)), 3)
            take_hi = f_h < need
            take_lo = jnp.where(take_hi, f_hl, f_l) < need
            return lo_pos + jnp.where(take_hi, hi, 0) + jnp.where(take_lo, lo, 0)
        return lax.fori_loop(0, DSA_POS_BITS // 2, pos_bits, jnp.zeros((1, qb), jnp.int32)) + 1

    surplus = jnp.max(jnp.where((c_eq > need) & jnp.logical_not(select_all), 1.0, 0.0)) > 0.0
    cut = lax.cond(surplus, tie_cut, lambda: jnp.full((1, qb), 2 ** DSA_POS_BITS, jnp.int32))
    thr = jnp.where(select_all, KEY_NEG_INF, thr)
    cut = jnp.where(select_all, 0, cut)

    for acc in acc_refs:
        acc[...] = jnp.zeros(acc.shape, F32)

    def attend_tile(t, carry):
        m_all, l_all = carry
        k0 = pl.multiple_of(t * kt_, kt_)
        keys = keys_scr[pl.ds(k0, kt_), :]
        sel = (keys - jnp.where(row + k0 < cut, 0, 1)) >= thr
        step = (k0 - q0 + (DSA_NEAR - 1) * LANES) // LANES
        nidx = jnp.where(step < 0, DSA_NEAR, step)
        for h in range(H_B):
            kh = k_ref[0, pl.ds(k0, kt_), (h // 2) * LANES:(h // 2 + 1) * LANES]
            qh = qpad_ref[0, :, h * LANES:(h + 1) * LANES]
            s_scr[h] = lax.dot_general(kh, qh, _NT, preferred_element_type=F32)
        m_rows, l_rows, alphas = [], [], []
        for h in range(H_B):
            s = jnp.where(sel, s_scr[h] * (DH_B ** -0.5) + nb_ref[nidx, h], NEG_INF)
            m_old = m_all[h:h + 1]
            m_new = jnp.maximum(m_old, jnp.max(s, axis=0, keepdims=True))
            m_safe = jnp.where(m_new == NEG_INF, 0.0, m_new)
            p = jnp.exp(s - m_safe)
            alpha = jnp.exp(m_old - m_safe)
            p_scr[h] = p.astype(BF16)
            m_rows.append(m_new)
            l_rows.append(alpha * l_all[h:h + 1] + jnp.sum(p, axis=0, keepdims=True))
            alphas.append(alpha)
        for h in range(H_B):
            vth = vt_ref[0, h * DH_B:(h + 1) * DH_B, pl.ds(k0, kt_)]
            acc_refs[h][...] = (alphas[h] * acc_refs[h][...]
                                + jnp.dot(vth, p_scr[h], preferred_element_type=F32))
        return jnp.concatenate(m_rows, axis=0), jnp.concatenate(l_rows, axis=0)

    _, l_all = lax.fori_loop(0, n_tiles, attend_tile,
                             (jnp.full((H_B, qb), NEG_INF, F32), jnp.zeros((H_B, qb), F32)))
    for h in range(H_B):
        o_ref[0, h * DH_B:(h + 1) * DH_B] = acc_refs[h][...] / l_all[h:h + 1]


def _near_bias_tiles(rel_bias, qb):
    k = jnp.arange(DSA_KT, dtype=jnp.int32)[:, None]
    q = jnp.arange(qb, dtype=jnp.int32)[None, :]
    rels = [(j - (DSA_NEAR - 1)) * LANES + k - q for j in range(DSA_NEAR)]
    rels.append(jnp.full((DSA_KT, qb), -(DSA_NEAR * LANES + DSA_KT), jnp.int32))
    tiles = rel_bias[_t5_bucket(jnp.stack(rels))]
    return jnp.moveaxis(tiles, -1, 1).astype(F32)


def dsa_attention(qpad, k_all, v_all, qi, wi, ki_all, rel_bias, offset):
    bx, t, _ = qpad.shape
    n_keys = k_all.shape[1]
    topk = min(TOPK_MAX, n_keys // 4)
    qb = Q_BLOCK if t % Q_BLOCK == 0 else t
    lp = _round_up(n_keys, DSA_KT)
    assert lp <= 2 ** DSA_POS_BITS and offset % DSA_KT == 0 and (qb == Q_BLOCK or t == qb)
    pad = ((0, 0), (0, lp - n_keys), (0, 0))
    kk = jnp.pad(k_all, pad).astype(BF16)
    vt = jnp.pad(v_all, pad).astype(BF16).transpose(0, 2, 1)
    ki = jnp.pad(ki_all, pad).astype(BF16)
    wit = jnp.pad(wi.astype(F32).transpose(0, 2, 1), ((0, 0), (0, 8 - H_I), (0, 0)))
    nb = _near_bias_tiles(rel_bias, qb)
    kern = functools.partial(_dsa_kernel, offset=offset, n_keys=n_keys, topk=topk, qb=qb)
    ot = pl.pallas_call(
        kern,
        grid=(bx, t // qb),
        in_specs=[
            pl.BlockSpec((1, qb, H_B * LANES), lambda b, i: (b, i, 0)),
            pl.BlockSpec((1, qb, H_I * D_IDX), lambda b, i: (b, i, 0)),
            pl.BlockSpec((1, 8, qb), lambda b, i: (b, 0, i)),
            pl.BlockSpec((1, lp, D_B), lambda b, i: (b, 0, 0)),
            pl.BlockSpec((1, D_B, lp), lambda b, i: (b, 0, 0)),
            pl.BlockSpec((1, lp, D_IDX), lambda b, i: (b, 0, 0)),
            pl.BlockSpec((DSA_NEAR + 1, H_B, DSA_KT, qb), lambda b, i: (0, 0, 0, 0)),
        ],
        out_specs=pl.BlockSpec((1, D_B, qb), lambda b, i: (b, 0, i)),
        out_shape=jax.ShapeDtypeStruct((bx, D_B, t), F32),
        scratch_shapes=[pltpu.VMEM((_round_up(n_keys, DSA_KT * DSA_COUNT_TILES), qb), jnp.int32),
                        pltpu.VMEM((H_B, DSA_KT, qb), F32),
                        pltpu.VMEM((H_B, DSA_KT, qb), BF16)]
        + [pltpu.VMEM((DH_B, qb), F32)] * H_B,
        compiler_params=pltpu.CompilerParams(
            dimension_semantics=("arbitrary", "arbitrary"), vmem_limit_bytes=PEER_TABLE_VMEM_LIMIT),
        name="dsa_attention",
    )(qpad, qi.astype(BF16), wit, kk, vt, ki, nb)
    return ot.transpose(0, 2, 1)


WKV_CHUNK = 64
WKV_BLOCK = 256
_TN = (((0,), (0,)), ((), ()))


def _wkv_kernel(r_ref, lw_ref, k_ref, v_ref, kk_ref, a_ref, s0_ref, o_ref, st_ref, s_scr, *, chunk, n_chunks):
    c = chunk

    @pl.when(pl.program_id(1) == 0)
    def _():
        s_scr[...] = s0_ref[0]

    row = lax.broadcasted_iota(jnp.int32, (c, c), 0)
    col = lax.broadcasted_iota(jnp.int32, (c, c), 1)
    strict, incl = row > col, row >= col
    tri = jnp.where(incl, 1.0, 0.0)
    eye = jnp.where(row == col, 1.0, 0.0)
    bdot = lambda x, y: jnp.dot(x.astype(BF16), y.astype(BF16), preferred_element_type=F32)
    bdot_nt = lambda x, y: lax.dot_general(x.astype(BF16), y.astype(BF16), _NT, preferred_element_type=F32)
    bdot_tn = lambda x, y: lax.dot_general(x.astype(BF16), y.astype(BF16), _TN, preferred_element_type=F32)

    def solve_chunk(ci, carry):
        c0 = pl.multiple_of(ci * c, c)
        rows = pl.ds(c0, c)
        heads = range(H_A)
        v, kap_t, r_t, k_b, b_b, decay_all, grams = [], [], [], [], [], [], []
        for h in heads:
            hs = slice(h * DH_A, (h + 1) * DH_A)
            r, lw, k = r_ref[0, rows, hs], lw_ref[0, rows, hs], k_ref[0, rows, hs]
            kap, a = kk_ref[0, rows, hs], a_ref[0, rows, hs]
            cum = jnp.dot(tri, lw, preferred_element_type=F32, precision=lax.Precision.HIGHEST)
            p, p_inv, p_prev = jnp.exp(cum), jnp.exp(-cum), jnp.exp(cum - lw)
            v.append(v_ref[0, rows, hs])
            kap_t.append(kap * p_prev)
            r_t.append(r * p)
            k_b.append(k * p_inv)
            b_b.append(kap * a * p_inv)
            decay_all.append(p[c - 1:c, :])
            grams.append(bdot_nt(jnp.concatenate([kap_t[h], r_t[h]], axis=0),
                                 jnp.concatenate([k_b[h], b_b[h]], axis=0)))
        power = [jnp.where(strict, -g[:c, c:], 0.0) for g in grams]
        inv = [eye + n for n in power]
        for _ in range(c.bit_length() - 2):
            power = [bdot(n, n) for n in power]
            inv = [x + bdot(x, n) for x, n in zip(inv, power)]
        s_prev = [s_scr[h] for h in heads]
        rhs = [bdot_nt(kap_t[h], s_prev[h]) + bdot(jnp.where(strict, grams[h][:c, :c], 0.0), v[h]) for h in heads]
        u = [bdot(inv[h], rhs[h]) for h in heads]
        outs = [bdot_nt(r_t[h], s_prev[h]) + bdot(jnp.where(incl, grams[h][c:, :c], 0.0), v[h])
                - bdot(jnp.where(incl, grams[h][c:, c:], 0.0), u[h]) for h in heads]
        for h in heads:
            s_scr[h] = (s_prev[h] + bdot_tn(v[h], k_b[h]) - bdot_tn(u[h], b_b[h])) * decay_all[h]
        o_ref[0, rows, :] = jnp.concatenate(outs, axis=1)
        return carry

    lax.fori_loop(0, n_chunks, solve_chunk, 0)

    @pl.when(pl.program_id(1) == pl.num_programs(1) - 1)
    def _():
        st_ref[0] = s_scr[...]


def wkv_chunked(r, lw, k, v, kk, a, s0):
    bx, t, _ = r.shape
    chunk = min(WKV_CHUNK, t)
    blk = min(WKV_BLOCK, t)
    assert t % blk == 0 and blk % chunk == 0 and chunk & (chunk - 1) == 0
    tok = pl.BlockSpec((1, blk, D_A), lambda b, i: (b, i, 0))
    state = pl.BlockSpec((1, H_A, DH_A, DH_A), lambda b, i: (b, 0, 0, 0))
    return pl.pallas_call(
        functools.partial(_wkv_kernel, chunk=chunk, n_chunks=blk // chunk),
        grid=(bx, t // blk),
        in_specs=[tok] * 6 + [state],
        out_specs=[tok, state],
        out_shape=[jax.ShapeDtypeStruct((bx, t, D_A), F32),
                   jax.ShapeDtypeStruct((bx, H_A, DH_A, DH_A), F32)],
        scratch_shapes=[pltpu.VMEM((H_A, DH_A, DH_A), F32)],
        compiler_params=pltpu.CompilerParams(
            dimension_semantics=("arbitrary", "arbitrary"), vmem_limit_bytes=VMEM_LIMIT),
        name="wkv_chunked",
    )(r, lw, k, v, kk, a, s0.astype(F32))


CONV_HALO = 32
CONV_BLOCK = 256


def _conv_kernel(u_ref, prev_ref, w_ref, b_ref, g_ref, beta_ref, y_ref, state_ref, pad_scr, *, blk):
    lead = CONV_HALO - (CONV_W - 1)

    @pl.when(pl.program_id(1) == 0)
    def _():
        pad_scr[0:lead, :] = jnp.zeros((lead, D_CONV), F32)
        pad_scr[lead:CONV_HALO, :] = prev_ref[0]

    @pl.when(pl.program_id(1) > 0)
    def _():
        pad_scr[0:CONV_HALO, :] = pad_scr[blk:blk + CONV_HALO, :]

    u = u_ref[0]
    pad_scr[CONV_HALO:CONV_HALO + blk, :] = u[:, :D_CONV] * jax.nn.sigmoid(u[:, D_CONV:])
    y = jnp.zeros((blk, D_CONV), F32) + b_ref[...]
    for j in range(CONV_W):
        y = y + w_ref[j:j + 1, :] * pad_scr[lead + j:lead + j + blk, :]
    m = jnp.mean(y, axis=-1, keepdims=True)
    var = jnp.mean(jnp.square(y - m), axis=-1, keepdims=True)
    z = (y - m) * lax.rsqrt(var + 1e-5) * g_ref[...] + beta_ref[...]
    y_ref[0] = z * jax.nn.sigmoid(z)

    @pl.when(pl.program_id(1) == pl.num_programs(1) - 1)
    def _():
        state_ref[0] = pad_scr[blk + lead:blk + CONV_HALO, :]


def conv_module(u, conv_prev, w_dw, b_dw, ln_g, ln_b):
    bx, t, _ = u.shape
    blk = min(CONV_BLOCK, t)
    assert t % blk == 0 and blk % 8 == 0
    vec = pl.BlockSpec((1, D_CONV), lambda b, i: (0, 0))
    return pl.pallas_call(
        functools.partial(_conv_kernel, blk=blk),
        grid=(bx, t // blk),
        in_specs=[
            pl.BlockSpec((1, blk, 2 * D_CONV), lambda b, i: (b, i, 0)),
            pl.BlockSpec((1, CONV_W - 1, D_CONV), lambda b, i: (b, 0, 0)),
            pl.BlockSpec((CONV_W, D_CONV), lambda b, i: (0, 0)),
            vec, vec, vec,
        ],
        out_specs=[
            pl.BlockSpec((1, blk, D_CONV), lambda b, i: (b, i, 0)),
            pl.BlockSpec((1, CONV_W - 1, D_CONV), lambda b, i: (b, 0, 0)),
        ],
        out_shape=[jax.ShapeDtypeStruct((bx, t, D_CONV), F32),
                   jax.ShapeDtypeStruct((bx, CONV_W - 1, D_CONV), F32)],
        scratch_shapes=[pltpu.VMEM((CONV_HALO + blk, D_CONV), F32)],
        compiler_params=pltpu.CompilerParams(
            dimension_semantics=("arbitrary", "arbitrary"), vmem_limit_bytes=VMEM_LIMIT),
        name="conv_module",
    )(u, conv_prev.astype(F32), w_dw.astype(F32), b_dw[None].astype(F32), ln_g[None].astype(F32),
      ln_b[None].astype(F32))


EVEN_BLOCK = 256
_HI = lax.Precision.HIGHEST


def _head_sum(x, hs):
    return jnp.dot(x, hs, preferred_element_type=F32, precision=_HI)


def _even_prep_kernel(p_ref, shift_ref, mu_ref, w0_ref, a0_ref, wdu_ref, wau_ref, wgu_ref, kk_ref, ka_ref,
                      qn_ref, kn_ref, hs_ref,
                      r_o, lw_o, k_o, v_o, kap_o, a_o, gate_o, qpad_o, kn_o, last_scr, *, blk):
    @pl.when(pl.program_id(1) == 0)
    def _():
        last_scr[0:1, :] = shift_ref[0]

    p = p_ref[0]
    pa = p[:, :A_PROJ]
    first = lax.broadcasted_iota(jnp.int32, pa.shape, 0) == 0
    prev = jnp.where(first, last_scr[0:1, :], pltpu.roll(pa, 1, axis=0))
    last_scr[0:1, :] = pa[blk - 1:blk, :]
    xm = pa + (prev - pa) * mu_ref[...]
    r, k, v = xm[:, :D_A], xm[:, D_A:2 * D_A], xm[:, 2 * D_A:3 * D_A]
    c = 3 * D_A
    dw, da, dg = xm[:, c:c + LORA_W], xm[:, c + LORA_W:c + LORA_W + LORA_A], xm[:, c + LORA_W + LORA_A:A_PROJ]
    bdot = lambda x, w: jnp.dot(x.astype(BF16), w[...], preferred_element_type=F32)
    z = -(w0_ref[...] + bdot(jnp.tanh(dw), wdu_ref))
    w_log = -(jnp.maximum(z, 0.0) + jnp.log(1.0 + jnp.exp(-jnp.abs(z)))) - 0.5
    a = jax.nn.sigmoid(a0_ref[...] + bdot(da, wau_ref))
    kap = k * kk_ref[...]
    hs = hs_ref[...]
    kap = kap / jnp.maximum(jnp.sqrt(_head_sum(kap * kap, hs)), 1e-12)
    r_o[0] = r
    lw_o[0] = -jnp.exp(w_log)
    k_o[0] = k * (1.0 + (a - 1.0) * ka_ref[...])
    v_o[0] = v
    kap_o[0] = kap
    a_o[0] = a
    gate_o[0] = bdot(jax.nn.sigmoid(dg), wgu_ref)

    q = p[:, A_PROJ:A_PROJ + D_B]
    kb = p[:, A_PROJ + D_B:A_PROJ + 2 * D_B]
    qn = q * lax.rsqrt(_head_sum(q * q, hs) * (1.0 / DH_B) + NORM_EPS) * qn_ref[...]
    kn_o[0] = kb * lax.rsqrt(_head_sum(kb * kb, hs) * (1.0 / DH_B) + NORM_EPS) * kn_ref[...]
    upper = lax.broadcasted_iota(jnp.int32, (blk, LANES), 1) >= DH_B
    chunks = []
    for h in range(H_B):
        pair = qn[:, (h // 2) * LANES:(h // 2 + 1) * LANES]
        chunks.append(jnp.where(upper == (h % 2 == 1), pair, 0.0))
    qpad_o[0] = jnp.concatenate(chunks, axis=1).astype(BF16)


def even_prep(p, shift_prev, mu, w0, w_du, a0, w_au, w_gu, k_k, k_a, qn_g, kn_g):
    bx, t, width = p.shape
    blk = min(EVEN_BLOCK, t)
    assert t % blk == 0
    head = jnp.arange(D_A) // DH_A
    hs = (head[:, None] == head[None, :]).astype(F32)
    row = lambda v: v.reshape(1, -1).astype(F32)
    full = lambda a: pl.BlockSpec(a.shape, lambda b, i: (0,) * a.ndim)
    consts = (row(mu), row(w0), row(a0), w_du.astype(BF16), w_au.astype(BF16), w_gu.astype(BF16),
              row(k_k), row(k_a), row(jnp.tile(qn_g, H_B)), row(jnp.tile(kn_g, H_B)), hs)
    tok = pl.BlockSpec((1, blk, D_A), lambda b, i: (b, i, 0))
    return pl.pallas_call(
        functools.partial(_even_prep_kernel, blk=blk),
        grid=(bx, t // blk),
        in_specs=[pl.BlockSpec((1, blk, width), lambda b, i: (b, i, 0)),
                  pl.BlockSpec((1, 1, A_PROJ), lambda b, i: (b, 0, 0))] + [full(c) for c in consts],
        out_specs=[tok] * 7 + [pl.BlockSpec((1, blk, H_B * LANES), lambda b, i: (b, i, 0)), tok],
        out_shape=[jax.ShapeDtypeStruct((bx, t, D_A), F32)] * 7
        + [jax.ShapeDtypeStruct((bx, t, H_B * LANES), BF16), jax.ShapeDtypeStruct((bx, t, D_B), F32)],
        scratch_shapes=[pltpu.VMEM((8, A_PROJ), F32)],
        compiler_params=pltpu.CompilerParams(
            dimension_semantics=("arbitrary", "arbitrary"), vmem_limit_bytes=VMEM_LIMIT),
        name="even_prep",
    )(p, shift_prev[:, None, :].astype(F32), *consts)


def _rwkv_post_kernel(o_ref, r_ref, k_ref, v_ref, gate_ref, rk_ref, gw_ref, gb_ref, hs_ref, out_ref):
    hs = hs_ref[...]
    o = o_ref[...]
    d = o - _head_sum(o, hs) * (1.0 / DH_A)
    var = _head_sum(d * d, hs) * (1.0 / DH_A)
    on = d * lax.rsqrt(var + GN_EPS) * gw_ref[...] + gb_ref[...]
    bonus = _head_sum(r_ref[...] * k_ref[...] * rk_ref[...], hs) * v_ref[...]
    out_ref[...] = (on + bonus) * gate_ref[...]


def rwkv_post(o, r, k, v, gate, r_k, gn_w, gn_b):
    n = o.shape[0]
    blk = min(512, n)
    assert n % blk == 0
    head = jnp.arange(D_A) // DH_A
    hs = (head[:, None] == head[None, :]).astype(F32)
    row = lambda v: v.reshape(1, -1).astype(F32)
    tok = pl.BlockSpec((blk, D_A), lambda i: (i, 0))
    vec = pl.BlockSpec((1, D_A), lambda i: (0, 0))
    return pl.pallas_call(
        _rwkv_post_kernel,
        grid=(n // blk,),
        in_specs=[tok] * 5 + [vec] * 3 + [pl.BlockSpec((D_A, D_A), lambda i: (0, 0))],
        out_specs=tok,
        out_shape=jax.ShapeDtypeStruct((n, D_A), F32),
        compiler_params=pltpu.CompilerParams(
            dimension_semantics=("arbitrary",), vmem_limit_bytes=VMEM_LIMIT),
        name="rwkv_post",
    )(o, r, k, v, gate, row(r_k), row(gn_w), row(gn_b), hs)


def _linear2_residual_kernel(xa_ref, xb_ref, wa_ref, wb_ref, r_ref, o_ref):
    o_ref[...] = (r_ref[...]
                  + jnp.dot(xa_ref[...].astype(BF16), wa_ref[...], preferred_element_type=F32)
                  + jnp.dot(xb_ref[...].astype(BF16), wb_ref[...], preferred_element_type=F32))


def linear2_residual(xa, xb, w, res, row_tile=512):
    n, ka = xa.shape
    kb = xb.shape[1]
    m = w.shape[1]
    tm = min(row_tile, n)
    assert n % tm == 0 and w.shape[0] == ka + kb
    wb16 = w.astype(BF16)
    return pl.pallas_call(
        _linear2_residual_kernel,
        grid=(n // tm,),
        in_specs=[
            pl.BlockSpec((tm, ka), lambda i: (i, 0)),
            pl.BlockSpec((tm, kb), lambda i: (i, 0)),
            pl.BlockSpec((ka, m), lambda i: (0, 0)),
            pl.BlockSpec((kb, m), lambda i: (0, 0)),
            pl.BlockSpec((tm, m), lambda i: (i, 0)),
        ],
        out_specs=pl.BlockSpec((tm, m), lambda i: (i, 0)),
        out_shape=jax.ShapeDtypeStruct((n, m), F32),
        compiler_params=pltpu.CompilerParams(
            dimension_semantics=("arbitrary",), vmem_limit_bytes=VMEM_LIMIT),
        name="linear2_residual",
    )(xa, xb, wb16[:ka], wb16[ka:], res)


def _t5_bucket(rel):
    nb = REL_BUCKETS // 2
    ret = jnp.where(rel > 0, nb, 0)
    n = jnp.abs(rel)
    max_exact = nb // 2
    nf = jnp.maximum(n, 1).astype(F32)
    large = max_exact + (jnp.log(nf / max_exact) / math.log(REL_MAX_DIST / max_exact)
                         * (nb - max_exact)).astype(jnp.int32)
    large = jnp.minimum(large, nb - 1)
    return ret + jnp.where(n < max_exact, n, large)


def _even_mixer(x, shift_prev, wkv_prev, k_past, v_past, kidx_past,
                g_mix, w_in, mu, w0, w_du, a0, w_au, w_gu, k_k, k_a, r_k, gn_w, gn_b,
                qn_g, kn_g, rel_bias, w_out):
    Bx, T, _ = x.shape
    n = Bx * T
    p = norm_linear(x.reshape(n, D_MODEL), g_mix, w_in, keep_padding=True).reshape(Bx, T, -1)
    r, log_decay, k, v, kk, a, gate, qpad, kb = even_prep(
        p, shift_prev, mu, w0, w_du, a0, w_au, w_gu, k_k, k_a, qn_g, kn_g)
    o, wkv_new = wkv_chunked(r, log_decay, k, v, kk, a, wkv_prev)
    rows = lambda t: t.reshape(n, D_A)
    o_a = rwkv_post(rows(o), rows(r), rows(k), rows(v), rows(gate), r_k, gn_w, gn_b)
    c = A_PROJ + 2 * D_B
    vb = p[..., c:c + D_B]
    qi = p[..., c + D_B:c + D_B + H_I * D_IDX]
    ki = p[..., c + D_B + H_I * D_IDX:c + D_B + H_I * D_IDX + D_IDX]
    wi = p[..., c + D_B + H_I * D_IDX + D_IDX:A_PROJ + B_PROJ]
    offset = k_past.shape[1]
    k_all = jnp.concatenate([k_past.reshape(Bx, offset, D_B).astype(F32), kb], axis=1)
    v_all = jnp.concatenate([v_past.reshape(Bx, offset, D_B).astype(F32), vb], axis=1)
    ki_all = jnp.concatenate([kidx_past.astype(F32), ki], axis=1)
    o_b = dsa_attention(qpad, k_all, v_all, qi, wi, ki_all, rel_bias, offset)
    x_new = linear2_residual(o_a, o_b.reshape(n, D_B), w_out, x.reshape(n, D_MODEL)).reshape(Bx, T, D_MODEL)
    heads = lambda t: t.reshape(Bx, T, H_B, DH_B)
    return x_new, p[:, -1, :A_PROJ], wkv_new, heads(kb), heads(vb), ki


def _conv_mixer(x, conv_prev, g_mix, w1, b1, w_dw, b_dw, ln_g, ln_b, w2, b2):
    Bx, T, _ = x.shape
    u = norm_linear(x.reshape(Bx * T, D_MODEL), g_mix, w1, b1).reshape(Bx, T, -1)
    yn, conv_new = conv_module(u, conv_prev, w_dw, b_dw, ln_g, ln_b)
    x_new = linear_residual(yn.reshape(Bx * T, D_CONV), w2, b2,
                            x.reshape(Bx * T, D_MODEL)).reshape(Bx, T, D_MODEL)
    return x_new, conv_new


def _peer_both(xp, xs, g, wq, subkeys, u_tab, v_tab):
    np_ = xp.shape[0] * xp.shape[1]
    rows = jnp.concatenate([xp.reshape(np_, D_MODEL), xs.reshape(-1, D_MODEL)], axis=0)
    out = peer(rows, g, wq, subkeys, u_tab, v_tab)
    return out[:np_].reshape(xp.shape), out[np_:].reshape(xs.shape)


def kernel(x_prompt, x_sample, state_wkv, state_shift, cache_k, cache_v, cache_kidx, state_conv,
           norm_mix, norm_ffn, w_in, mu_shift, w0, w_decay_up, a0, w_iclr_up, w_gate_up,
           k_k, k_a, r_k, gn_w, gn_b, qn_g, kn_g, rel_bias, w_out,
           conv_w1, conv_b1, conv_dw, conv_bdw, conv_ln_g, conv_ln_b, conv_w2, conv_b2,
           peer_wq, peer_subkeys, peer_u, peer_v):
    xp, xs = x_prompt, x_sample
    Bp = xp.shape[0]
    dt = xp.dtype
    wkv_p, shift_p, k_p, v_p, kidx_p, conv_p = [], [], [], [], [], []
    wkv_s, shift_s, k_s, v_s, kidx_s, conv_s = [], [], [], [], [], []
    for li in range(DEPTH):
        if li % 2 == 0:
            e = li // 2
            prm = (norm_mix[li], w_in[e], mu_shift[e], w0[e], w_decay_up[e], a0[e], w_iclr_up[e],
                   w_gate_up[e], k_k[e], k_a[e], r_k[e], gn_w[e], gn_b[e], qn_g[e], kn_g[e],
                   rel_bias, w_out[e])
            xp, sh, wk, kb, vb, ki = _even_mixer(
                xp, jnp.zeros((Bp, A_PROJ), dt), jnp.zeros((Bp, H_A, DH_A, DH_A), dt),
                jnp.zeros((Bp, 0, H_B, DH_B), dt), jnp.zeros((Bp, 0, H_B, DH_B), dt),
                jnp.zeros((Bp, 0, D_IDX), dt), *prm)
            wkv_p.append(wk); shift_p.append(sh); k_p.append(kb); v_p.append(vb); kidx_p.append(ki)
            xs, sh, wk, kb, vb, ki = _even_mixer(
                xs, state_shift[e], state_wkv[e], cache_k[e], cache_v[e], cache_kidx[e], *prm)
            wkv_s.append(wk); shift_s.append(sh); k_s.append(kb); v_s.append(vb); kidx_s.append(ki)
        else:
            o = li // 2
            prm = (norm_mix[li], conv_w1[o], conv_b1[o], conv_dw[o], conv_bdw[o],
                   conv_ln_g[o], conv_ln_b[o], conv_w2[o], conv_b2[o])
            xp, cp = _conv_mixer(xp, jnp.zeros((Bp, CONV_W - 1, D_CONV), dt), *prm)
            xs, cs = _conv_mixer(xs, state_conv[o], *prm)
            conv_p.append(cp); conv_s.append(cs)
        pprm = (norm_ffn[li], peer_wq[li], peer_subkeys[li], peer_u[li], peer_v[li])
        xp, xs = _peer_both(xp, xs, *pprm)
    return (xp, xs,
            jnp.stack(wkv_p), jnp.stack(shift_p), jnp.stack(k_p), jnp.stack(v_p), jnp.stack(kidx_p), jnp.stack(conv_p),
            jnp.stack(wkv_s), jnp.stack(shift_s), jnp.stack(k_s), jnp.stack(v_s), jnp.stack(kidx_s), jnp.stack(conv_s))
```

```python
import functools
import math

import jax
import jax.numpy as jnp
from jax import lax
from jax.experimental import pallas as pl
from jax.experimental.pallas import tpu as pltpu

D_MODEL = 1024
DEPTH = 4
CHUNK = 64
NORM_EPS = 1e-6
D_A = D_MODEL // 2
DH_A = 64
H_A = D_A // DH_A
LORA_W = 64
LORA_A = 64
LORA_G = 128
A_PROJ = 3 * D_A + LORA_W + LORA_A + LORA_G
GN_EPS = 64e-5
D_B = D_MODEL // 2
DH_B = 64
H_B = D_B // DH_B
H_I = 4
D_IDX = 64
TOPK_MAX = 256
Q_BLOCK = 128
B_PROJ = 3 * D_B + H_I * D_IDX + D_IDX + H_I
REL_BUCKETS = 32
REL_MAX_DIST = 128
D_CONV = D_MODEL
CONV_W = 31
PEER_HEADS = 8
PEER_DK = 256
N_KEYS = 128
TOPK_HALF = 16
PEER_TOPK = 16

LANES = 128
VMEM_LIMIT = 48 * 1024 * 1024
PEER_TABLE_VMEM_LIMIT = 56 * 1024 * 1024

F32 = jnp.float32
BF16 = jnp.bfloat16


def _round_up(n, m):
    return -(-n // m) * m


def _col_tile(m):
    for t in (1024, 896, 768, 640, 512, 384, 256, 128):
        if m % t == 0:
            return t
    raise ValueError(m)


def _norm_linear_kernel(x_ref, g_ref, w_ref, b_ref, o_ref, h_scr):
    @pl.when(pl.program_id(1) == 0)
    def _():
        x = x_ref[...]
        ms = jnp.mean(x * x, axis=-1, keepdims=True)
        h_scr[...] = (x * lax.rsqrt(ms + NORM_EPS) * g_ref[...]).astype(BF16)

    o_ref[...] = jnp.dot(h_scr[...], w_ref[...], preferred_element_type=F32) + b_ref[...]


def norm_linear(x, g, w, b=None, row_tile=512, keep_padding=False):
    n, d = x.shape
    m = w.shape[1]
    mp = _round_up(m, 2 * LANES)
    wb = jnp.pad(w.astype(BF16), ((0, 0), (0, mp - m)))
    bb = jnp.zeros((1, mp), F32) if b is None else jnp.pad(b.astype(F32), (0, mp - m))[None]
    tn = _col_tile(mp)
    tm = min(row_tile, n)
    assert n % tm == 0
    out = pl.pallas_call(
        _norm_linear_kernel,
        grid=(n // tm, mp // tn),
        in_specs=[
            pl.BlockSpec((tm, d), lambda i, j: (i, 0)),
            pl.BlockSpec((1, d), lambda i, j: (0, 0)),
            pl.BlockSpec((d, tn), lambda i, j: (0, j)),
            pl.BlockSpec((1, tn), lambda i, j: (0, j)),
        ],
        out_specs=pl.BlockSpec((tm, tn), lambda i, j: (i, j)),
        out_shape=jax.ShapeDtypeStruct((n, mp), F32),
        scratch_shapes=[pltpu.VMEM((tm, d), BF16)],
        compiler_params=pltpu.CompilerParams(
            dimension_semantics=("arbitrary", "arbitrary"), vmem_limit_bytes=VMEM_LIMIT),
        name="norm_linear",
    )(x, g[None].astype(F32), wb, bb)
    return out if (keep_padding or mp == m) else out[:, :m]


def _linear_residual_kernel(x_ref, w_ref, b_ref, r_ref, o_ref):
    o_ref[...] = (r_ref[...] + b_ref[...]
                  + jnp.dot(x_ref[...].astype(BF16), w_ref[...], preferred_element_type=F32))


def linear_residual(x, w, b, res, row_tile=512):
    n, k = x.shape
    m = w.shape[1]
    tm = min(row_tile, n)
    assert n % tm == 0 and m % LANES == 0
    bb = jnp.zeros((1, m), F32) if b is None else b.astype(F32)[None]
    return pl.pallas_call(
        _linear_residual_kernel,
        grid=(n // tm,),
        in_specs=[
            pl.BlockSpec((tm, k), lambda i: (i, 0)),
            pl.BlockSpec((k, m), lambda i: (0, 0)),
            pl.BlockSpec((1, m), lambda i: (0, 0)),
            pl.BlockSpec((tm, m), lambda i: (i, 0)),
        ],
        out_specs=pl.BlockSpec((tm, m), lambda i: (i, 0)),
        out_shape=jax.ShapeDtypeStruct((n, m), F32),
        compiler_params=pltpu.CompilerParams(
            dimension_semantics=("arbitrary",), vmem_limit_bytes=VMEM_LIMIT),
        name="linear_residual",
    )(x, w.astype(BF16), bb, res)


PEER_TB = 128
PEER_E = PEER_HEADS * PEER_TOPK
N_EXPERTS = N_KEYS * N_KEYS
ROW_WORDS = D_MODEL // 2
ROW_SUB = ROW_WORDS // LANES
PSTRIDE = PEER_E + 8
PEER_DOWN_CHUNK = 64
PEER_CAND_ROWS = 16 + 7 * 8 + 8
NEG_INF = float("-inf")


def _top_rows(s, k):
    n = s.shape[0]
    rows = lax.broadcasted_iota(jnp.int32, s.shape, 0).astype(F32)
    out_rows = lax.broadcasted_iota(jnp.int32, (k, s.shape[1]), 0)
    vals = jnp.zeros((k, s.shape[1]), F32)
    ids = jnp.zeros((k, s.shape[1]), F32)
    for it in range(k):
        m = jnp.max(s, axis=0, keepdims=True)
        first = jnp.min(jnp.where(s == m, rows, float(n)), axis=0, keepdims=True)
        vals = jnp.where(out_rows == it, m, vals)
        ids = jnp.where(out_rows == it, first, ids)
        s = jnp.where(rows == first, NEG_INF, s)
    return vals, ids


def _peer_select_kernel(x_ref, g_ref, wq_ref, sk_ref, h_ref, idx_ref, gate_ref):
    x = x_ref[...]
    ms = jnp.mean(x * x, axis=-1, keepdims=True)
    h = x * lax.rsqrt(ms + NORM_EPS) * g_ref[...]
    h_ref[...] = h
    q = jnp.dot(h.astype(BF16), wq_ref[...], preferred_element_type=F32).astype(BF16)
    half = PEER_DK // 2
    tb = x.shape[0]
    assert TOPK_HALF == 16 and PEER_TOPK == 16
    crow = lax.broadcasted_iota(jnp.int32, (PEER_CAND_ROWS, tb), 0).astype(F32)
    sub8 = lax.broadcasted_iota(jnp.int32, (8, tb), 0)
    orow = lax.broadcasted_iota(jnp.int32, (PEER_TOPK, tb), 0)
    ids = []
    for hd in range(PEER_HEADS):
        sv, si = [], []
        for p in range(2):
            c = (hd * 2 + p) * half
            s = lax.dot_general(sk_ref[hd * 2 + p], q[:, c:c + half],
                                (((1,), (1,)), ((), ())), preferred_element_type=F32)
            v, i = _top_rows(s, TOPK_HALF)
            sv.append(v)
            si.append(i)
        cand = [sv[0][0:1] + sv[1]]
        eid = [si[0][0:1] * float(N_KEYS) + si[1]]
        for i in range(1, 8):
            keep = sub8 < PEER_TOPK // (i + 1)
            cand.append(jnp.where(keep, sv[0][i:i + 1] + sv[1][0:8], NEG_INF))
            eid.append(si[0][i:i + 1] * float(N_KEYS) + si[1][0:8])
        cand.append(sv[0][8:16] + sv[1][0:1])
        eid.append(si[0][8:16] * float(N_KEYS) + si[1][0:1])
        cand = jnp.concatenate(cand, axis=0)
        eid = jnp.concatenate(eid, axis=0)
        cs = jnp.zeros((PEER_TOPK, tb), F32)
        ce = jnp.zeros((PEER_TOPK, tb), F32)
        for it in range(PEER_TOPK):
            m = jnp.max(cand, axis=0, keepdims=True)
            first = jnp.min(jnp.where(cand == m, crow, float(crow.shape[0])), axis=0, keepdims=True)
            hit = crow == first
            e = jnp.max(jnp.where(hit, eid, -1.0), axis=0, keepdims=True)
            cs = jnp.where(orow == it, m, cs)
            ce = jnp.where(orow == it, e, ce)
            cand = jnp.where(hit, NEG_INF, cand)
        ex = jnp.exp(cs - cs[0:1])
        gates = ex / jnp.sum(ex, axis=0, keepdims=True)
        ids.append(ce * float(ROW_SUB))
        gate_ref[0, hd * PEER_TOPK:(hd + 1) * PEER_TOPK, :] = gates
    idx_ref[0] = jnp.concatenate(ids, axis=0).T.astype(jnp.int32)


def _table_row(tab, first_row):
    return tab[pl.ds(pl.multiple_of(first_row, ROW_SUB), ROW_SUB), :]


def _unpack_row(row):
    lo = pltpu.bitcast(lax.shift_left(row, 16), F32)
    hi = pltpu.bitcast(jnp.bitwise_and(row, jnp.int32(-65536)), F32)
    return lo, hi


def _load_block_scalars(src_hbm, dst_smem, sem):
    cp = pltpu.make_async_copy(src_hbm.at[pl.program_id(0)], dst_smem, sem)
    cp.start()
    cp.wait()


def _load_table_once(tab_hbm, tab_vmem, sem):
    @pl.when(pl.program_id(0) == 0)
    def _():
        cp = pltpu.make_async_copy(tab_hbm, tab_vmem, sem)
        cp.start()
        cp.wait()


def _peer_up_kernel(idx_hbm, tab_hbm, h_ref, gate_ref, w_ref, tab, idx_s, pbuf_a, pbuf_b, act_s, sems):
    _load_table_once(tab_hbm, tab, sems.at[0])
    _load_block_scalars(idx_hbm, idx_s, sems.at[1])
    lane = lax.broadcasted_iota(jnp.int32, (PEER_E, PEER_TB), 1)
    act_s[...] = jnp.zeros((PEER_E, PEER_TB), F32)
    pbuf_a[...] = jnp.zeros(pbuf_a.shape, F32)
    pbuf_b[...] = jnp.zeros(pbuf_b.shape, F32)

    def gather(t, pbuf):
        ht = h_ref[t]
        h_lo, h_hi = ht[0:ROW_SUB], ht[ROW_SUB:2 * ROW_SUB]
        base = t * PEER_E
        for e in range(PEER_E):
            lo, hi = _unpack_row(_table_row(tab, idx_s[base + e]))
            pbuf[pl.ds(e, ROW_SUB, stride=PSTRIDE), :] = lo * h_lo + hi * h_hi

    def reduce(t, pbuf):
        acc = pbuf[0:PEER_E, :]
        for c in range(1, ROW_SUB):
            acc = acc + pbuf[c * PSTRIDE:c * PSTRIDE + PEER_E, :]
        col = jnp.sum(acc, axis=-1, keepdims=True)
        act_s[...] = jnp.where(lane == t, col, act_s[...])

    def token_pair(j, carry):
        reduce(2 * j - 2, pbuf_a)
        reduce(2 * j - 1, pbuf_b)
        gather(2 * j, pbuf_a)
        gather(2 * j + 1, pbuf_b)
        return carry

    lax.fori_loop(0, PEER_TB // 2, token_pair, 0)
    reduce(PEER_TB - 2, pbuf_a)
    reduce(PEER_TB - 1, pbuf_b)
    a = act_s[...]
    w_ref[0] = (gate_ref[0] * (0.5 * a * (1.0 + lax.erf(a * (2.0 ** -0.5))))).T


def _peer_down_kernel(idx_hbm, w_hbm, tab_hbm, x_ref, o_ref, tab, idx_s, w_s, sems):
    _load_table_once(tab_hbm, tab, sems.at[0])
    _load_block_scalars(idx_hbm, idx_s, sems.at[1])
    _load_block_scalars(w_hbm, w_s, sems.at[2])
    n_acc = 4
    zero = jnp.zeros((ROW_SUB, LANES), F32)

    def token(t, carry):
        def chunk(c, accs):
            acc_lo, acc_hi = list(accs[:n_acc]), list(accs[n_acc:])
            base = t * PEER_E + c * PEER_DOWN_CHUNK
            for j in range(PEER_DOWN_CHUNK):
                lo, hi = _unpack_row(_table_row(tab, idx_s[base + j]))
                w = w_s[base + j]
                acc_lo[j % n_acc] = acc_lo[j % n_acc] + w * lo
                acc_hi[j % n_acc] = acc_hi[j % n_acc] + w * hi
            return tuple(acc_lo) + tuple(acc_hi)

        accs = lax.fori_loop(0, PEER_E // PEER_DOWN_CHUNK, chunk, (zero,) * (2 * n_acc))
        lo = (accs[0] + accs[1]) + (accs[2] + accs[3])
        hi = (accs[4] + accs[5]) + (accs[6] + accs[7])
        o_ref[t] = x_ref[t] + jnp.concatenate([lo, hi], axis=0)
        return carry

    lax.fori_loop(0, PEER_TB, token, 0)


def _pack_table(tab):
    n = tab.shape[0]
    bits = lax.bitcast_convert_type(tab.astype(jnp.bfloat16), jnp.uint16).astype(jnp.uint32)
    words = bits[:, :ROW_WORDS] | (bits[:, ROW_WORDS:] << 16)
    return lax.bitcast_convert_type(words, jnp.int32).reshape(n * ROW_SUB, LANES)


def peer(x, g, wq, subkeys, u_tab, v_tab):
    n, d = x.shape
    assert n % PEER_TB == 0 and d == D_MODEL
    nb = n // PEER_TB
    sk = subkeys.reshape(PEER_HEADS * 2, N_KEYS, PEER_DK // 2).astype(BF16)
    params = pltpu.CompilerParams(dimension_semantics=("arbitrary",), vmem_limit_bytes=VMEM_LIMIT)
    h, idx, gates = pl.pallas_call(
        _peer_select_kernel,
        grid=(nb,),
        in_specs=[
            pl.BlockSpec((PEER_TB, d), lambda i: (i, 0)),
            pl.BlockSpec((1, d), lambda i: (0, 0)),
            pl.BlockSpec((d, PEER_HEADS * PEER_DK), lambda i: (0, 0)),
            pl.BlockSpec((PEER_HEADS * 2, N_KEYS, PEER_DK // 2), lambda i: (0, 0, 0)),
        ],
        out_specs=[
            pl.BlockSpec((PEER_TB, d), lambda i: (i, 0)),
            pl.BlockSpec((1, PEER_TB, PEER_E), lambda i: (i, 0, 0)),
            pl.BlockSpec((1, PEER_E, PEER_TB), lambda i: (i, 0, 0)),
        ],
        out_shape=[
            jax.ShapeDtypeStruct((n, d), F32),
            jax.ShapeDtypeStruct((nb, PEER_TB, PEER_E), jnp.int32),
            jax.ShapeDtypeStruct((nb, PEER_E, PEER_TB), F32),
        ],
        compiler_params=params,
        name="peer_select",
    )(x, g[None].astype(F32), wq.astype(BF16), sk)

    table_params = pltpu.CompilerParams(dimension_semantics=("arbitrary",),
                                        vmem_limit_bytes=PEER_TABLE_VMEM_LIMIT)
    tok_tiles = (PEER_TB, d // LANES, LANES)
    w = pl.pallas_call(
        _peer_up_kernel,
        grid=(nb,),
        in_specs=[
            pl.BlockSpec(memory_space=pl.ANY),
            pl.BlockSpec(memory_space=pl.ANY),
            pl.BlockSpec(tok_tiles, lambda i: (i, 0, 0)),
            pl.BlockSpec((1, PEER_E, PEER_TB), lambda i: (i, 0, 0)),
        ],
        out_specs=pl.BlockSpec((1, PEER_TB, PEER_E), lambda i: (i, 0, 0)),
        out_shape=jax.ShapeDtypeStruct((nb, PEER_TB, PEER_E), F32),
        scratch_shapes=[
            pltpu.VMEM((N_EXPERTS * ROW_SUB, LANES), jnp.int32),
            pltpu.SMEM((PEER_TB * PEER_E,), jnp.int32),
            pltpu.VMEM((ROW_SUB * PSTRIDE, LANES), F32),
            pltpu.VMEM((ROW_SUB * PSTRIDE, LANES), F32),
            pltpu.VMEM((PEER_E, PEER_TB), F32),
            pltpu.SemaphoreType.DMA((2,)),
        ],
        compiler_params=table_params,
        name="peer_up",
    )(idx.reshape(nb, PEER_TB * PEER_E), _pack_table(u_tab), h.reshape(n, d // LANES, LANES), gates)

    out = pl.pallas_call(
        _peer_down_kernel,
        grid=(nb,),
        in_specs=[
            pl.BlockSpec(memory_space=pl.ANY),
            pl.BlockSpec(memory_space=pl.ANY),
            pl.BlockSpec(memory_space=pl.ANY),
            pl.BlockSpec(tok_tiles, lambda i: (i, 0, 0)),
        ],
        out_specs=pl.BlockSpec(tok_tiles, lambda i: (i, 0, 0)),
        out_shape=jax.ShapeDtypeStruct((n, d // LANES, LANES), F32),
        scratch_shapes=[
            pltpu.VMEM((N_EXPERTS * ROW_SUB, LANES), jnp.int32),
            pltpu.SMEM((PEER_TB * PEER_E,), jnp.int32),
            pltpu.SMEM((PEER_TB * PEER_E,), F32),
            pltpu.SemaphoreType.DMA((3,)),
        ],
        compiler_params=table_params,
        name="peer_down",
    )(idx.reshape(nb, PEER_TB * PEER_E), w.reshape(nb, PEER_TB * PEER_E), _pack_table(v_tab),
      x.reshape(n, d // LANES, LANES))
    return out.reshape(n, d)


DSA_KT = 256
DSA_NEAR = 3
DSA_POS_BITS = 14
DSA_COUNT_TILES = 4
CHUNK_SHIFT = CHUNK.bit_length() - 1
INT_MIN = -2 ** 31
KEY_NEG_INF = 0x807FFFFF - 2 ** 32
_NT = (((1,), (1,)), ((), ()))


def _sort_key(x):
    b = pltpu.bitcast(x, jnp.int32)
    return b ^ ((b >> 31) & 0x7FFFFFFF)


def _dsa_kernel(qpad_ref, qi_ref, wit_ref, k_ref, vt_ref, ki_ref, nb_ref, o_ref,
                keys_scr, s_scr, p_scr, *acc_refs, offset, n_keys, topk, qb):
    kt_ = DSA_KT
    q0 = offset + pl.program_id(1) * qb
    qpos = q0 + lax.broadcasted_iota(jnp.int32, (1, qb), 1)
    vis_end = jnp.minimum((lax.shift_right_logical(qpos, CHUNK_SHIFT) + 1) * CHUNK, n_keys)
    blk_end = jnp.minimum(((q0 + qb - 1) // CHUNK + 1) * CHUNK, n_keys)
    n_tiles = (blk_end + kt_ - 1) // kt_
    row = lax.broadcasted_iota(jnp.int32, (kt_, qb), 0)

    qi = qi_ref[0]
    qis = [qi[:, h * D_IDX:(h + 1) * D_IDX] for h in range(H_I)]
    wit = wit_ref[0] * (H_I ** -0.5)

    def score_tile(t, c):
        k0 = pl.multiple_of(t * kt_, kt_)
        kit = ki_ref[0, pl.ds(k0, kt_), :]
        scores = [lax.dot_general(kit, qis[h], _NT, preferred_element_type=F32) for h in range(H_I)]
        idx = jnp.zeros((kt_, qb), F32)
        for h in range(H_I):
            idx = idx + jnp.maximum(scores[h] * (D_IDX ** -0.5), 0.0) * wit[h:h + 1]
        idx = jnp.where(idx == 0.0, 0.0, idx)
        keys_scr[pl.ds(k0, kt_), :] = jnp.where(row + k0 < vis_end, _sort_key(idx), KEY_NEG_INF)
        return c

    lax.fori_loop(0, n_tiles, score_tile, 0)

    n_steps = (n_tiles + DSA_COUNT_TILES - 1) // DSA_COUNT_TILES

    def pad_tile(t, c):
        keys_scr[pl.ds(pl.multiple_of(t * kt_, kt_), kt_), :] = jnp.full((kt_, qb), KEY_NEG_INF, jnp.int32)
        return c

    lax.fori_loop(n_tiles, n_steps * DSA_COUNT_TILES, pad_tile, 0)

    def count(preds, n_out):
        def body(s, accs):
            accs = list(accs)
            for u in range(DSA_COUNT_TILES):
                k0 = pl.multiple_of((s * DSA_COUNT_TILES + u) * kt_, kt_)
                hits = preds(keys_scr[pl.ds(k0, kt_), :], row + k0)
                for i in range(n_out):
                    accs[i] = accs[i] + jnp.sum(hits[i].reshape(kt_ // 8, 8, qb), axis=0)
            return tuple(accs)
        accs = lax.fori_loop(0, n_steps, body, (jnp.zeros((8, qb), F32),) * n_out)
        return [jnp.sum(acc, axis=0, keepdims=True) for acc in accs]

    one = lambda hit: jnp.where(hit, 1.0, 0.0)
    bit = lambda n: lax.shift_left(jnp.int32(1), n)
    kf = float(topk)
    c0, = count(lambda kt, kp: (one(kt >= 0),), 1)
    thr0 = jnp.where(c0 >= kf, 0, INT_MIN).astype(jnp.int32)

    def thr_bits(j, thr):
        hi, lo = bit(30 - 2 * j), bit(29 - 2 * j)
        c_hl, c_h, c_l = count(lambda kt, kp: (one(kt >= (thr | hi | lo)), one(kt >= (thr | hi)),
                                                one(kt >= (thr | lo))), 3)
        take_hi = c_h >= kf
        take_lo = jnp.where(take_hi, c_hl, c_l) >= kf
        return thr | jnp.where(take_hi, hi, 0) | jnp.where(take_lo, lo, 0)

    thr = lax.fori_loop(0, 15, thr_bits, thr0)
    c_last, = count(lambda kt, kp: (one(kt >= (thr | 1)),), 1)
    thr = jnp.where(c_last >= kf, thr | 1, thr)
    c_gt, c_eq = count(lambda kt, kp: (one(kt > thr), one(kt == thr)), 2)
    need = kf - c_gt
    select_all = vis_end <= topk

    def tie_cut():
        def pos_bits(j, lo_pos):
            hi, lo = bit(DSA_POS_BITS - 1 - 2 * j), bit(DSA_POS_BITS - 2 - 2 * j)
            tied = lambda kt, kp, bound: jnp.where(kt == thr, one(kp < bound), 0.0)
            f_hl, f_h, f_l = count(lambda kt, kp: (tied(kt, kp, lo_pos + hi + lo), tied(kt, kp, lo_pos + hi),
                                                    tied(kt, kp, lo_pos + lo)), 3)
            take_hi = f_h < need
            take_lo = jnp.where(take_hi, f_hl, f_l) < need
            return lo_pos + jnp.where(take_hi, hi, 0) + jnp.where(take_lo, lo, 0)
        return lax.fori_loop(0, DSA_POS_BITS // 2, pos_bits, jnp.zeros((1, qb), jnp.int32)) + 1

    surplus = jnp.max(jnp.where((c_eq > need) & jnp.logical_not(select_all), 1.0, 0.0)) > 0.0
    cut = lax.cond(surplus, tie_cut, lambda: jnp.full((1, qb), 2 ** DSA_POS_BITS, jnp.int32))
    thr = jnp.where(select_all, KEY_NEG_INF, thr)
    cut = jnp.where(select_all, 0, cut)

    for acc in acc_refs:
        acc[...] = jnp.zeros(acc.shape, F32)

    def attend_tile(t, carry):
        m_all, l_all = carry
        k0 = pl.multiple_of(t * kt_, kt_)
        keys = keys_scr[pl.ds(k0, kt_), :]
        sel = (keys - jnp.where(row + k0 < cut, 0, 1)) >= thr
        step = (k0 - q0 + (DSA_NEAR - 1) * LANES) // LANES
        nidx = jnp.where(step < 0, DSA_NEAR, step)
        for h in range(H_B):
            kh = k_ref[0, pl.ds(k0, kt_), (h // 2) * LANES:(h // 2 + 1) * LANES]
            qh = qpad_ref[0, :, h * LANES:(h + 1) * LANES]
            s_scr[h] = lax.dot_general(kh, qh, _NT, preferred_element_type=F32)
        m_rows, l_rows, alphas = [], [], []
        for h in range(H_B):
            s = jnp.where(sel, s_scr[h] * (DH_B ** -0.5) + nb_ref[nidx, h], NEG_INF)
            m_old = m_all[h:h + 1]
            m_new = jnp.maximum(m_old, jnp.max(s, axis=0, keepdims=True))
            m_safe = jnp.where(m_new == NEG_INF, 0.0, m_new)
            p = jnp.exp(s - m_safe)
            alpha = jnp.exp(m_old - m_safe)
            p_scr[h] = p.astype(BF16)
            m_rows.append(m_new)
            l_rows.append(alpha * l_all[h:h + 1] + jnp.sum(p, axis=0, keepdims=True))
            alphas.append(alpha)
        for h in range(H_B):
            vth = vt_ref[0, h * DH_B:(h + 1) * DH_B, pl.ds(k0, kt_)]
            acc_refs[h][...] = (alphas[h] * acc_refs[h][...]
                                + jnp.dot(vth, p_scr[h], preferred_element_type=F32))
        return jnp.concatenate(m_rows, axis=0), jnp.concatenate(l_rows, axis=0)

    _, l_all = lax.fori_loop(0, n_tiles, attend_tile,
                             (jnp.full((H_B, qb), NEG_INF, F32), jnp.zeros((H_B, qb), F32)))
    for h in range(H_B):
        o_ref[0, h * DH_B:(h + 1) * DH_B] = acc_refs[h][...] / l_all[h:h + 1]


def _near_bias_tiles(rel_bias, qb):
    k = jnp.arange(DSA_KT, dtype=jnp.int32)[:, None]
    q = jnp.arange(qb, dtype=jnp.int32)[None, :]
    rels = [(j - (DSA_NEAR - 1)) * LANES + k - q for j in range(DSA_NEAR)]
    rels.append(jnp.full((DSA_KT, qb), -(DSA_NEAR * LANES + DSA_KT), jnp.int32))
    tiles = rel_bias[_t5_bucket(jnp.stack(rels))]
    return jnp.moveaxis(tiles, -1, 1).astype(F32)


def dsa_attention(qpad, k_all, v_all, qi, wi, ki_all, rel_bias, offset):
    bx, t, _ = qpad.shape
    n_keys = k_all.shape[1]
    topk = min(TOPK_MAX, n_keys // 4)
    qb = Q_BLOCK if t % Q_BLOCK == 0 else t
    lp = _round_up(n_keys, DSA_KT)
    assert lp <= 2 ** DSA_POS_BITS and offset % DSA_KT == 0 and (qb == Q_BLOCK or t == qb)
    pad = ((0, 0), (0, lp - n_keys), (0, 0))
    kk = jnp.pad(k_all, pad).astype(BF16)
    vt = jnp.pad(v_all, pad).astype(BF16).transpose(0, 2, 1)
    ki = jnp.pad(ki_all, pad).astype(BF16)
    wit = jnp.pad(wi.astype(F32).transpose(0, 2, 1), ((0, 0), (0, 8 - H_I), (0, 0)))
    nb = _near_bias_tiles(rel_bias, qb)
    kern = functools.partial(_dsa_kernel, offset=offset, n_keys=n_keys, topk=topk, qb=qb)
    ot = pl.pallas_call(
        kern,
        grid=(bx, t // qb),
        in_specs=[
            pl.BlockSpec((1, qb, H_B * LANES), lambda b, i: (b, i, 0)),
            pl.BlockSpec((1, qb, H_I * D_IDX), lambda b, i: (b, i, 0)),
            pl.BlockSpec((1, 8, qb), lambda b, i: (b, 0, i)),
            pl.BlockSpec((1, lp, D_B), lambda b, i: (b, 0, 0)),
            pl.BlockSpec((1, D_B, lp), lambda b, i: (b, 0, 0)),
            pl.BlockSpec((1, lp, D_IDX), lambda b, i: (b, 0, 0)),
            pl.BlockSpec((DSA_NEAR + 1, H_B, DSA_KT, qb), lambda b, i: (0, 0, 0, 0)),
        ],
        out_specs=pl.BlockSpec((1, D_B, qb), lambda b, i: (b, 0, i)),
        out_shape=jax.ShapeDtypeStruct((bx, D_B, t), F32),
        scratch_shapes=[pltpu.VMEM((_round_up(n_keys, DSA_KT * DSA_COUNT_TILES), qb), jnp.int32),
                        pltpu.VMEM((H_B, DSA_KT, qb), F32),
                        pltpu.VMEM((H_B, DSA_KT, qb), BF16)]
        + [pltpu.VMEM((DH_B, qb), F32)] * H_B,
        compiler_params=pltpu.CompilerParams(
            dimension_semantics=("arbitrary", "arbitrary"), vmem_limit_bytes=PEER_TABLE_VMEM_LIMIT),
        name="dsa_attention",
    )(qpad, qi.astype(BF16), wit, kk, vt, ki, nb)
    return ot.transpose(0, 2, 1)


WKV_CHUNK = 64
WKV_BLOCK = 256
_TN = (((0,), (0,)), ((), ()))


def _wkv_kernel(r_ref, lw_ref, k_ref, v_ref, kk_ref, a_ref, s0_ref, o_ref, st_ref, s_scr, *, chunk, n_chunks):
    c = chunk

    @pl.when(pl.program_id(1) == 0)
    def _():
        s_scr[...] = s0_ref[0]

    row = lax.broadcasted_iota(jnp.int32, (c, c), 0)
    col = lax.broadcasted_iota(jnp.int32, (c, c), 1)
    strict, incl = row > col, row >= col
    tri = jnp.where(incl, 1.0, 0.0)
    eye = jnp.where(row == col, 1.0, 0.0)
    bdot = lambda x, y: jnp.dot(x.astype(BF16), y.astype(BF16), preferred_element_type=F32)
    bdot_nt = lambda x, y: lax.dot_general(x.astype(BF16), y.astype(BF16), _NT, preferred_element_type=F32)
    bdot_tn = lambda x, y: lax.dot_general(x.astype(BF16), y.astype(BF16), _TN, preferred_element_type=F32)

    def solve_chunk(ci, carry):
        c0 = pl.multiple_of(ci * c, c)
        rows = pl.ds(c0, c)
        heads = range(H_A)
        v, kap_t, r_t, k_b, b_b, decay_all, grams = [], [], [], [], [], [], []
        for h in heads:
            hs = slice(h * DH_A, (h + 1) * DH_A)
            r, lw, k = r_ref[0, rows, hs], lw_ref[0, rows, hs], k_ref[0, rows, hs]
            kap, a = kk_ref[0, rows, hs], a_ref[0, rows, hs]
            cum = jnp.dot(tri, lw, preferred_element_type=F32, precision=lax.Precision.HIGHEST)
            p, p_inv, p_prev = jnp.exp(cum), jnp.exp(-cum), jnp.exp(cum - lw)
            v.append(v_ref[0, rows, hs])
            kap_t.append(kap * p_prev)
            r_t.append(r * p)
            k_b.append(k * p_inv)
            b_b.append(kap * a * p_inv)
            decay_all.append(p[c - 1:c, :])
            grams.append(bdot_nt(jnp.concatenate([kap_t[h], r_t[h]], axis=0),
                                 jnp.concatenate([k_b[h], b_b[h]], axis=0)))
        power = [jnp.where(strict, -g[:c, c:], 0.0) for g in grams]
        inv = [eye + n for n in power]
        for _ in range(c.bit_length() - 2):
            power = [bdot(n, n) for n in power]
            inv = [x + bdot(x, n) for x, n in zip(inv, power)]
        s_prev = [s_scr[h] for h in heads]
        rhs = [bdot_nt(kap_t[h], s_prev[h]) + bdot(jnp.where(strict, grams[h][:c, :c], 0.0), v[h]) for h in heads]
        u = [bdot(inv[h], rhs[h]) for h in heads]
        outs = [bdot_nt(r_t[h], s_prev[h]) + bdot(jnp.where(incl, grams[h][c:, :c], 0.0), v[h])
                - bdot(jnp.where(incl, grams[h][c:, c:], 0.0), u[h]) for h in heads]
        for h in heads:
            s_scr[h] = (s_prev[h] + bdot_tn(v[h], k_b[h]) - bdot_tn(u[h], b_b[h])) * decay_all[h]
        o_ref[0, rows, :] = jnp.concatenate(outs, axis=1)
        return carry

    lax.fori_loop(0, n_chunks, solve_chunk, 0)

    @pl.when(pl.program_id(1) == pl.num_programs(1) - 1)
    def _():
        st_ref[0] = s_scr[...]


def wkv_chunked(r, lw, k, v, kk, a, s0):
    bx, t, _ = r.shape
    chunk = min(WKV_CHUNK, t)
    blk = min(WKV_BLOCK, t)
    assert t % blk == 0 and blk % chunk == 0 and chunk & (chunk - 1) == 0
    tok = pl.BlockSpec((1, blk, D_A), lambda b, i: (b, i, 0))
    state = pl.BlockSpec((1, H_A, DH_A, DH_A), lambda b, i: (b, 0, 0, 0))
    return pl.pallas_call(
        functools.partial(_wkv_kernel, chunk=chunk, n_chunks=blk // chunk),
        grid=(bx, t // blk),
        in_specs=[tok] * 6 + [state],
        out_specs=[tok, state],
        out_shape=[jax.ShapeDtypeStruct((bx, t, D_A), F32),
                   jax.ShapeDtypeStruct((bx, H_A, DH_A, DH_A), F32)],
        scratch_shapes=[pltpu.VMEM((H_A, DH_A, DH_A), F32)],
        compiler_params=pltpu.CompilerParams(
            dimension_semantics=("arbitrary", "arbitrary"), vmem_limit_bytes=VMEM_LIMIT),
        name="wkv_chunked",
    )(r, lw, k, v, kk, a, s0.astype(F32))


CONV_HALO = 32
CONV_BLOCK = 256


def _conv_kernel(u_ref, prev_ref, w_ref, b_ref, g_ref, beta_ref, y_ref, state_ref, pad_scr, *, blk):
    lead = CONV_HALO - (CONV_W - 1)

    @pl.when(pl.program_id(1) == 0)
    def _():
        pad_scr[0:lead, :] = jnp.zeros((lead, D_CONV), F32)
        pad_scr[lead:CONV_HALO, :] = prev_ref[0]

    @pl.when(pl.program_id(1) > 0)
    def _():
        pad_scr[0:CONV_HALO, :] = pad_scr[blk:blk + CONV_HALO, :]

    u = u_ref[0]
    pad_scr[CONV_HALO:CONV_HALO + blk, :] = u[:, :D_CONV] * jax.nn.sigmoid(u[:, D_CONV:])
    y = jnp.zeros((blk, D_CONV), F32) + b_ref[...]
    for j in range(CONV_W):
        y = y + w_ref[j:j + 1, :] * pad_scr[lead + j:lead + j + blk, :]
    m = jnp.mean(y, axis=-1, keepdims=True)
    var = jnp.mean(jnp.square(y - m), axis=-1, keepdims=True)
    z = (y - m) * lax.rsqrt(var + 1e-5) * g_ref[...] + beta_ref[...]
    y_ref[0] = z * jax.nn.sigmoid(z)

    @pl.when(pl.program_id(1) == pl.num_programs(1) - 1)
    def _():
        state_ref[0] = pad_scr[blk + lead:blk + CONV_HALO, :]


def conv_module(u, conv_prev, w_dw, b_dw, ln_g, ln_b):
    bx, t, _ = u.shape
    blk = min(CONV_BLOCK, t)
    assert t % blk == 0 and blk % 8 == 0
    vec = pl.BlockSpec((1, D_CONV), lambda b, i: (0, 0))
    return pl.pallas_call(
        functools.partial(_conv_kernel, blk=blk),
        grid=(bx, t // blk),
        in_specs=[
            pl.BlockSpec((1, blk, 2 * D_CONV), lambda b, i: (b, i, 0)),
            pl.BlockSpec((1, CONV_W - 1, D_CONV), lambda b, i: (b, 0, 0)),
            pl.BlockSpec((CONV_W, D_CONV), lambda b, i: (0, 0)),
            vec, vec, vec,
        ],
        out_specs=[
            pl.BlockSpec((1, blk, D_CONV), lambda b, i: (b, i, 0)),
            pl.BlockSpec((1, CONV_W - 1, D_CONV), lambda b, i: (b, 0, 0)),
        ],
        out_shape=[jax.ShapeDtypeStruct((bx, t, D_CONV), F32),
                   jax.ShapeDtypeStruct((bx, CONV_W - 1, D_CONV), F32)],
        scratch_shapes=[pltpu.VMEM((CONV_HALO + blk, D_CONV), F32)],
        compiler_params=pltpu.CompilerParams(
            dimension_semantics=("arbitrary", "arbitrary"), vmem_limit_bytes=VMEM_LIMIT),
        name="conv_module",
    )(u, conv_prev.astype(F32), w_dw.astype(F32), b_dw[None].astype(F32), ln_g[None].astype(F32),
      ln_b[None].astype(F32))


EVEN_BLOCK = 256
_HI = lax.Precision.HIGHEST


def _head_sum(x, hs):
    return jnp.dot(x, hs, preferred_element_type=F32, precision=_HI)


def _even_prep_kernel(p_ref, shift_ref, mu_ref, w0_ref, a0_ref, wdu_ref, wau_ref, wgu_ref, kk_ref, ka_ref,
                      qn_ref, kn_ref, hs_ref,
                      r_o, lw_o, k_o, v_o, kap_o, a_o, gate_o, qpad_o, kn_o, last_scr, *, blk):
    @pl.when(pl.program_id(1) == 0)
    def _():
        last_scr[0:1, :] = shift_ref[0]

    p = p_ref[0]
    pa = p[:, :A_PROJ]
    first = lax.broadcasted_iota(jnp.int32, pa.shape, 0) == 0
    prev = jnp.where(first, last_scr[0:1, :], pltpu.roll(pa, 1, axis=0))
    last_scr[0:1, :] = pa[blk - 1:blk, :]
    xm = pa + (prev - pa) * mu_ref[...]
    r, k, v = xm[:, :D_A], xm[:, D_A:2 * D_A], xm[:, 2 * D_A:3 * D_A]
    c = 3 * D_A
    dw, da, dg = xm[:, c:c + LORA_W], xm[:, c + LORA_W:c + LORA_W + LORA_A], xm[:, c + LORA_W + LORA_A:A_PROJ]
    bdot = lambda x, w: jnp.dot(x.astype(BF16), w[...], preferred_element_type=F32)
    z = -(w0_ref[...] + bdot(jnp.tanh(dw), wdu_ref))
    w_log = -(jnp.maximum(z, 0.0) + jnp.log(1.0 + jnp.exp(-jnp.abs(z)))) - 0.5
    a = jax.nn.sigmoid(a0_ref[...] + bdot(da, wau_ref))
    kap = k * kk_ref[...]
    hs = hs_ref[...]
    kap = kap / jnp.maximum(jnp.sqrt(_head_sum(kap * kap, hs)), 1e-12)
    r_o[0] = r
    lw_o[0] = -jnp.exp(w_log)
    k_o[0] = k * (1.0 + (a - 1.0) * ka_ref[...])
    v_o[0] = v
    kap_o[0] = kap
    a_o[0] = a
    gate_o[0] = bdot(jax.nn.sigmoid(dg), wgu_ref)

    q = p[:, A_PROJ:A_PROJ + D_B]
    kb = p[:, A_PROJ + D_B:A_PROJ + 2 * D_B]
    qn = q * lax.rsqrt(_head_sum(q * q, hs) * (1.0 / DH_B) + NORM_EPS) * qn_ref[...]
    kn_o[0] = kb * lax.rsqrt(_head_sum(kb * kb, hs) * (1.0 / DH_B) + NORM_EPS) * kn_ref[...]
    upper = lax.broadcasted_iota(jnp.int32, (blk, LANES), 1) >= DH_B
    chunks = []
    for h in range(H_B):
        pair = qn[:, (h // 2) * LANES:(h // 2 + 1) * LANES]
        chunks.append(jnp.where(upper == (h % 2 == 1), pair, 0.0))
    qpad_o[0] = jnp.concatenate(chunks, axis=1).astype(BF16)


def even_prep(p, shift_prev, mu, w0, w_du, a0, w_au, w_gu, k_k, k_a, qn_g, kn_g):
    bx, t, width = p.shape
    blk = min(EVEN_BLOCK, t)
    assert t % blk == 0
    head = jnp.arange(D_A) // DH_A
    hs = (head[:, None] == head[None, :]).astype(F32)
    row = lambda v: v.reshape(1, -1).astype(F32)
    full = lambda a: pl.BlockSpec(a.shape, lambda b, i: (0,) * a.ndim)
    consts = (row(mu), row(w0), row(a0), w_du.astype(BF16), w_au.astype(BF16), w_gu.astype(BF16),
              row(k_k), row(k_a), row(jnp.tile(qn_g, H_B)), row(jnp.tile(kn_g, H_B)), hs)
    tok = pl.BlockSpec((1, blk, D_A), lambda b, i: (b, i, 0))
    return pl.pallas_call(
        functools.partial(_even_prep_kernel, blk=blk),
        grid=(bx, t // blk),
        in_specs=[pl.BlockSpec((1, blk, width), lambda b, i: (b, i, 0)),
                  pl.BlockSpec((1, 1, A_PROJ), lambda b, i: (b, 0, 0))] + [full(c) for c in consts],
        out_specs=[tok] * 7 + [pl.BlockSpec((1, blk, H_B * LANES), lambda b, i: (b, i, 0)), tok],
        out_shape=[jax.ShapeDtypeStruct((bx, t, D_A), F32)] * 7
        + [jax.ShapeDtypeStruct((bx, t, H_B * LANES), BF16), jax.ShapeDtypeStruct((bx, t, D_B), F32)],
        scratch_shapes=[pltpu.VMEM((8, A_PROJ), F32)],
        compiler_params=pltpu.CompilerParams(
            dimension_semantics=("arbitrary", "arbitrary"), vmem_limit_bytes=VMEM_LIMIT),
        name="even_prep",
    )(p, shift_prev[:, None, :].astype(F32), *consts)


def _rwkv_post_kernel(o_ref, r_ref, k_ref, v_ref, gate_ref, rk_ref, gw_ref, gb_ref, hs_ref, out_ref):
    hs = hs_ref[...]
    o = o_ref[...]
    d = o - _head_sum(o, hs) * (1.0 / DH_A)
    var = _head_sum(d * d, hs) * (1.0 / DH_A)
    on = d * lax.rsqrt(var + GN_EPS) * gw_ref[...] + gb_ref[...]
    bonus = _head_sum(r_ref[...] * k_ref[...] * rk_ref[...], hs) * v_ref[...]
    out_ref[...] = (on + bonus) * gate_ref[...]


def rwkv_post(o, r, k, v, gate, r_k, gn_w, gn_b):
    n = o.shape[0]
    blk = min(512, n)
    assert n % blk == 0
    head = jnp.arange(D_A) // DH_A
    hs = (head[:, None] == head[None, :]).astype(F32)
    row = lambda v: v.reshape(1, -1).astype(F32)
    tok = pl.BlockSpec((blk, D_A), lambda i: (i, 0))
    vec = pl.BlockSpec((1, D_A), lambda i: (0, 0))
    return pl.pallas_call(
        _rwkv_post_kernel,
        grid=(n // blk,),
        in_specs=[tok] * 5 + [vec] * 3 + [pl.BlockSpec((D_A, D_A), lambda i: (0, 0))],
        out_specs=tok,
        out_shape=jax.ShapeDtypeStruct((n, D_A), F32),
        compiler_params=pltpu.CompilerParams(
            dimension_semantics=("arbitrary",), vmem_limit_bytes=VMEM_LIMIT),
        name="rwkv_post",
    )(o, r, k, v, gate, row(r_k), row(gn_w), row(gn_b), hs)


def _linear2_residual_kernel(xa_ref, xb_ref, wa_ref, wb_ref, r_ref, o_ref):
    o_ref[...] = (r_ref[...]
                  + jnp.dot(xa_ref[...].astype(BF16), wa_ref[...], preferred_element_type=F32)
                  + jnp.dot(xb_ref[...].astype(BF16), wb_ref[...], preferred_element_type=F32))


def linear2_residual(xa, xb, w, res, row_tile=512):
    n, ka = xa.shape
    kb = xb.shape[1]
    m = w.shape[1]
    tm = min(row_tile, n)
    assert n % tm == 0 and w.shape[0] == ka + kb
    wb16 = w.astype(BF16)
    return pl.pallas_call(
        _linear2_residual_kernel,
        grid=(n // tm,),
        in_specs=[
            pl.BlockSpec((tm, ka), lambda i: (i, 0)),
            pl.BlockSpec((tm, kb), lambda i: (i, 0)),
            pl.BlockSpec((ka, m), lambda i: (0, 0)),
            pl.BlockSpec((kb, m), lambda i: (0, 0)),
            pl.BlockSpec((tm, m), lambda i: (i, 0)),
        ],
        out_specs=pl.BlockSpec((tm, m), lambda i: (i, 0)),
        out_shape=jax.ShapeDtypeStruct((n, m), F32),
        compiler_params=pltpu.CompilerParams(
            dimension_semantics=("arbitrary",), vmem_limit_bytes=VMEM_LIMIT),
        name="linear2_residual",
    )(xa, xb, wb16[:ka], wb16[ka:], res)


def _t5_bucket(rel):
    nb = REL_BUCKETS // 2
    ret = jnp.where(rel > 0, nb, 0)
    n = jnp.abs(rel)
    max_exact = nb // 2
    nf = jnp.maximum(n, 1).astype(F32)
    large = max_exact + (jnp.log(nf / max_exact) / math.log(REL_MAX_DIST / max_exact)
                         * (nb - max_exact)).astype(jnp.int32)
    large = jnp.minimum(large, nb - 1)
    return ret + jnp.where(n < max_exact, n, large)


def _even_mixer(x, shift_prev, wkv_prev, k_past, v_past, kidx_past,
                g_mix, w_in, mu, w0, w_du, a0, w_au, w_gu, k_k, k_a, r_k, gn_w, gn_b,
                qn_g, kn_g, rel_bias, w_out):
    Bx, T, _ = x.shape
    n = Bx * T
    p = norm_linear(x.reshape(n, D_MODEL), g_mix, w_in, keep_padding=True).reshape(Bx, T, -1)
    r, log_decay, k, v, kk, a, gate, qpad, kb = even_prep(
        p, shift_prev, mu, w0, w_du, a0, w_au, w_gu, k_k, k_a, qn_g, kn_g)
    o, wkv_new = wkv_chunked(r, log_decay, k, v, kk, a, wkv_prev)
    rows = lambda t: t.reshape(n, D_A)
    o_a = rwkv_post(rows(o), rows(r), rows(k), rows(v), rows(gate), r_k, gn_w, gn_b)
    c = A_PROJ + 2 * D_B
    vb = p[..., c:c + D_B]
    qi = p[..., c + D_B:c + D_B + H_I * D_IDX]
    ki = p[..., c + D_B + H_I * D_IDX:c + D_B + H_I * D_IDX + D_IDX]
    wi = p[..., c + D_B + H_I * D_IDX + D_IDX:A_PROJ + B_PROJ]
    offset = k_past.shape[1]
    k_all = jnp.concatenate([k_past.reshape(Bx, offset, D_B).astype(F32), kb], axis=1)
    v_all = jnp.concatenate([v_past.reshape(Bx, offset, D_B).astype(F32), vb], axis=1)
    ki_all = jnp.concatenate([kidx_past.astype(F32), ki], axis=1)
    o_b = dsa_attention(qpad, k_all, v_all, qi, wi, ki_all, rel_bias, offset)
    x_new = linear2_residual(o_a, o_b.reshape(n, D_B), w_out, x.reshape(n, D_MODEL)).reshape(Bx, T, D_MODEL)
    heads = lambda t: t.reshape(Bx, T, H_B, DH_B)
    return x_new, p[:, -1, :A_PROJ], wkv_new, heads(kb), heads(vb), ki


def _conv_mixer(x, conv_prev, g_mix, w1, b1, w_dw, b_dw, ln_g, ln_b, w2, b2):
    Bx, T, _ = x.shape
    u = norm_linear(x.reshape(Bx * T, D_MODEL), g_mix, w1, b1).reshape(Bx, T, -1)
    yn, conv_new = conv_module(u, conv_prev, w_dw, b_dw, ln_g, ln_b)
    x_new = linear_residual(yn.reshape(Bx * T, D_CONV), w2, b2,
                            x.reshape(Bx * T, D_MODEL)).reshape(Bx, T, D_MODEL)
    return x_new, conv_new


def _peer_both(xp, xs, g, wq, subkeys, u_tab, v_tab):
    np_ = xp.shape[0] * xp.shape[1]
    rows = jnp.concatenate([xp.reshape(np_, D_MODEL), xs.reshape(-1, D_MODEL)], axis=0)
    out = peer(rows, g, wq, subkeys, u_tab, v_tab)
    return out[:np_].reshape(xp.shape), out[np_:].reshape(xs.shape)


def kernel(x_prompt, x_sample, state_wkv, state_shift, cache_k, cache_v, cache_kidx, state_conv,
           norm_mix, norm_ffn, w_in, mu_shift, w0, w_decay_up, a0, w_iclr_up, w_gate_up,
           k_k, k_a, r_k, gn_w, gn_b, qn_g, kn_g, rel_bias, w_out,
           conv_w1, conv_b1, conv_dw, conv_bdw, conv_ln_g, conv_ln_b, conv_w2, conv_b2,
           peer_wq, peer_subkeys, peer_u, peer_v):
    xp, xs = x_prompt, x_sample
    Bp = xp.shape[0]
    dt = xp.dtype
    wkv_p, shift_p, k_p, v_p, kidx_p, conv_p = [], [], [], [], [], []
    wkv_s, shift_s, k_s, v_s, kidx_s, conv_s = [], [], [], [], [], []
    for li in range(DEPTH):
        if li % 2 == 0:
            e = li // 2
            prm = (norm_mix[li], w_in[e], mu_shift[e], w0[e], w_decay_up[e], a0[e], w_iclr_up[e],
                   w_gate_up[e], k_k[e], k_a[e], r_k[e], gn_w[e], gn_b[e], qn_g[e], kn_g[e],
                   rel_bias, w_out[e])
            xp, sh, wk, kb, vb, ki = _even_mixer(
                xp, jnp.zeros((Bp, A_PROJ), dt), jnp.zeros((Bp, H_A, DH_A, DH_A), dt),
                jnp.zeros((Bp, 0, H_B, DH_B), dt), jnp.zeros((Bp, 0, H_B, DH_B), dt),
                jnp.zeros((Bp, 0, D_IDX), dt), *prm)
            wkv_p.append(wk); shift_p.append(sh); k_p.append(kb); v_p.append(vb); kidx_p.append(ki)
            xs, sh, wk, kb, vb, ki = _even_mixer(
                xs, state_shift[e], state_wkv[e], cache_k[e], cache_v[e], cache_kidx[e], *prm)
            wkv_s.append(wk); shift_s.append(sh); k_s.append(kb); v_s.append(vb); kidx_s.append(ki)
        else:
            o = li // 2
            prm = (norm_mix[li], conv_w1[o], conv_b1[o], conv_dw[o], conv_bdw[o],
                   conv_ln_g[o], conv_ln_b[o], conv_w2[o], conv_b2[o])
            xp, cp = _conv_mixer(xp, jnp.zeros((Bp, CONV_W - 1, D_CONV), dt), *prm)
            xs, cs = _conv_mixer(xs, state_conv[o], *prm)
            conv_p.append(cp); conv_s.append(cs)
        pprm = (norm_ffn[li], peer_wq[li], peer_subkeys[li], peer_u[li], peer_v[li])
        xp, xs = _peer_both(xp, xs, *pprm)
    return (xp, xs,
            jnp.stack(wkv_p), jnp.stack(shift_p), jnp.stack(k_p), jnp.stack(v_p), jnp.stack(kidx_p), jnp.stack(conv_p),
            jnp.stack(wkv_s), jnp.stack(shift_s), jnp.stack(k_s), jnp.stack(v_s), jnp.stack(kidx_s), jnp.stack(conv_s))
```

```python
import functools
import math

import jax
import jax.numpy as jnp
from jax import lax
from jax.experimental import pallas as pl
from jax.experimental.pallas import tpu as pltpu

D_MODEL = 1024
DEPTH = 4
CHUNK = 64
NORM_EPS = 1e-6
D_A = D_MODEL // 2
DH_A = 64
H_A = D_A // DH_A
LORA_W = 64
LORA_A = 64
LORA_G = 128
A_PROJ = 3 * D_A + LORA_W + LORA_A + LORA_G
GN_EPS = 64e-5
D_B = D_MODEL // 2
DH_B = 64
H_B = D_B // DH_B
H_I = 4
D_IDX = 64
TOPK_MAX = 256
Q_BLOCK = 128
B_PROJ = 3 * D_B + H_I * D_IDX + D_IDX + H_I
REL_BUCKETS = 32
REL_MAX_DIST = 128
D_CONV = D_MODEL
CONV_W = 31
PEER_HEADS = 8
PEER_DK = 256
N_KEYS = 128
TOPK_HALF = 16
PEER_TOPK = 16

LANES = 128
VMEM_LIMIT = 48 * 1024 * 1024
PEER_TABLE_VMEM_LIMIT = 56 * 1024 * 1024

F32 = jnp.float32
BF16 = jnp.bfloat16


def _round_up(n, m):
    return -(-n // m) * m


def _col_tile(m):
    for t in (1024, 896, 768, 640, 512, 384, 256, 128):
        if m % t == 0:
            return t
    raise ValueError(m)


def _norm_linear_kernel(x_ref, g_ref, w_ref, b_ref, o_ref, h_scr):
    @pl.when(pl.program_id(1) == 0)
    def _():
        x = x_ref[...]
        ms = jnp.mean(x * x, axis=-1, keepdims=True)
        h_scr[...] = (x * lax.rsqrt(ms + NORM_EPS) * g_ref[...]).astype(BF16)

    o_ref[...] = jnp.dot(h_scr[...], w_ref[...], preferred_element_type=F32) + b_ref[...]


def norm_linear(x, g, w, b=None, row_tile=512, keep_padding=False):
    n, d = x.shape
    m = w.shape[1]
    mp = _round_up(m, 2 * LANES)
    wb = jnp.pad(w.astype(BF16), ((0, 0), (0, mp - m)))
    bb = jnp.zeros((1, mp), F32) if b is None else jnp.pad(b.astype(F32), (0, mp - m))[None]
    tn = _col_tile(mp)
    tm = min(row_tile, n)
    assert n % tm == 0
    out = pl.pallas_call(
        _norm_linear_kernel,
        grid=(n // tm, mp // tn),
        in_specs=[
            pl.BlockSpec((tm, d), lambda i, j: (i, 0)),
            pl.BlockSpec((1, d), lambda i, j: (0, 0)),
            pl.BlockSpec((d, tn), lambda i, j: (0, j)),
            pl.BlockSpec((1, tn), lambda i, j: (0, j)),
        ],
        out_specs=pl.BlockSpec((tm, tn), lambda i, j: (i, j)),
        out_shape=jax.ShapeDtypeStruct((n, mp), F32),
        scratch_shapes=[pltpu.VMEM((tm, d), BF16)],
        compiler_params=pltpu.CompilerParams(
            dimension_semantics=("arbitrary", "arbitrary"), vmem_limit_bytes=VMEM_LIMIT),
        name="norm_linear",
    )(x, g[None].astype(F32), wb, bb)
    return out if (keep_padding or mp == m) else out[:, :m]


def _linear_residual_kernel(x_ref, w_ref, b_ref, r_ref, o_ref):
    o_ref[...] = (r_ref[...] + b_ref[...]
                  + jnp.dot(x_ref[...].astype(BF16), w_ref[...], preferred_element_type=F32))


def linear_residual(x, w, b, res, row_tile=512):
    n, k = x.shape
    m = w.shape[1]
    tm = min(row_tile, n)
    assert n % tm == 0 and m % LANES == 0
    bb = jnp.zeros((1, m), F32) if b is None else b.astype(F32)[None]
    return pl.pallas_call(
        _linear_residual_kernel,
        grid=(n // tm,),
        in_specs=[
            pl.BlockSpec((tm, k), lambda i: (i, 0)),
            pl.BlockSpec((k, m), lambda i: (0, 0)),
            pl.BlockSpec((1, m), lambda i: (0, 0)),
            pl.BlockSpec((tm, m), lambda i: (i, 0)),
        ],
        out_specs=pl.BlockSpec((tm, m), lambda i: (i, 0)),
        out_shape=jax.ShapeDtypeStruct((n, m), F32),
        compiler_params=pltpu.CompilerParams(
            dimension_semantics=("arbitrary",), vmem_limit_bytes=VMEM_LIMIT),
        name="linear_residual",
    )(x, w.astype(BF16), bb, res)


PEER_TB = 128
PEER_E = PEER_HEADS * PEER_TOPK
N_EXPERTS = N_KEYS * N_KEYS
ROW_WORDS = D_MODEL // 2
ROW_SUB = ROW_WORDS // LANES
PSTRIDE = PEER_E + 8
PEER_DOWN_CHUNK = 64
PEER_CAND_ROWS = 16 + 7 * 8 + 8
NEG_INF = float("-inf")


def _top_rows(s, k):
    n = s.shape[0]
    rows = lax.broadcasted_iota(jnp.int32, s.shape, 0).astype(F32)
    out_rows = lax.broadcasted_iota(jnp.int32, (k, s.shape[1]), 0)
    vals = jnp.zeros((k, s.shape[1]), F32)
    ids = jnp.zeros((k, s.shape[1]), F32)
    for it in range(k):
        m = jnp.max(s, axis=0, keepdims=True)
        first = jnp.min(jnp.where(s == m, rows, float(n)), axis=0, keepdims=True)
        vals = jnp.where(out_rows == it, m, vals)
        ids = jnp.where(out_rows == it, first, ids)
        s = jnp.where(rows == first, NEG_INF, s)
    return vals, ids


def _peer_select_kernel(x_ref, g_ref, wq_ref, sk_ref, h_ref, idx_ref, gate_ref):
    x = x_ref[...]
    ms = jnp.mean(x * x, axis=-1, keepdims=True)
    h = x * lax.rsqrt(ms + NORM_EPS) * g_ref[...]
    h_ref[...] = h
    q = jnp.dot(h.astype(BF16), wq_ref[...], preferred_element_type=F32).astype(BF16)
    half = PEER_DK // 2
    tb = x.shape[0]
    assert TOPK_HALF == 16 and PEER_TOPK == 16
    crow = lax.broadcasted_iota(jnp.int32, (PEER_CAND_ROWS, tb), 0).astype(F32)
    sub8 = lax.broadcasted_iota(jnp.int32, (8, tb), 0)
    orow = lax.broadcasted_iota(jnp.int32, (PEER_TOPK, tb), 0)
    ids = []
    for hd in range(PEER_HEADS):
        sv, si = [], []
        for p in range(2):
            c = (hd * 2 + p) * half
            s = lax.dot_general(sk_ref[hd * 2 + p], q[:, c:c + half],
                                (((1,), (1,)), ((), ())), preferred_element_type=F32)
            v, i = _top_rows(s, TOPK_HALF)
            sv.append(v)
            si.append(i)
        cand = [sv[0][0:1] + sv[1]]
        eid = [si[0][0:1] * float(N_KEYS) + si[1]]
        for i in range(1, 8):
            keep = sub8 < PEER_TOPK // (i + 1)
            cand.append(jnp.where(keep, sv[0][i:i + 1] + sv[1][0:8], NEG_INF))
            eid.append(si[0][i:i + 1] * float(N_KEYS) + si[1][0:8])
        cand.append(sv[0][8:16] + sv[1][0:1])
        eid.append(si[0][8:16] * float(N_KEYS) + si[1][0:1])
        cand = jnp.concatenate(cand, axis=0)
        eid = jnp.concatenate(eid, axis=0)
        cs = jnp.zeros((PEER_TOPK, tb), F32)
        ce = jnp.zeros((PEER_TOPK, tb), F32)
        for it in range(PEER_TOPK):
            m = jnp.max(cand, axis=0, keepdims=True)
            first = jnp.min(jnp.where(cand == m, crow, float(crow.shape[0])), axis=0, keepdims=True)
            hit = crow == first
            e = jnp.max(jnp.where(hit, eid, -1.0), axis=0, keepdims=True)
            cs = jnp.where(orow == it, m, cs)
            ce = jnp.where(orow == it, e, ce)
            cand = jnp.where(hit, NEG_INF, cand)
        ex = jnp.exp(cs - cs[0:1])
        gates = ex / jnp.sum(ex, axis=0, keepdims=True)
        ids.append(ce * float(ROW_SUB))
        gate_ref[0, hd * PEER_TOPK:(hd + 1) * PEER_TOPK, :] = gates
    idx_ref[0] = jnp.concatenate(ids, axis=0).T.astype(jnp.int32)


def _table_row(tab, first_row):
    return tab[pl.ds(pl.multiple_of(first_row, ROW_SUB), ROW_SUB), :]


def _unpack_row(row):
    lo = pltpu.bitcast(lax.shift_left(row, 16), F32)
    hi = pltpu.bitcast(jnp.bitwise_and(row, jnp.int32(-65536)), F32)
    return lo, hi


def _load_block_scalars(src_hbm, dst_smem, sem):
    cp = pltpu.make_async_copy(src_hbm.at[pl.program_id(0)], dst_smem, sem)
    cp.start()
    cp.wait()


def _load_table_once(tab_hbm, tab_vmem, sem):
    @pl.when(pl.program_id(0) == 0)
    def _():
        cp = pltpu.make_async_copy(tab_hbm, tab_vmem, sem)
        cp.start()
        cp.wait()


def _peer_up_kernel(idx_hbm, tab_hbm, h_ref, gate_ref, w_ref, tab, idx_s, pbuf_a, pbuf_b, act_s, sems):
    _load_table_once(tab_hbm, tab, sems.at[0])
    _load_block_scalars(idx_hbm, idx_s, sems.at[1])
    lane = lax.broadcasted_iota(jnp.int32, (PEER_E, PEER_TB), 1)
    act_s[...] = jnp.zeros((PEER_E, PEER_TB), F32)
    pbuf_a[...] = jnp.zeros(pbuf_a.shape, F32)
    pbuf_b[...] = jnp.zeros(pbuf_b.shape, F32)

    def gather(t, pbuf):
        ht = h_ref[t]
        h_lo, h_hi = ht[0:ROW_SUB], ht[ROW_SUB:2 * ROW_SUB]
        base = t * PEER_E
        for e in range(PEER_E):
            lo, hi = _unpack_row(_table_row(tab, idx_s[base + e]))
            pbuf[pl.ds(e, ROW_SUB, stride=PSTRIDE), :] = lo * h_lo + hi * h_hi

    def reduce(t, pbuf):
        acc = pbuf[0:PEER_E, :]
        for c in range(1, ROW_SUB):
            acc = acc + pbuf[c * PSTRIDE:c * PSTRIDE + PEER_E, :]
        col = jnp.sum(acc, axis=-1, keepdims=True)
        act_s[...] = jnp.where(lane == t, col, act_s[...])

    def token_pair(j, carry):
        reduce(2 * j - 2, pbuf_a)
        reduce(2 * j - 1, pbuf_b)
        gather(2 * j, pbuf_a)
        gather(2 * j + 1, pbuf_b)
        return carry

    lax.fori_loop(0, PEER_TB // 2, token_pair, 0)
    reduce(PEER_TB - 2, pbuf_a)
    reduce(PEER_TB - 1, pbuf_b)
    a = act_s[...]
    w_ref[0] = (gate_ref[0] * (0.5 * a * (1.0 + lax.erf(a * (2.0 ** -0.5))))).T


def _peer_down_kernel(idx_hbm, w_hbm, tab_hbm, x_ref, o_ref, tab, idx_s, w_s, sems):
    _load_table_once(tab_hbm, tab, sems.at[0])
    _load_block_scalars(idx_hbm, idx_s, sems.at[1])
    _load_block_scalars(w_hbm, w_s, sems.at[2])
    n_acc = 4
    zero = jnp.zeros((ROW_SUB, LANES), F32)

    def token(t, carry):
        def chunk(c, accs):
            acc_lo, acc_hi = list(accs[:n_acc]), list(accs[n_acc:])
            base = t * PEER_E + c * PEER_DOWN_CHUNK
            for j in range(PEER_DOWN_CHUNK):
                lo, hi = _unpack_row(_table_row(tab, idx_s[base + j]))
                w = w_s[base + j]
                acc_lo[j % n_acc] = acc_lo[j % n_acc] + w * lo
                acc_hi[j % n_acc] = acc_hi[j % n_acc] + w * hi
            return tuple(acc_lo) + tuple(acc_hi)

        accs = lax.fori_loop(0, PEER_E // PEER_DOWN_CHUNK, chunk, (zero,) * (2 * n_acc))
        lo = (accs[0] + accs[1]) + (accs[2] + accs[3])
        hi = (accs[4] + accs[5]) + (accs[6] + accs[7])
        o_ref[t] = x_ref[t] + jnp.concatenate([lo, hi], axis=0)
        return carry

    lax.fori_loop(0, PEER_TB, token, 0)


def _pack_table(tab):
    n = tab.shape[0]
    bits = lax.bitcast_convert_type(tab.astype(jnp.bfloat16), jnp.uint16).astype(jnp.uint32)
    words = bits[:, :ROW_WORDS] | (bits[:, ROW_WORDS:] << 16)
    return lax.bitcast_convert_type(words, jnp.int32).reshape(n * ROW_SUB, LANES)


def peer(x, g, wq, subkeys, u_tab, v_tab):
    n, d = x.shape
    assert n % PEER_TB == 0 and d == D_MODEL
    nb = n // PEER_TB
    sk = subkeys.reshape(PEER_HEADS * 2, N_KEYS, PEER_DK // 2).astype(BF16)
    params = pltpu.CompilerParams(dimension_semantics=("arbitrary",), vmem_limit_bytes=VMEM_LIMIT)
    h, idx, gates = pl.pallas_call(
        _peer_select_kernel,
        grid=(nb,),
        in_specs=[
            pl.BlockSpec((PEER_TB, d), lambda i: (i, 0)),
            pl.BlockSpec((1, d), lambda i: (0, 0)),
            pl.BlockSpec((d, PEER_HEADS * PEER_DK), lambda i: (0, 0)),
            pl.BlockSpec((PEER_HEADS * 2, N_KEYS, PEER_DK // 2), lambda i: (0, 0, 0)),
        ],
        out_specs=[
            pl.BlockSpec((PEER_TB, d), lambda i: (i, 0)),
            pl.BlockSpec((1, PEER_TB, PEER_E), lambda i: (i, 0, 0)),
            pl.BlockSpec((1, PEER_E, PEER_TB), lambda i: (i, 0, 0)),
        ],
        out_shape=[
            jax.ShapeDtypeStruct((n, d), F32),
            jax.ShapeDtypeStruct((nb, PEER_TB, PEER_E), jnp.int32),
            jax.ShapeDtypeStruct((nb, PEER_E, PEER_TB), F32),
        ],
        compiler_params=params,
        name="peer_select",
    )(x, g[None].astype(F32), wq.astype(BF16), sk)

    table_params = pltpu.CompilerParams(dimension_semantics=("arbitrary",),
                                        vmem_limit_bytes=PEER_TABLE_VMEM_LIMIT)
    tok_tiles = (PEER_TB, d // LANES, LANES)
    w = pl.pallas_call(
        _peer_up_kernel,
        grid=(nb,),
        in_specs=[
            pl.BlockSpec(memory_space=pl.ANY),
            pl.BlockSpec(memory_space=pl.ANY),
            pl.BlockSpec(tok_tiles, lambda i: (i, 0, 0)),
            pl.BlockSpec((1, PEER_E, PEER_TB), lambda i: (i, 0, 0)),
        ],
        out_specs=pl.BlockSpec((1, PEER_TB, PEER_E), lambda i: (i, 0, 0)),
        out_shape=jax.ShapeDtypeStruct((nb, PEER_TB, PEER_E), F32),
        scratch_shapes=[
            pltpu.VMEM((N_EXPERTS * ROW_SUB, LANES), jnp.int32),
            pltpu.SMEM((PEER_TB * PEER_E,), jnp.int32),
            pltpu.VMEM((ROW_SUB * PSTRIDE, LANES), F32),
            pltpu.VMEM((ROW_SUB * PSTRIDE, LANES), F32),
            pltpu.VMEM((PEER_E, PEER_TB), F32),
            pltpu.SemaphoreType.DMA((2,)),
        ],
        compiler_params=table_params,
        name="peer_up",
    )(idx.reshape(nb, PEER_TB * PEER_E), _pack_table(u_tab), h.reshape(n, d // LANES, LANES), gates)

    out = pl.pallas_call(
        _peer_down_kernel,
        grid=(nb,),
        in_specs=[
            pl.BlockSpec(memory_space=pl.ANY),
            pl.BlockSpec(memory_space=pl.ANY),
            pl.BlockSpec(memory_space=pl.ANY),
            pl.BlockSpec(tok_tiles, lambda i: (i, 0, 0)),
        ],
        out_specs=pl.BlockSpec(tok_tiles, lambda i: (i, 0, 0)),
        out_shape=jax.ShapeDtypeStruct((n, d // LANES, LANES), F32),
        scratch_shapes=[
            pltpu.VMEM((N_EXPERTS * ROW_SUB, LANES), jnp.int32),
            pltpu.SMEM((PEER_TB * PEER_E,), jnp.int32),
            pltpu.SMEM((PEER_TB * PEER_E,), F32),
            pltpu.SemaphoreType.DMA((3,)),
        ],
        compiler_params=table_params,
        name="peer_down",
    )(idx.reshape(nb, PEER_TB * PEER_E), w.reshape(nb, PEER_TB * PEER_E), _pack_table(v_tab),
      x.reshape(n, d // LANES, LANES))
    return out.reshape(n, d)


DSA_KT = 256
DSA_NEAR = 3
DSA_POS_BITS = 14
DSA_COUNT_TILES = 4
CHUNK_SHIFT = CHUNK.bit_length() - 1
INT_MIN = -2 ** 31
KEY_NEG_INF = 0x807FFFFF - 2 ** 32
LOG2_E = 1.4426950408889634
_NT = (((1,), (1,)), ((), ()))


def _sort_key(x):
    b = pltpu.bitcast(x, jnp.int32)
    return b ^ ((b >> 31) & 0x7FFFFFFF)


def _dsa_kernel(qpad_ref, qi_ref, wit_ref, k_ref, vt_ref, ki_ref, nb_ref, o_ref,
                keys_scr, s_scr, p_scr, *acc_refs, offset, n_keys, topk, qb):
    kt_ = DSA_KT
    q0 = offset + pl.program_id(1) * qb
    qpos = q0 + lax.broadcasted_iota(jnp.int32, (1, qb), 1)
    vis_end = jnp.minimum((lax.shift_right_logical(qpos, CHUNK_SHIFT) + 1) * CHUNK, n_keys)
    blk_end = jnp.minimum(((q0 + qb - 1) // CHUNK + 1) * CHUNK, n_keys)
    n_tiles = (blk_end + kt_ - 1) // kt_
    row = lax.broadcasted_iota(jnp.int32, (kt_, qb), 0)

    qi = qi_ref[0]
    qcat = jnp.concatenate([qi[:, h * D_IDX:(h + 1) * D_IDX] for h in range(H_I)], axis=0)
    wit = wit_ref[0] * (H_I ** -0.5)

    def score_tile(t, c):
        k0 = pl.multiple_of(t * kt_, kt_)
        kit = ki_ref[0, pl.ds(k0, kt_), :]
        scores = lax.dot_general(kit, qcat, _NT, preferred_element_type=F32)
        idx = jnp.zeros((kt_, qb), F32)
        for h in range(H_I):
            idx = idx + jnp.maximum(scores[:, h * qb:(h + 1) * qb] * (D_IDX ** -0.5), 0.0) * wit[h:h + 1]
        idx = jnp.where(idx == 0.0, 0.0, idx)
        keys_scr[pl.ds(k0, kt_), :] = jnp.where(row + k0 < vis_end, _sort_key(idx), KEY_NEG_INF)
        return c

    lax.fori_loop(0, n_tiles, score_tile, 0)

    n_steps = (n_tiles + DSA_COUNT_TILES - 1) // DSA_COUNT_TILES

    def pad_tile(t, c):
        keys_scr[pl.ds(pl.multiple_of(t * kt_, kt_), kt_), :] = jnp.full((kt_, qb), KEY_NEG_INF, jnp.int32)
        return c

    lax.fori_loop(n_tiles, n_steps * DSA_COUNT_TILES, pad_tile, 0)

    def count(preds, n_out):
        def body(s, accs):
            accs = list(accs)
            for u in range(DSA_COUNT_TILES):
                k0 = pl.multiple_of((s * DSA_COUNT_TILES + u) * kt_, kt_)
                hits = preds(keys_scr[pl.ds(k0, kt_), :], row + k0)
                for i in range(n_out):
                    accs[i] = accs[i] + jnp.sum(hits[i].reshape(kt_ // 8, 8, qb), axis=0)
            return tuple(accs)
        accs = lax.fori_loop(0, n_steps, body, (jnp.zeros((8, qb), F32),) * n_out)
        return [jnp.sum(acc, axis=0, keepdims=True) for acc in accs]

    one = lambda hit: jnp.where(hit, 1.0, 0.0)
    bit = lambda n: lax.shift_left(jnp.int32(1), n)
    kf = float(topk)
    c0, = count(lambda kt, kp: (one(kt >= 0),), 1)
    thr0 = jnp.where(c0 >= kf, 0, INT_MIN).astype(jnp.int32)

    def thr_bits(j, thr):
        hi, lo = bit(30 - 2 * j), bit(29 - 2 * j)
        c_hl, c_h, c_l = count(lambda kt, kp: (one(kt >= (thr | hi | lo)), one(kt >= (thr | hi)),
                                                one(kt >= (thr | lo))), 3)
        take_hi = c_h >= kf
        take_lo = jnp.where(take_hi, c_hl, c_l) >= kf
        return thr | jnp.where(take_hi, hi, 0) | jnp.where(take_lo, lo, 0)

    thr = lax.fori_loop(0, 15, thr_bits, thr0)
    c_last, = count(lambda kt, kp: (one(kt >= (thr | 1)),), 1)
    thr = jnp.where(c_last >= kf, thr | 1, thr)
    c_gt, c_eq = count(lambda kt, kp: (one(kt > thr), one(kt == thr)), 2)
    need = kf - c_gt
    select_all = vis_end <= topk

    def tie_cut():
        def pos_bits(j, lo_pos):
            hi, lo = bit(DSA_POS_BITS - 1 - 2 * j), bit(DSA_POS_BITS - 2 - 2 * j)
            tied = lambda kt, kp, bound: jnp.where(kt == thr, one(kp < bound), 0.0)
            f_hl, f_h, f_l = count(lambda kt, kp: (tied(kt, kp, lo_pos + hi + lo), tied(kt, kp, lo_pos + hi),
                                                    tied(kt, kp, lo_pos + lo)), 3)
            take_hi = f_h < need
            take_lo = jnp.where(take_hi, f_hl, f_l) < need
            return lo_pos + jnp.where(take_hi, hi, 0) + jnp.where(take_lo, lo, 0)
        return lax.fori_loop(0, DSA_POS_BITS // 2, pos_bits, jnp.zeros((1, qb), jnp.int32)) + 1

    surplus = jnp.max(jnp.where((c_eq > need) & jnp.logical_not(select_all), 1.0, 0.0)) > 0.0
    cut = lax.cond(surplus, tie_cut, lambda: jnp.full((1, qb), 2 ** DSA_POS_BITS, jnp.int32))
    thr = jnp.where(select_all, KEY_NEG_INF, thr)
    cut = jnp.where(select_all, 0, cut)

    for acc in acc_refs:
        acc[...] = jnp.zeros(acc.shape, F32)

    def attend_tile(t, carry):
        m_all, l_all = carry
        k0 = pl.multiple_of(t * kt_, kt_)
        keys = keys_scr[pl.ds(k0, kt_), :]
        sel = (keys - jnp.where(row + k0 < cut, 0, 1)) >= thr
        step = (k0 - q0 + (DSA_NEAR - 1) * LANES) // LANES
        nidx = jnp.where(step < 0, DSA_NEAR, step)
        for h in range(H_B):
            kh = k_ref[0, pl.ds(k0, kt_), (h // 2) * LANES:(h // 2 + 1) * LANES]
            qh = qpad_ref[0, :, h * LANES:(h + 1) * LANES]
            s_scr[h] = lax.dot_general(kh, qh, _NT, preferred_element_type=F32)
        m_rows, l_rows, alphas = [], [], []
        for h in range(H_B):
            s = jnp.where(sel, s_scr[h] * (DH_B ** -0.5 * LOG2_E) + nb_ref[nidx, h], NEG_INF)
            m_old = m_all[h:h + 1]
            m_new = jnp.maximum(m_old, jnp.max(s, axis=0, keepdims=True))
            m_safe = jnp.where(m_new == NEG_INF, 0.0, m_new)
            p = jnp.exp2(s - m_safe)
            alpha = jnp.exp2(m_old - m_safe)
            p_scr[h] = p.astype(BF16)
            m_rows.append(m_new)
            l_rows.append(alpha * l_all[h:h + 1] + jnp.sum(p, axis=0, keepdims=True))
            alphas.append(alpha)
        for h in range(H_B):
            vth = vt_ref[0, h * DH_B:(h + 1) * DH_B, pl.ds(k0, kt_)]
            acc_refs[h][...] = (alphas[h] * acc_refs[h][...]
                                + jnp.dot(vth, p_scr[h], preferred_element_type=F32))
        return jnp.concatenate(m_rows, axis=0), jnp.concatenate(l_rows, axis=0)

    _, l_all = lax.fori_loop(0, n_tiles, attend_tile,
                             (jnp.full((H_B, qb), NEG_INF, F32), jnp.zeros((H_B, qb), F32)))
    for h in range(H_B):
        o_ref[0, h * DH_B:(h + 1) * DH_B] = acc_refs[h][...] / l_all[h:h + 1]


def _near_bias_tiles(rel_bias, qb):
    k = jnp.arange(DSA_KT, dtype=jnp.int32)[:, None]
    q = jnp.arange(qb, dtype=jnp.int32)[None, :]
    rels = [(j - (DSA_NEAR - 1)) * LANES + k - q for j in range(DSA_NEAR)]
    rels.append(jnp.full((DSA_KT, qb), -(DSA_NEAR * LANES + DSA_KT), jnp.int32))
    tiles = rel_bias[_t5_bucket(jnp.stack(rels))]
    return jnp.moveaxis(tiles, -1, 1).astype(F32) * LOG2_E


def dsa_attention(qpad, k_all, v_all, qi, wi, ki_all, rel_bias, offset):
    bx, t, _ = qpad.shape
    n_keys = k_all.shape[1]
    topk = min(TOPK_MAX, n_keys // 4)
    qb = Q_BLOCK if t % Q_BLOCK == 0 else t
    lp = _round_up(n_keys, DSA_KT)
    assert lp <= 2 ** DSA_POS_BITS and offset % DSA_KT == 0 and (qb == Q_BLOCK or t == qb)
    pad = ((0, 0), (0, lp - n_keys), (0, 0))
    kk = jnp.pad(k_all, pad).astype(BF16)
    vt = jnp.pad(v_all, pad).astype(BF16).transpose(0, 2, 1)
    ki = jnp.pad(ki_all, pad).astype(BF16)
    wit = jnp.pad(wi.astype(F32).transpose(0, 2, 1), ((0, 0), (0, 8 - H_I), (0, 0)))
    nb = _near_bias_tiles(rel_bias, qb)
    kern = functools.partial(_dsa_kernel, offset=offset, n_keys=n_keys, topk=topk, qb=qb)
    ot = pl.pallas_call(
        kern,
        grid=(bx, t // qb),
        in_specs=[
            pl.BlockSpec((1, qb, H_B * LANES), lambda b, i: (b, i, 0)),
            pl.BlockSpec((1, qb, H_I * D_IDX), lambda b, i: (b, i, 0)),
            pl.BlockSpec((1, 8, qb), lambda b, i: (b, 0, i)),
            pl.BlockSpec((1, lp, D_B), lambda b, i: (b, 0, 0)),
            pl.BlockSpec((1, D_B, lp), lambda b, i: (b, 0, 0)),
            pl.BlockSpec((1, lp, D_IDX), lambda b, i: (b, 0, 0)),
            pl.BlockSpec((DSA_NEAR + 1, H_B, DSA_KT, qb), lambda b, i: (0, 0, 0, 0)),
        ],
        out_specs=pl.BlockSpec((1, D_B, qb), lambda b, i: (b, 0, i)),
        out_shape=jax.ShapeDtypeStruct((bx, D_B, t), F32),
        scratch_shapes=[pltpu.VMEM((_round_up(n_keys, DSA_KT * DSA_COUNT_TILES), qb), jnp.int32),
                        pltpu.VMEM((H_B, DSA_KT, qb), F32),
                        pltpu.VMEM((H_B, DSA_KT, qb), BF16)]
        + [pltpu.VMEM((DH_B, qb), F32)] * H_B,
        compiler_params=pltpu.CompilerParams(
            dimension_semantics=("arbitrary", "arbitrary"), vmem_limit_bytes=PEER_TABLE_VMEM_LIMIT),
        name="dsa_attention",
    )(qpad, qi.astype(BF16), wit, kk, vt, ki, nb)
    return ot.transpose(0, 2, 1)


WKV_CHUNK = 64
WKV_BLOCK = 256
_TN = (((0,), (0,)), ((), ()))


def _wkv_kernel(r_ref, lw_ref, k_ref, v_ref, kk_ref, a_ref, s0_ref, o_ref, st_ref, s_scr, *, chunk, n_chunks):
    c = chunk

    @pl.when(pl.program_id(1) == 0)
    def _():
        s_scr[...] = s0_ref[0]

    row = lax.broadcasted_iota(jnp.int32, (c, c), 0)
    col = lax.broadcasted_iota(jnp.int32, (c, c), 1)
    strict, incl = row > col, row >= col
    tri = jnp.where(incl, 1.0, 0.0)
    eye = jnp.where(row == col, 1.0, 0.0)
    bdot = lambda x, y: jnp.dot(x.astype(BF16), y.astype(BF16), preferred_element_type=F32)
    bdot_nt = lambda x, y: lax.dot_general(x.astype(BF16), y.astype(BF16), _NT, preferred_element_type=F32)
    bdot_tn = lambda x, y: lax.dot_general(x.astype(BF16), y.astype(BF16), _TN, preferred_element_type=F32)

    def solve_chunk(ci, carry):
        c0 = pl.multiple_of(ci * c, c)
        rows = pl.ds(c0, c)
        heads = range(H_A)
        v, kap_t, r_t, k_b, b_b, decay_all, grams = [], [], [], [], [], [], []
        for h in heads:
            hs = slice(h * DH_A, (h + 1) * DH_A)
            r, lw, k = r_ref[0, rows, hs], lw_ref[0, rows, hs], k_ref[0, rows, hs]
            kap, a = kk_ref[0, rows, hs], a_ref[0, rows, hs]
            cum = jnp.dot(tri, lw, preferred_element_type=F32, precision=lax.Precision.HIGHEST)
            p, p_inv, p_prev = jnp.exp(cum), jnp.exp(-cum), jnp.exp(cum - lw)
            v.append(v_ref[0, rows, hs])
            kap_t.append(kap * p_prev)
            r_t.append(r * p)
            k_b.append(k * p_inv)
            b_b.append(kap * a * p_inv)
            decay_all.append(p[c - 1:c, :])
            grams.append(bdot_nt(jnp.concatenate([kap_t[h], r_t[h]], axis=0),
                                 jnp.concatenate([k_b[h], b_b[h]], axis=0)))
        power = [jnp.where(strict, -g[:c, c:], 0.0) for g in grams]
        inv = [eye + n for n in power]
        for _ in range(c.bit_length() - 2):
            power = [bdot(n, n) for n in power]
            inv = [x + bdot(x, n) for x, n in zip(inv, power)]
        s_prev = [s_scr[h] for h in heads]
        rhs = [bdot_nt(kap_t[h], s_prev[h]) + bdot(jnp.where(strict, grams[h][:c, :c], 0.0), v[h]) for h in heads]
        u = [bdot(inv[h], rhs[h]) for h in heads]
        outs = [bdot_nt(r_t[h], s_prev[h]) + bdot(jnp.where(incl, grams[h][c:, :c], 0.0), v[h])
                - bdot(jnp.where(incl, grams[h][c:, c:], 0.0), u[h]) for h in heads]
        for h in heads:
            s_scr[h] = (s_prev[h] + bdot_tn(v[h], k_b[h]) - bdot_tn(u[h], b_b[h])) * decay_all[h]
        o_ref[0, rows, :] = jnp.concatenate(outs, axis=1)
        return carry

    lax.fori_loop(0, n_chunks, solve_chunk, 0)

    @pl.when(pl.program_id(1) == pl.num_programs(1) - 1)
    def _():
        st_ref[0] = s_scr[...]


def wkv_chunked(r, lw, k, v, kk, a, s0):
    bx, t, _ = r.shape
    chunk = min(WKV_CHUNK, t)
    blk = min(WKV_BLOCK, t)
    assert t % blk == 0 and blk % chunk == 0 and chunk & (chunk - 1) == 0
    tok = pl.BlockSpec((1, blk, D_A), lambda b, i: (b, i, 0))
    state = pl.BlockSpec((1, H_A, DH_A, DH_A), lambda b, i: (b, 0, 0, 0))
    return pl.pallas_call(
        functools.partial(_wkv_kernel, chunk=chunk, n_chunks=blk // chunk),
        grid=(bx, t // blk),
        in_specs=[tok] * 6 + [state],
        out_specs=[tok, state],
        out_shape=[jax.ShapeDtypeStruct((bx, t, D_A), F32),
                   jax.ShapeDtypeStruct((bx, H_A, DH_A, DH_A), F32)],
        scratch_shapes=[pltpu.VMEM((H_A, DH_A, DH_A), F32)],
        compiler_params=pltpu.CompilerParams(
            dimension_semantics=("arbitrary", "arbitrary"), vmem_limit_bytes=VMEM_LIMIT),
        name="wkv_chunked",
    )(r, lw, k, v, kk, a, s0.astype(F32))


CONV_HALO = 32
CONV_BLOCK = 256


def _conv_kernel(u_ref, prev_ref, w_ref, b_ref, g_ref, beta_ref, y_ref, state_ref, pad_scr, *, blk):
    lead = CONV_HALO - (CONV_W - 1)

    @pl.when(pl.program_id(1) == 0)
    def _():
        pad_scr[0:lead, :] = jnp.zeros((lead, D_CONV), F32)
        pad_scr[lead:CONV_HALO, :] = prev_ref[0]

    @pl.when(pl.program_id(1) > 0)
    def _():
        pad_scr[0:CONV_HALO, :] = pad_scr[blk:blk + CONV_HALO, :]

    u = u_ref[0]
    pad_scr[CONV_HALO:CONV_HALO + blk, :] = u[:, :D_CONV] * jax.nn.sigmoid(u[:, D_CONV:])
    y = jnp.zeros((blk, D_CONV), F32) + b_ref[...]
    for j in range(CONV_W):
        y = y + w_ref[j:j + 1, :] * pad_scr[lead + j:lead + j + blk, :]
    m = jnp.mean(y, axis=-1, keepdims=True)
    var = jnp.mean(jnp.square(y - m), axis=-1, keepdims=True)
    z = (y - m) * lax.rsqrt(var + 1e-5) * g_ref[...] + beta_ref[...]
    y_ref[0] = z * jax.nn.sigmoid(z)

    @pl.when(pl.program_id(1) == pl.num_programs(1) - 1)
    def _():
        state_ref[0] = pad_scr[blk + lead:blk + CONV_HALO, :]


def conv_module(u, conv_prev, w_dw, b_dw, ln_g, ln_b):
    bx, t, _ = u.shape
    blk = min(CONV_BLOCK, t)
    assert t % blk == 0 and blk % 8 == 0
    vec = pl.BlockSpec((1, D_CONV), lambda b, i: (0, 0))
    return pl.pallas_call(
        functools.partial(_conv_kernel, blk=blk),
        grid=(bx, t // blk),
        in_specs=[
            pl.BlockSpec((1, blk, 2 * D_CONV), lambda b, i: (b, i, 0)),
            pl.BlockSpec((1, CONV_W - 1, D_CONV), lambda b, i: (b, 0, 0)),
            pl.BlockSpec((CONV_W, D_CONV), lambda b, i: (0, 0)),
            vec, vec, vec,
        ],
        out_specs=[
            pl.BlockSpec((1, blk, D_CONV), lambda b, i: (b, i, 0)),
            pl.BlockSpec((1, CONV_W - 1, D_CONV), lambda b, i: (b, 0, 0)),
        ],
        out_shape=[jax.ShapeDtypeStruct((bx, t, D_CONV), F32),
                   jax.ShapeDtypeStruct((bx, CONV_W - 1, D_CONV), F32)],
        scratch_shapes=[pltpu.VMEM((CONV_HALO + blk, D_CONV), F32)],
        compiler_params=pltpu.CompilerParams(
            dimension_semantics=("arbitrary", "arbitrary"), vmem_limit_bytes=VMEM_LIMIT),
        name="conv_module",
    )(u, conv_prev.astype(F32), w_dw.astype(F32), b_dw[None].astype(F32), ln_g[None].astype(F32),
      ln_b[None].astype(F32))


EVEN_BLOCK = 256
_HI = lax.Precision.HIGHEST


def _head_sum(x, hs):
    return jnp.dot(x, hs, preferred_element_type=F32, precision=_HI)


def _even_prep_kernel(p_ref, shift_ref, mu_ref, w0_ref, a0_ref, wdu_ref, wau_ref, wgu_ref, kk_ref, ka_ref,
                      qn_ref, kn_ref, hs_ref,
                      r_o, lw_o, k_o, v_o, kap_o, a_o, gate_o, qpad_o, kn_o, last_scr, *, blk):
    @pl.when(pl.program_id(1) == 0)
    def _():
        last_scr[0:1, :] = shift_ref[0]

    p = p_ref[0]
    pa = p[:, :A_PROJ]
    first = lax.broadcasted_iota(jnp.int32, pa.shape, 0) == 0
    prev = jnp.where(first, last_scr[0:1, :], pltpu.roll(pa, 1, axis=0))
    last_scr[0:1, :] = pa[blk - 1:blk, :]
    xm = pa + (prev - pa) * mu_ref[...]
    r, k, v = xm[:, :D_A], xm[:, D_A:2 * D_A], xm[:, 2 * D_A:3 * D_A]
    c = 3 * D_A
    dw, da, dg = xm[:, c:c + LORA_W], xm[:, c + LORA_W:c + LORA_W + LORA_A], xm[:, c + LORA_W + LORA_A:A_PROJ]
    bdot = lambda x, w: jnp.dot(x.astype(BF16), w[...], preferred_element_type=F32)
    z = -(w0_ref[...] + bdot(jnp.tanh(dw), wdu_ref))
    w_log = -(jnp.maximum(z, 0.0) + jnp.log(1.0 + jnp.exp(-jnp.abs(z)))) - 0.5
    a = jax.nn.sigmoid(a0_ref[...] + bdot(da, wau_ref))
    kap = k * kk_ref[...]
    hs = hs_ref[...]
    kap = kap / jnp.maximum(jnp.sqrt(_head_sum(kap * kap, hs)), 1e-12)
    r_o[0] = r
    lw_o[0] = -jnp.exp(w_log)
    k_o[0] = k * (1.0 + (a - 1.0) * ka_ref[...])
    v_o[0] = v
    kap_o[0] = kap
    a_o[0] = a
    gate_o[0] = bdot(jax.nn.sigmoid(dg), wgu_ref)

    q = p[:, A_PROJ:A_PROJ + D_B]
    kb = p[:, A_PROJ + D_B:A_PROJ + 2 * D_B]
    qn = q * lax.rsqrt(_head_sum(q * q, hs) * (1.0 / DH_B) + NORM_EPS) * qn_ref[...]
    kn_o[0] = kb * lax.rsqrt(_head_sum(kb * kb, hs) * (1.0 / DH_B) + NORM_EPS) * kn_ref[...]
    upper = lax.broadcasted_iota(jnp.int32, (blk, LANES), 1) >= DH_B
    chunks = []
    for h in range(H_B):
        pair = qn[:, (h // 2) * LANES:(h // 2 + 1) * LANES]
        chunks.append(jnp.where(upper == (h % 2 == 1), pair, 0.0))
    qpad_o[0] = jnp.concatenate(chunks, axis=1).astype(BF16)


def even_prep(p, shift_prev, mu, w0, w_du, a0, w_au, w_gu, k_k, k_a, qn_g, kn_g):
    bx, t, width = p.shape
    blk = min(EVEN_BLOCK, t)
    assert t % blk == 0
    head = jnp.arange(D_A) // DH_A
    hs = (head[:, None] == head[None, :]).astype(F32)
    row = lambda v: v.reshape(1, -1).astype(F32)
    full = lambda a: pl.BlockSpec(a.shape, lambda b, i: (0,) * a.ndim)
    consts = (row(mu), row(w0), row(a0), w_du.astype(BF16), w_au.astype(BF16), w_gu.astype(BF16),
              row(k_k), row(k_a), row(jnp.tile(qn_g, H_B)), row(jnp.tile(kn_g, H_B)), hs)
    tok = pl.BlockSpec((1, blk, D_A), lambda b, i: (b, i, 0))
    return pl.pallas_call(
        functools.partial(_even_prep_kernel, blk=blk),
        grid=(bx, t // blk),
        in_specs=[pl.BlockSpec((1, blk, width), lambda b, i: (b, i, 0)),
                  pl.BlockSpec((1, 1, A_PROJ), lambda b, i: (b, 0, 0))] + [full(c) for c in consts],
        out_specs=[tok] * 7 + [pl.BlockSpec((1, blk, H_B * LANES), lambda b, i: (b, i, 0)), tok],
        out_shape=[jax.ShapeDtypeStruct((bx, t, D_A), F32)] * 7
        + [jax.ShapeDtypeStruct((bx, t, H_B * LANES), BF16), jax.ShapeDtypeStruct((bx, t, D_B), F32)],
        scratch_shapes=[pltpu.VMEM((8, A_PROJ), F32)],
        compiler_params=pltpu.CompilerParams(
            dimension_semantics=("arbitrary", "arbitrary"), vmem_limit_bytes=VMEM_LIMIT),
        name="even_prep",
    )(p, shift_prev[:, None, :].astype(F32), *consts)


def _rwkv_post_kernel(o_ref, r_ref, k_ref, v_ref, gate_ref, rk_ref, gw_ref, gb_ref, hs_ref, out_ref):
    hs = hs_ref[...]
    o = o_ref[...]
    d = o - _head_sum(o, hs) * (1.0 / DH_A)
    var = _head_sum(d * d, hs) * (1.0 / DH_A)
    on = d * lax.rsqrt(var + GN_EPS) * gw_ref[...] + gb_ref[...]
    bonus = _head_sum(r_ref[...] * k_ref[...] * rk_ref[...], hs) * v_ref[...]
    out_ref[...] = (on + bonus) * gate_ref[...]


def rwkv_post(o, r, k, v, gate, r_k, gn_w, gn_b):
    n = o.shape[0]
    blk = min(512, n)
    assert n % blk == 0
    head = jnp.arange(D_A) // DH_A
    hs = (head[:, None] == head[None, :]).astype(F32)
    row = lambda v: v.reshape(1, -1).astype(F32)
    tok = pl.BlockSpec((blk, D_A), lambda i: (i, 0))
    vec = pl.BlockSpec((1, D_A), lambda i: (0, 0))
    return pl.pallas_call(
        _rwkv_post_kernel,
        grid=(n // blk,),
        in_specs=[tok] * 5 + [vec] * 3 + [pl.BlockSpec((D_A, D_A), lambda i: (0, 0))],
        out_specs=tok,
        out_shape=jax.ShapeDtypeStruct((n, D_A), F32),
        compiler_params=pltpu.CompilerParams(
            dimension_semantics=("arbitrary",), vmem_limit_bytes=VMEM_LIMIT),
        name="rwkv_post",
    )(o, r, k, v, gate, row(r_k), row(gn_w), row(gn_b), hs)


def _linear2_residual_kernel(xa_ref, xb_ref, wa_ref, wb_ref, r_ref, o_ref):
    o_ref[...] = (r_ref[...]
                  + jnp.dot(xa_ref[...].astype(BF16), wa_ref[...], preferred_element_type=F32)
                  + jnp.dot(xb_ref[...].astype(BF16), wb_ref[...], preferred_element_type=F32))


def linear2_residual(xa, xb, w, res, row_tile=512):
    n, ka = xa.shape
    kb = xb.shape[1]
    m = w.shape[1]
    tm = min(row_tile, n)
    assert n % tm == 0 and w.shape[0] == ka + kb
    wb16 = w.astype(BF16)
    return pl.pallas_call(
        _linear2_residual_kernel,
        grid=(n // tm,),
        in_specs=[
            pl.BlockSpec((tm, ka), lambda i: (i, 0)),
            pl.BlockSpec((tm, kb), lambda i: (i, 0)),
            pl.BlockSpec((ka, m), lambda i: (0, 0)),
            pl.BlockSpec((kb, m), lambda i: (0, 0)),
            pl.BlockSpec((tm, m), lambda i: (i, 0)),
        ],
        out_specs=pl.BlockSpec((tm, m), lambda i: (i, 0)),
        out_shape=jax.ShapeDtypeStruct((n, m), F32),
        compiler_params=pltpu.CompilerParams(
            dimension_semantics=("arbitrary",), vmem_limit_bytes=VMEM_LIMIT),
        name="linear2_residual",
    )(xa, xb, wb16[:ka], wb16[ka:], res)


def _t5_bucket(rel):
    nb = REL_BUCKETS // 2
    ret = jnp.where(rel > 0, nb, 0)
    n = jnp.abs(rel)
    max_exact = nb // 2
    nf = jnp.maximum(n, 1).astype(F32)
    large = max_exact + (jnp.log(nf / max_exact) / math.log(REL_MAX_DIST / max_exact)
                         * (nb - max_exact)).astype(jnp.int32)
    large = jnp.minimum(large, nb - 1)
    return ret + jnp.where(n < max_exact, n, large)


def _even_mixer(x, shift_prev, wkv_prev, k_past, v_past, kidx_past,
                g_mix, w_in, mu, w0, w_du, a0, w_au, w_gu, k_k, k_a, r_k, gn_w, gn_b,
                qn_g, kn_g, rel_bias, w_out):
    Bx, T, _ = x.shape
    n = Bx * T
    p = norm_linear(x.reshape(n, D_MODEL), g_mix, w_in, keep_padding=True).reshape(Bx, T, -1)
    r, log_decay, k, v, kk, a, gate, qpad, kb = even_prep(
        p, shift_prev, mu, w0, w_du, a0, w_au, w_gu, k_k, k_a, qn_g, kn_g)
    o, wkv_new = wkv_chunked(r, log_decay, k, v, kk, a, wkv_prev)
    rows = lambda t: t.reshape(n, D_A)
    o_a = rwkv_post(rows(o), rows(r), rows(k), rows(v), rows(gate), r_k, gn_w, gn_b)
    c = A_PROJ + 2 * D_B
    vb = p[..., c:c + D_B]
    qi = p[..., c + D_B:c + D_B + H_I * D_IDX]
    ki = p[..., c + D_B + H_I * D_IDX:c + D_B + H_I * D_IDX + D_IDX]
    wi = p[..., c + D_B + H_I * D_IDX + D_IDX:A_PROJ + B_PROJ]
    offset = k_past.shape[1]
    k_all = jnp.concatenate([k_past.reshape(Bx, offset, D_B).astype(F32), kb], axis=1)
    v_all = jnp.concatenate([v_past.reshape(Bx, offset, D_B).astype(F32), vb], axis=1)
    ki_all = jnp.concatenate([kidx_past.astype(F32), ki], axis=1)
    o_b = dsa_attention(qpad, k_all, v_all, qi, wi, ki_all, rel_bias, offset)
    x_new = linear2_residual(o_a, o_b.reshape(n, D_B), w_out, x.reshape(n, D_MODEL)).reshape(Bx, T, D_MODEL)
    heads = lambda t: t.reshape(Bx, T, H_B, DH_B)
    return x_new, p[:, -1, :A_PROJ], wkv_new, heads(kb), heads(vb), ki


def _conv_mixer(x, conv_prev, g_mix, w1, b1, w_dw, b_dw, ln_g, ln_b, w2, b2):
    Bx, T, _ = x.shape
    u = norm_linear(x.reshape(Bx * T, D_MODEL), g_mix, w1, b1).reshape(Bx, T, -1)
    yn, conv_new = conv_module(u, conv_prev, w_dw, b_dw, ln_g, ln_b)
    x_new = linear_residual(yn.reshape(Bx * T, D_CONV), w2, b2,
                            x.reshape(Bx * T, D_MODEL)).reshape(Bx, T, D_MODEL)
    return x_new, conv_new


def _peer_both(xp, xs, g, wq, subkeys, u_tab, v_tab):
    np_ = xp.shape[0] * xp.shape[1]
    rows = jnp.concatenate([xp.reshape(np_, D_MODEL), xs.reshape(-1, D_MODEL)], axis=0)
    out = peer(rows, g, wq, subkeys, u_tab, v_tab)
    return out[:np_].reshape(xp.shape), out[np_:].reshape(xs.shape)


def kernel(x_prompt, x_sample, state_wkv, state_shift, cache_k, cache_v, cache_kidx, state_conv,
           norm_mix, norm_ffn, w_in, mu_shift, w0, w_decay_up, a0, w_iclr_up, w_gate_up,
           k_k, k_a, r_k, gn_w, gn_b, qn_g, kn_g, rel_bias, w_out,
           conv_w1, conv_b1, conv_dw, conv_bdw, conv_ln_g, conv_ln_b, conv_w2, conv_b2,
           peer_wq, peer_subkeys, peer_u, peer_v):
    xp, xs = x_prompt, x_sample
    Bp = xp.shape[0]
    dt = xp.dtype
    wkv_p, shift_p, k_p, v_p, kidx_p, conv_p = [], [], [], [], [], []
    wkv_s, shift_s, k_s, v_s, kidx_s, conv_s = [], [], [], [], [], []
    for li in range(DEPTH):
        if li % 2 == 0:
            e = li // 2
            prm = (norm_mix[li], w_in[e], mu_shift[e], w0[e], w_decay_up[e], a0[e], w_iclr_up[e],
                   w_gate_up[e], k_k[e], k_a[e], r_k[e], gn_w[e], gn_b[e], qn_g[e], kn_g[e],
                   rel_bias, w_out[e])
            xp, sh, wk, kb, vb, ki = _even_mixer(
                xp, jnp.zeros((Bp, A_PROJ), dt), jnp.zeros((Bp, H_A, DH_A, DH_A), dt),
                jnp.zeros((Bp, 0, H_B, DH_B), dt), jnp.zeros((Bp, 0, H_B, DH_B), dt),
                jnp.zeros((Bp, 0, D_IDX), dt), *prm)
            wkv_p.append(wk); shift_p.append(sh); k_p.append(kb); v_p.append(vb); kidx_p.append(ki)
            xs, sh, wk, kb, vb, ki = _even_mixer(
                xs, state_shift[e], state_wkv[e], cache_k[e], cache_v[e], cache_kidx[e], *prm)
            wkv_s.append(wk); shift_s.append(sh); k_s.append(kb); v_s.append(vb); kidx_s.append(ki)
        else:
            o = li // 2
            prm = (norm_mix[li], conv_w1[o], conv_b1[o], conv_dw[o], conv_bdw[o],
                   conv_ln_g[o], conv_ln_b[o], conv_w2[o], conv_b2[o])
            xp, cp = _conv_mixer(xp, jnp.zeros((Bp, CONV_W - 1, D_CONV), dt), *prm)
            xs, cs = _conv_mixer(xs, state_conv[o], *prm)
            conv_p.append(cp); conv_s.append(cs)
        pprm = (norm_ffn[li], peer_wq[li], peer_subkeys[li], peer_u[li], peer_v[li])
        xp, xs = _peer_both(xp, xs, *pprm)
    return (xp, xs,
            jnp.stack(wkv_p), jnp.stack(shift_p), jnp.stack(k_p), jnp.stack(v_p), jnp.stack(kidx_p), jnp.stack(conv_p),
            jnp.stack(wkv_s), jnp.stack(shift_s), jnp.stack(k_s), jnp.stack(v_s), jnp.stack(kidx_s), jnp.stack(conv_s))
```

```python
import functools
import math

import jax
import jax.numpy as jnp
from jax import lax
from jax.experimental import pallas as pl
from jax.experimental.pallas import tpu as pltpu

D_MODEL = 1024
DEPTH = 4
CHUNK = 64
NORM_EPS = 1e-6
D_A = D_MODEL // 2
DH_A = 64
H_A = D_A // DH_A
LORA_W = 64
LORA_A = 64
LORA_G = 128
A_PROJ = 3 * D_A + LORA_W + LORA_A + LORA_G
GN_EPS = 64e-5
D_B = D_MODEL // 2
DH_B = 64
H_B = D_B // DH_B
H_I = 4
D_IDX = 64
TOPK_MAX = 256
Q_BLOCK = 128
B_PROJ = 3 * D_B + H_I * D_IDX + D_IDX + H_I
REL_BUCKETS = 32
REL_MAX_DIST = 128
D_CONV = D_MODEL
CONV_W = 31
PEER_HEADS = 8
PEER_DK = 256
N_KEYS = 128
TOPK_HALF = 16
PEER_TOPK = 16

LANES = 128
VMEM_LIMIT = 48 * 1024 * 1024
PEER_TABLE_VMEM_LIMIT = 56 * 1024 * 1024

F32 = jnp.float32
BF16 = jnp.bfloat16


def _round_up(n, m):
    return -(-n // m) * m


def _col_tile(m):
    for t in (1024, 896, 768, 640, 512, 384, 256, 128):
        if m % t == 0:
            return t
    raise ValueError(m)


def _norm_linear_kernel(x_ref, g_ref, w_ref, b_ref, o_ref, h_scr):
    @pl.when(pl.program_id(1) == 0)
    def _():
        x = x_ref[...]
        ms = jnp.mean(x * x, axis=-1, keepdims=True)
        h_scr[...] = (x * lax.rsqrt(ms + NORM_EPS) * g_ref[...]).astype(BF16)

    o_ref[...] = jnp.dot(h_scr[...], w_ref[...], preferred_element_type=F32) + b_ref[...]


def norm_linear(x, g, w, b=None, row_tile=512, keep_padding=False):
    n, d = x.shape
    m = w.shape[1]
    mp = _round_up(m, 2 * LANES)
    wb = jnp.pad(w.astype(BF16), ((0, 0), (0, mp - m)))
    bb = jnp.zeros((1, mp), F32) if b is None else jnp.pad(b.astype(F32), (0, mp - m))[None]
    tn = _col_tile(mp)
    tm = min(row_tile, n)
    assert n % tm == 0
    out = pl.pallas_call(
        _norm_linear_kernel,
        grid=(n // tm, mp // tn),
        in_specs=[
            pl.BlockSpec((tm, d), lambda i, j: (i, 0)),
            pl.BlockSpec((1, d), lambda i, j: (0, 0)),
            pl.BlockSpec((d, tn), lambda i, j: (0, j)),
            pl.BlockSpec((1, tn), lambda i, j: (0, j)),
        ],
        out_specs=pl.BlockSpec((tm, tn), lambda i, j: (i, j)),
        out_shape=jax.ShapeDtypeStruct((n, mp), F32),
        scratch_shapes=[pltpu.VMEM((tm, d), BF16)],
        compiler_params=pltpu.CompilerParams(
            dimension_semantics=("arbitrary", "arbitrary"), vmem_limit_bytes=VMEM_LIMIT),
        name="norm_linear",
    )(x, g[None].astype(F32), wb, bb)
    return out if (keep_padding or mp == m) else out[:, :m]


def _linear_residual_kernel(x_ref, w_ref, b_ref, r_ref, o_ref):
    o_ref[...] = (r_ref[...] + b_ref[...]
                  + jnp.dot(x_ref[...].astype(BF16), w_ref[...], preferred_element_type=F32))


def linear_residual(x, w, b, res, row_tile=512):
    n, k = x.shape
    m = w.shape[1]
    tm = min(row_tile, n)
    assert n % tm == 0 and m % LANES == 0
    bb = jnp.zeros((1, m), F32) if b is None else b.astype(F32)[None]
    return pl.pallas_call(
        _linear_residual_kernel,
        grid=(n // tm,),
        in_specs=[
            pl.BlockSpec((tm, k), lambda i: (i, 0)),
            pl.BlockSpec((k, m), lambda i: (0, 0)),
            pl.BlockSpec((1, m), lambda i: (0, 0)),
            pl.BlockSpec((tm, m), lambda i: (i, 0)),
        ],
        out_specs=pl.BlockSpec((tm, m), lambda i: (i, 0)),
        out_shape=jax.ShapeDtypeStruct((n, m), F32),
        compiler_params=pltpu.CompilerParams(
            dimension_semantics=("arbitrary",), vmem_limit_bytes=VMEM_LIMIT),
        name="linear_residual",
    )(x, w.astype(BF16), bb, res)


PEER_TB = 128
PEER_E = PEER_HEADS * PEER_TOPK
N_EXPERTS = N_KEYS * N_KEYS
ROW_WORDS = D_MODEL // 2
ROW_SUB = ROW_WORDS // LANES
PSTRIDE = PEER_E + 8
PEER_DOWN_CHUNK = 64
PEER_CAND_ROWS = 16 + 7 * 8 + 8
NEG_INF = float("-inf")


def _top_rows(s, k):
    n = s.shape[0]
    rows = lax.broadcasted_iota(jnp.int32, s.shape, 0).astype(F32)
    out_rows = lax.broadcasted_iota(jnp.int32, (k, s.shape[1]), 0)
    vals = jnp.zeros((k, s.shape[1]), F32)
    ids = jnp.zeros((k, s.shape[1]), F32)
    for it in range(k):
        m = jnp.max(s, axis=0, keepdims=True)
        first = jnp.min(jnp.where(s == m, rows, float(n)), axis=0, keepdims=True)
        vals = jnp.where(out_rows == it, m, vals)
        ids = jnp.where(out_rows == it, first, ids)
        s = jnp.where(rows == first, NEG_INF, s)
    return vals, ids


def _peer_select_kernel(x_ref, g_ref, wq_ref, sk_ref, h_ref, idx_ref, gate_ref):
    x = x_ref[...]
    ms = jnp.mean(x * x, axis=-1, keepdims=True)
    h = x * lax.rsqrt(ms + NORM_EPS) * g_ref[...]
    h_ref[...] = h
    q = jnp.dot(h.astype(BF16), wq_ref[...], preferred_element_type=F32).astype(BF16)
    half = PEER_DK // 2
    tb = x.shape[0]
    assert TOPK_HALF == 16 and PEER_TOPK == 16
    crow = lax.broadcasted_iota(jnp.int32, (PEER_CAND_ROWS, tb), 0).astype(F32)
    sub8 = lax.broadcasted_iota(jnp.int32, (8, tb), 0)
    orow = lax.broadcasted_iota(jnp.int32, (PEER_TOPK, tb), 0)
    ids = []
    for hd in range(PEER_HEADS):
        sv, si = [], []
        for p in range(2):
            c = (hd * 2 + p) * half
            s = lax.dot_general(sk_ref[hd * 2 + p], q[:, c:c + half],
                                (((1,), (1,)), ((), ())), preferred_element_type=F32)
            v, i = _top_rows(s, TOPK_HALF)
            sv.append(v)
            si.append(i)
        cand = [sv[0][0:1] + sv[1]]
        eid = [si[0][0:1] * float(N_KEYS) + si[1]]
        for i in range(1, 8):
            keep = sub8 < PEER_TOPK // (i + 1)
            cand.append(jnp.where(keep, sv[0][i:i + 1] + sv[1][0:8], NEG_INF))
            eid.append(si[0][i:i + 1] * float(N_KEYS) + si[1][0:8])
        cand.append(sv[0][8:16] + sv[1][0:1])
        eid.append(si[0][8:16] * float(N_KEYS) + si[1][0:1])
        cand = jnp.concatenate(cand, axis=0)
        eid = jnp.concatenate(eid, axis=0)
        cs = jnp.zeros((PEER_TOPK, tb), F32)
        ce = jnp.zeros((PEER_TOPK, tb), F32)
        for it in range(PEER_TOPK):
            m = jnp.max(cand, axis=0, keepdims=True)
            first = jnp.min(jnp.where(cand == m, crow, float(crow.shape[0])), axis=0, keepdims=True)
            hit = crow == first
            e = jnp.max(jnp.where(hit, eid, -1.0), axis=0, keepdims=True)
            cs = jnp.where(orow == it, m, cs)
            ce = jnp.where(orow == it, e, ce)
            cand = jnp.where(hit, NEG_INF, cand)
        ex = jnp.exp(cs - cs[0:1])
        gates = ex / jnp.sum(ex, axis=0, keepdims=True)
        ids.append(ce * float(ROW_SUB))
        gate_ref[0, hd * PEER_TOPK:(hd + 1) * PEER_TOPK, :] = gates
    idx_ref[0] = jnp.concatenate(ids, axis=0).T.astype(jnp.int32)


def _table_row(tab, first_row):
    return tab[pl.ds(pl.multiple_of(first_row, ROW_SUB), ROW_SUB), :]


def _unpack_row(row):
    lo = pltpu.bitcast(lax.shift_left(row, 16), F32)
    hi = pltpu.bitcast(jnp.bitwise_and(row, jnp.int32(-65536)), F32)
    return lo, hi


def _ring_block_scalars(src_hbm, ring_smem, sems):
    i = pl.program_id(0)
    n = PEER_TB * PEER_E

    def copy(step, slot):
        dst = ring_smem.at[pl.ds(pl.multiple_of(slot * n, n), n)]
        return pltpu.make_async_copy(src_hbm.at[step], dst, sems.at[slot])

    @pl.when(i == 0)
    def _():
        copy(0, 0).start()

    slot = lax.rem(i, 2)
    copy(i, slot).wait()

    @pl.when(i + 1 < pl.num_programs(0))
    def _():
        copy(i + 1, 1 - slot).start()

    return slot * n


def _load_table_once(tab_hbm, tab_vmem, sem):
    @pl.when(pl.program_id(0) == 0)
    def _():
        cp = pltpu.make_async_copy(tab_hbm, tab_vmem, sem)
        cp.start()
        cp.wait()


def _peer_up_kernel(idx_hbm, tab_hbm, h_ref, gate_ref, w_ref, tab, idx_s, pbuf_a, pbuf_b, act_s, tab_sem, idx_sems):
    _load_table_once(tab_hbm, tab, tab_sem)
    slot_off = _ring_block_scalars(idx_hbm, idx_s, idx_sems)
    lane = lax.broadcasted_iota(jnp.int32, (PEER_E, PEER_TB), 1)
    act_s[...] = jnp.zeros((PEER_E, PEER_TB), F32)
    pbuf_a[...] = jnp.zeros(pbuf_a.shape, F32)
    pbuf_b[...] = jnp.zeros(pbuf_b.shape, F32)

    def gather(t, pbuf):
        ht = h_ref[t]
        h_lo, h_hi = ht[0:ROW_SUB], ht[ROW_SUB:2 * ROW_SUB]
        base = slot_off + t * PEER_E
        for e in range(PEER_E):
            lo, hi = _unpack_row(_table_row(tab, idx_s[base + e]))
            pbuf[pl.ds(e, ROW_SUB, stride=PSTRIDE), :] = lo * h_lo + hi * h_hi

    def reduce(t, pbuf):
        acc = pbuf[0:PEER_E, :]
        for c in range(1, ROW_SUB):
            acc = acc + pbuf[c * PSTRIDE:c * PSTRIDE + PEER_E, :]
        col = jnp.sum(acc, axis=-1, keepdims=True)
        act_s[...] = jnp.where(lane == t, col, act_s[...])

    def token_pair(j, carry):
        reduce(2 * j - 2, pbuf_a)
        reduce(2 * j - 1, pbuf_b)
        gather(2 * j, pbuf_a)
        gather(2 * j + 1, pbuf_b)
        return carry

    lax.fori_loop(0, PEER_TB // 2, token_pair, 0)
    reduce(PEER_TB - 2, pbuf_a)
    reduce(PEER_TB - 1, pbuf_b)
    a = act_s[...]
    w_ref[0] = (gate_ref[0] * (0.5 * a * (1.0 + lax.erf(a * (2.0 ** -0.5))))).T


def _peer_down_kernel(idx_hbm, w_hbm, tab_hbm, x_ref, o_ref, tab, idx_s, w_s, tab_sem, idx_sems, w_sems):
    _load_table_once(tab_hbm, tab, tab_sem)
    slot_off = _ring_block_scalars(idx_hbm, idx_s, idx_sems)
    _ring_block_scalars(w_hbm, w_s, w_sems)
    n_acc = 4
    zero = jnp.zeros((ROW_SUB, LANES), F32)

    def token(t, carry):
        def chunk(c, accs):
            acc_lo, acc_hi = list(accs[:n_acc]), list(accs[n_acc:])
            base = slot_off + t * PEER_E + c * PEER_DOWN_CHUNK
            for j in range(PEER_DOWN_CHUNK):
                lo, hi = _unpack_row(_table_row(tab, idx_s[base + j]))
                w = w_s[base + j]
                acc_lo[j % n_acc] = acc_lo[j % n_acc] + w * lo
                acc_hi[j % n_acc] = acc_hi[j % n_acc] + w * hi
            return tuple(acc_lo) + tuple(acc_hi)

        accs = lax.fori_loop(0, PEER_E // PEER_DOWN_CHUNK, chunk, (zero,) * (2 * n_acc))
        lo = (accs[0] + accs[1]) + (accs[2] + accs[3])
        hi = (accs[4] + accs[5]) + (accs[6] + accs[7])
        o_ref[t] = x_ref[t] + jnp.concatenate([lo, hi], axis=0)
        return carry

    lax.fori_loop(0, PEER_TB, token, 0)


def _pack_table(tab):
    n = tab.shape[0]
    bits = lax.bitcast_convert_type(tab.astype(jnp.bfloat16), jnp.uint16).astype(jnp.uint32)
    words = bits[:, :ROW_WORDS] | (bits[:, ROW_WORDS:] << 16)
    return lax.bitcast_convert_type(words, jnp.int32).reshape(n * ROW_SUB, LANES)


def peer(x, g, wq, subkeys, u_tab, v_tab):
    n, d = x.shape
    assert n % PEER_TB == 0 and d == D_MODEL
    nb = n // PEER_TB
    sk = subkeys.reshape(PEER_HEADS * 2, N_KEYS, PEER_DK // 2).astype(BF16)
    params = pltpu.CompilerParams(dimension_semantics=("arbitrary",), vmem_limit_bytes=VMEM_LIMIT)
    h, idx, gates = pl.pallas_call(
        _peer_select_kernel,
        grid=(nb,),
        in_specs=[
            pl.BlockSpec((PEER_TB, d), lambda i: (i, 0)),
            pl.BlockSpec((1, d), lambda i: (0, 0)),
            pl.BlockSpec((d, PEER_HEADS * PEER_DK), lambda i: (0, 0)),
            pl.BlockSpec((PEER_HEADS * 2, N_KEYS, PEER_DK // 2), lambda i: (0, 0, 0)),
        ],
        out_specs=[
            pl.BlockSpec((PEER_TB, d), lambda i: (i, 0)),
            pl.BlockSpec((1, PEER_TB, PEER_E), lambda i: (i, 0, 0)),
            pl.BlockSpec((1, PEER_E, PEER_TB), lambda i: (i, 0, 0)),
        ],
        out_shape=[
            jax.ShapeDtypeStruct((n, d), F32),
            jax.ShapeDtypeStruct((nb, PEER_TB, PEER_E), jnp.int32),
            jax.ShapeDtypeStruct((nb, PEER_E, PEER_TB), F32),
        ],
        compiler_params=params,
        name="peer_select",
    )(x, g[None].astype(F32), wq.astype(BF16), sk)

    table_params = pltpu.CompilerParams(dimension_semantics=("arbitrary",),
                                        vmem_limit_bytes=PEER_TABLE_VMEM_LIMIT)
    tok_tiles = (PEER_TB, d // LANES, LANES)
    w = pl.pallas_call(
        _peer_up_kernel,
        grid=(nb,),
        in_specs=[
            pl.BlockSpec(memory_space=pl.ANY),
            pl.BlockSpec(memory_space=pl.ANY),
            pl.BlockSpec(tok_tiles, lambda i: (i, 0, 0)),
            pl.BlockSpec((1, PEER_E, PEER_TB), lambda i: (i, 0, 0)),
        ],
        out_specs=pl.BlockSpec((1, PEER_TB, PEER_E), lambda i: (i, 0, 0)),
        out_shape=jax.ShapeDtypeStruct((nb, PEER_TB, PEER_E), F32),
        scratch_shapes=[
            pltpu.VMEM((N_EXPERTS * ROW_SUB, LANES), jnp.int32),
            pltpu.SMEM((2 * PEER_TB * PEER_E,), jnp.int32),
            pltpu.VMEM((ROW_SUB * PSTRIDE, LANES), F32),
            pltpu.VMEM((ROW_SUB * PSTRIDE, LANES), F32),
            pltpu.VMEM((PEER_E, PEER_TB), F32),
            pltpu.SemaphoreType.DMA,
            pltpu.SemaphoreType.DMA((2,)),
        ],
        compiler_params=table_params,
        name="peer_up",
    )(idx.reshape(nb, PEER_TB * PEER_E), _pack_table(u_tab), h.reshape(n, d // LANES, LANES), gates)

    out = pl.pallas_call(
        _peer_down_kernel,
        grid=(nb,),
        in_specs=[
            pl.BlockSpec(memory_space=pl.ANY),
            pl.BlockSpec(memory_space=pl.ANY),
            pl.BlockSpec(memory_space=pl.ANY),
            pl.BlockSpec(tok_tiles, lambda i: (i, 0, 0)),
        ],
        out_specs=pl.BlockSpec(tok_tiles, lambda i: (i, 0, 0)),
        out_shape=jax.ShapeDtypeStruct((n, d // LANES, LANES), F32),
        scratch_shapes=[
            pltpu.VMEM((N_EXPERTS * ROW_SUB, LANES), jnp.int32),
            pltpu.SMEM((2 * PEER_TB * PEER_E,), jnp.int32),
            pltpu.SMEM((2 * PEER_TB * PEER_E,), F32),
            pltpu.SemaphoreType.DMA,
            pltpu.SemaphoreType.DMA((2,)),
            pltpu.SemaphoreType.DMA((2,)),
        ],
        compiler_params=table_params,
        name="peer_down",
    )(idx.reshape(nb, PEER_TB * PEER_E), w.reshape(nb, PEER_TB * PEER_E), _pack_table(v_tab),
      x.reshape(n, d // LANES, LANES))
    return out.reshape(n, d)


DSA_KT = 256
DSA_NEAR = 3
DSA_POS_BITS = 14
DSA_COUNT_TILES = 4
CHUNK_SHIFT = CHUNK.bit_length() - 1
INT_MIN = -2 ** 31
KEY_NEG_INF = 0x807FFFFF - 2 ** 32
LOG2_E = 1.4426950408889634
_NT = (((1,), (1,)), ((), ()))


def _sort_key(x):
    b = pltpu.bitcast(x, jnp.int32)
    return b ^ ((b >> 31) & 0x7FFFFFFF)


def _dsa_kernel(qpad_ref, qi_ref, wit_ref, k_ref, vt_ref, ki_ref, nb_ref, o_ref,
                keys_scr, s_scr, p_scr, *acc_refs, offset, n_keys, topk, qb):
    kt_ = DSA_KT
    q0 = offset + pl.program_id(1) * qb
    qpos = q0 + lax.broadcasted_iota(jnp.int32, (1, qb), 1)
    vis_end = jnp.minimum((lax.shift_right_logical(qpos, CHUNK_SHIFT) + 1) * CHUNK, n_keys)
    blk_end = jnp.minimum(((q0 + qb - 1) // CHUNK + 1) * CHUNK, n_keys)
    n_tiles = (blk_end + kt_ - 1) // kt_
    row = lax.broadcasted_iota(jnp.int32, (kt_, qb), 0)

    qi = qi_ref[0]
    qcat = jnp.concatenate([qi[:, h * D_IDX:(h + 1) * D_IDX] for h in range(H_I)], axis=0)
    wit = wit_ref[0] * (H_I ** -0.5)

    def score_tile(t, c):
        k0 = pl.multiple_of(t * kt_, kt_)
        kit = ki_ref[0, pl.ds(k0, kt_), :]
        scores = lax.dot_general(kit, qcat, _NT, preferred_element_type=F32)
        idx = jnp.zeros((kt_, qb), F32)
        for h in range(H_I):
            idx = idx + jnp.maximum(scores[:, h * qb:(h + 1) * qb] * (D_IDX ** -0.5), 0.0) * wit[h:h + 1]
        idx = jnp.where(idx == 0.0, 0.0, idx)
        keys_scr[pl.ds(k0, kt_), :] = jnp.where(row + k0 < vis_end, _sort_key(idx), KEY_NEG_INF)
        return c

    lax.fori_loop(0, n_tiles, score_tile, 0)

    n_steps = (n_tiles + DSA_COUNT_TILES - 1) // DSA_COUNT_TILES

    def pad_tile(t, c):
        keys_scr[pl.ds(pl.multiple_of(t * kt_, kt_), kt_), :] = jnp.full((kt_, qb), KEY_NEG_INF, jnp.int32)
        return c

    lax.fori_loop(n_tiles, n_steps * DSA_COUNT_TILES, pad_tile, 0)

    def count(preds, n_out):
        def body(s, accs):
            accs = list(accs)
            for u in range(DSA_COUNT_TILES):
                k0 = pl.multiple_of((s * DSA_COUNT_TILES + u) * kt_, kt_)
                hits = preds(keys_scr[pl.ds(k0, kt_), :], row + k0)
                for i in range(n_out):
                    accs[i] = accs[i] + jnp.sum(hits[i].reshape(kt_ // 8, 8, qb), axis=0)
            return tuple(accs)
        accs = lax.fori_loop(0, n_steps, body, (jnp.zeros((8, qb), F32),) * n_out)
        return [jnp.sum(acc, axis=0, keepdims=True) for acc in accs]

    one = lambda hit: jnp.where(hit, 1.0, 0.0)
    bit = lambda n: lax.shift_left(jnp.int32(1), n)
    kf = float(topk)
    c0, = count(lambda kt, kp: (one(kt >= 0),), 1)
    thr0 = jnp.where(c0 >= kf, 0, INT_MIN).astype(jnp.int32)

    def thr_bits(j, thr):
        hi, lo = bit(30 - 2 * j), bit(29 - 2 * j)
        c_hl, c_h, c_l = count(lambda kt, kp: (one(kt >= (thr | hi | lo)), one(kt >= (thr | hi)),
                                                one(kt >= (thr | lo))), 3)
        take_hi = c_h >= kf
        take_lo = jnp.where(take_hi, c_hl, c_l) >= kf
        return thr | jnp.where(take_hi, hi, 0) | jnp.where(take_lo, lo, 0)

    thr = lax.fori_loop(0, 15, thr_bits, thr0)
    c_last, = count(lambda kt, kp: (one(kt >= (thr | 1)),), 1)
    thr = jnp.where(c_last >= kf, thr | 1, thr)
    c_gt, c_eq = count(lambda kt, kp: (one(kt > thr), one(kt == thr)), 2)
    need = kf - c_gt
    select_all = vis_end <= topk

    def tie_cut():
        def pos_bits(j, lo_pos):
            hi, lo = bit(DSA_POS_BITS - 1 - 2 * j), bit(DSA_POS_BITS - 2 - 2 * j)
            tied = lambda kt, kp, bound: jnp.where(kt == thr, one(kp < bound), 0.0)
            f_hl, f_h, f_l = count(lambda kt, kp: (tied(kt, kp, lo_pos + hi + lo), tied(kt, kp, lo_pos + hi),
                                                    tied(kt, kp, lo_pos + lo)), 3)
            take_hi = f_h < need
            take_lo = jnp.where(take_hi, f_hl, f_l) < need
            return lo_pos + jnp.where(take_hi, hi, 0) + jnp.where(take_lo, lo, 0)
        return lax.fori_loop(0, DSA_POS_BITS // 2, pos_bits, jnp.zeros((1, qb), jnp.int32)) + 1

    surplus = jnp.max(jnp.where((c_eq > need) & jnp.logical_not(select_all), 1.0, 0.0)) > 0.0
    cut = lax.cond(surplus, tie_cut, lambda: jnp.full((1, qb), 2 ** DSA_POS_BITS, jnp.int32))
    thr = jnp.where(select_all, KEY_NEG_INF, thr)
    cut = jnp.where(select_all, 0, cut)

    for acc in acc_refs:
        acc[...] = jnp.zeros(acc.shape, F32)

    def attend_tile(t, carry):
        m_all, l_all = carry
        k0 = pl.multiple_of(t * kt_, kt_)
        keys = keys_scr[pl.ds(k0, kt_), :]
        sel = (keys - jnp.where(row + k0 < cut, 0, 1)) >= thr
        step = (k0 - q0 + (DSA_NEAR - 1) * LANES) // LANES
        nidx = jnp.where(step < 0, DSA_NEAR, step)
        for h in range(H_B):
            kh = k_ref[0, pl.ds(k0, kt_), (h // 2) * LANES:(h // 2 + 1) * LANES]
            qh = qpad_ref[0, :, h * LANES:(h + 1) * LANES]
            s_scr[h] = lax.dot_general(kh, qh, _NT, preferred_element_type=F32)
        m_rows, l_rows, alphas = [], [], []
        for h in range(H_B):
            s = jnp.where(sel, s_scr[h] * (DH_B ** -0.5 * LOG2_E) + nb_ref[nidx, h], NEG_INF)
            m_old = m_all[h:h + 1]
            m_new = jnp.maximum(m_old, jnp.max(s, axis=0, keepdims=True))
            m_safe = jnp.where(m_new == NEG_INF, 0.0, m_new)
            p = jnp.exp2(s - m_safe)
            alpha = jnp.exp2(m_old - m_safe)
            p_scr[h] = p.astype(BF16)
            m_rows.append(m_new)
            l_rows.append(alpha * l_all[h:h + 1] + jnp.sum(p, axis=0, keepdims=True))
            alphas.append(alpha)
        for h in range(H_B):
            vth = vt_ref[0, h * DH_B:(h + 1) * DH_B, pl.ds(k0, kt_)]
            acc_refs[h][...] = (alphas[h] * acc_refs[h][...]
                                + jnp.dot(vth, p_scr[h], preferred_element_type=F32))
        return jnp.concatenate(m_rows, axis=0), jnp.concatenate(l_rows, axis=0)

    _, l_all = lax.fori_loop(0, n_tiles, attend_tile,
                             (jnp.full((H_B, qb), NEG_INF, F32), jnp.zeros((H_B, qb), F32)))
    for h in range(H_B):
        o_ref[0, h * DH_B:(h + 1) * DH_B] = acc_refs[h][...] / l_all[h:h + 1]


def _near_bias_tiles(rel_bias, qb):
    k = jnp.arange(DSA_KT, dtype=jnp.int32)[:, None]
    q = jnp.arange(qb, dtype=jnp.int32)[None, :]
    rels = [(j - (DSA_NEAR - 1)) * LANES + k - q for j in range(DSA_NEAR)]
    rels.append(jnp.full((DSA_KT, qb), -(DSA_NEAR * LANES + DSA_KT), jnp.int32))
    tiles = rel_bias[_t5_bucket(jnp.stack(rels))]
    return jnp.moveaxis(tiles, -1, 1).astype(F32) * LOG2_E


def dsa_attention(qpad, k_all, v_all, qi, wi, ki_all, rel_bias, offset):
    bx, t, _ = qpad.shape
    n_keys = k_all.shape[1]
    topk = min(TOPK_MAX, n_keys // 4)
    qb = Q_BLOCK if t % Q_BLOCK == 0 else t
    lp = _round_up(n_keys, DSA_KT)
    assert lp <= 2 ** DSA_POS_BITS and offset % DSA_KT == 0 and (qb == Q_BLOCK or t == qb)
    pad = ((0, 0), (0, lp - n_keys), (0, 0))
    kk = jnp.pad(k_all, pad).astype(BF16)
    vt = jnp.pad(v_all, pad).astype(BF16).transpose(0, 2, 1)
    ki = jnp.pad(ki_all, pad).astype(BF16)
    wit = jnp.pad(wi.astype(F32).transpose(0, 2, 1), ((0, 0), (0, 8 - H_I), (0, 0)))
    nb = _near_bias_tiles(rel_bias, qb)
    kern = functools.partial(_dsa_kernel, offset=offset, n_keys=n_keys, topk=topk, qb=qb)
    ot = pl.pallas_call(
        kern,
        grid=(bx, t // qb),
        in_specs=[
            pl.BlockSpec((1, qb, H_B * LANES), lambda b, i: (b, i, 0)),
            pl.BlockSpec((1, qb, H_I * D_IDX), lambda b, i: (b, i, 0)),
            pl.BlockSpec((1, 8, qb), lambda b, i: (b, 0, i)),
            pl.BlockSpec((1, lp, D_B), lambda b, i: (b, 0, 0)),
            pl.BlockSpec((1, D_B, lp), lambda b, i: (b, 0, 0)),
            pl.BlockSpec((1, lp, D_IDX), lambda b, i: (b, 0, 0)),
            pl.BlockSpec((DSA_NEAR + 1, H_B, DSA_KT, qb), lambda b, i: (0, 0, 0, 0)),
        ],
        out_specs=pl.BlockSpec((1, D_B, qb), lambda b, i: (b, 0, i)),
        out_shape=jax.ShapeDtypeStruct((bx, D_B, t), F32),
        scratch_shapes=[pltpu.VMEM((_round_up(n_keys, DSA_KT * DSA_COUNT_TILES), qb), jnp.int32),
                        pltpu.VMEM((H_B, DSA_KT, qb), F32),
                        pltpu.VMEM((H_B, DSA_KT, qb), BF16)]
        + [pltpu.VMEM((DH_B, qb), F32)] * H_B,
        compiler_params=pltpu.CompilerParams(
            dimension_semantics=("arbitrary", "arbitrary"), vmem_limit_bytes=PEER_TABLE_VMEM_LIMIT),
        name="dsa_attention",
    )(qpad, qi.astype(BF16), wit, kk, vt, ki, nb)
    return ot.transpose(0, 2, 1)


WKV_CHUNK = 64
WKV_BLOCK = 256
_TN = (((0,), (0,)), ((), ()))


def _wkv_kernel(r_ref, lw_ref, k_ref, v_ref, kk_ref, a_ref, s0_ref, o_ref, st_ref, s_scr, *, chunk, n_chunks):
    c = chunk

    @pl.when(pl.program_id(1) == 0)
    def _():
        s_scr[...] = s0_ref[0]

    row = lax.broadcasted_iota(jnp.int32, (c, c), 0)
    col = lax.broadcasted_iota(jnp.int32, (c, c), 1)
    strict, incl = row > col, row >= col
    tri = jnp.where(incl, 1.0, 0.0)
    eye = jnp.where(row == col, 1.0, 0.0)
    bdot = lambda x, y: jnp.dot(x.astype(BF16), y.astype(BF16), preferred_element_type=F32)
    bdot_nt = lambda x, y: lax.dot_general(x.astype(BF16), y.astype(BF16), _NT, preferred_element_type=F32)
    bdot_tn = lambda x, y: lax.dot_general(x.astype(BF16), y.astype(BF16), _TN, preferred_element_type=F32)

    def solve_chunk(ci, carry):
        c0 = pl.multiple_of(ci * c, c)
        rows = pl.ds(c0, c)
        heads = range(H_A)
        v, kap_t, r_t, k_b, b_b, decay_all, grams = [], [], [], [], [], [], []
        for h in heads:
            hs = slice(h * DH_A, (h + 1) * DH_A)
            r, lw, k = r_ref[0, rows, hs], lw_ref[0, rows, hs], k_ref[0, rows, hs]
            kap, a = kk_ref[0, rows, hs], a_ref[0, rows, hs]
            cum = jnp.dot(tri, lw, preferred_element_type=F32, precision=lax.Precision.HIGHEST)
            p, p_inv, p_prev = jnp.exp(cum), jnp.exp(-cum), jnp.exp(cum - lw)
            v.append(v_ref[0, rows, hs])
            kap_t.append(kap * p_prev)
            r_t.append(r * p)
            k_b.append(k * p_inv)
            b_b.append(kap * a * p_inv)
            decay_all.append(p[c - 1:c, :])
            grams.append(bdot_nt(jnp.concatenate([kap_t[h], r_t[h]], axis=0),
                                 jnp.concatenate([k_b[h], b_b[h]], axis=0)))
        power = [jnp.where(strict, -g[:c, c:], 0.0) for g in grams]
        inv = [eye + n for n in power]
        for _ in range(c.bit_length() - 2):
            power = [bdot(n, n) for n in power]
            inv = [x + bdot(x, n) for x, n in zip(inv, power)]
        s_prev = [s_scr[h] for h in heads]
        rhs = [bdot_nt(kap_t[h], s_prev[h]) + bdot(jnp.where(strict, grams[h][:c, :c], 0.0), v[h]) for h in heads]
        u = [bdot(inv[h], rhs[h]) for h in heads]
        outs = [bdot_nt(r_t[h], s_prev[h]) + bdot(jnp.where(incl, grams[h][c:, :c], 0.0), v[h])
                - bdot(jnp.where(incl, grams[h][c:, c:], 0.0), u[h]) for h in heads]
        for h in heads:
            s_scr[h] = (s_prev[h] + bdot_tn(v[h], k_b[h]) - bdot_tn(u[h], b_b[h])) * decay_all[h]
        o_ref[0, rows, :] = jnp.concatenate(outs, axis=1)
        return carry

    lax.fori_loop(0, n_chunks, solve_chunk, 0)

    @pl.when(pl.program_id(1) == pl.num_programs(1) - 1)
    def _():
        st_ref[0] = s_scr[...]


def wkv_chunked(r, lw, k, v, kk, a, s0):
    bx, t, _ = r.shape
    chunk = min(WKV_CHUNK, t)
    blk = min(WKV_BLOCK, t)
    assert t % blk == 0 and blk % chunk == 0 and chunk & (chunk - 1) == 0
    tok = pl.BlockSpec((1, blk, D_A), lambda b, i: (b, i, 0))
    state = pl.BlockSpec((1, H_A, DH_A, DH_A), lambda b, i: (b, 0, 0, 0))
    return pl.pallas_call(
        functools.partial(_wkv_kernel, chunk=chunk, n_chunks=blk // chunk),
        grid=(bx, t // blk),
        in_specs=[tok] * 6 + [state],
        out_specs=[tok, state],
        out_shape=[jax.ShapeDtypeStruct((bx, t, D_A), F32),
                   jax.ShapeDtypeStruct((bx, H_A, DH_A, DH_A), F32)],
        scratch_shapes=[pltpu.VMEM((H_A, DH_A, DH_A), F32)],
        compiler_params=pltpu.CompilerParams(
            dimension_semantics=("arbitrary", "arbitrary"), vmem_limit_bytes=VMEM_LIMIT),
        name="wkv_chunked",
    )(r, lw, k, v, kk, a, s0.astype(F32))


CONV_HALO = 32
CONV_BLOCK = 256


def _conv_kernel(u_ref, prev_ref, w_ref, b_ref, g_ref, beta_ref, y_ref, state_ref, pad_scr, *, blk):
    lead = CONV_HALO - (CONV_W - 1)

    @pl.when(pl.program_id(1) == 0)
    def _():
        pad_scr[0:lead, :] = jnp.zeros((lead, D_CONV), F32)
        pad_scr[lead:CONV_HALO, :] = prev_ref[0]

    @pl.when(pl.program_id(1) > 0)
    def _():
        pad_scr[0:CONV_HALO, :] = pad_scr[blk:blk + CONV_HALO, :]

    u = u_ref[0]
    pad_scr[CONV_HALO:CONV_HALO + blk, :] = u[:, :D_CONV] * jax.nn.sigmoid(u[:, D_CONV:])
    y = jnp.zeros((blk, D_CONV), F32) + b_ref[...]
    for j in range(CONV_W):
        y = y + w_ref[j:j + 1, :] * pad_scr[lead + j:lead + j + blk, :]
    m = jnp.mean(y, axis=-1, keepdims=True)
    var = jnp.mean(jnp.square(y - m), axis=-1, keepdims=True)
    z = (y - m) * lax.rsqrt(var + 1e-5) * g_ref[...] + beta_ref[...]
    y_ref[0] = z * jax.nn.sigmoid(z)

    @pl.when(pl.program_id(1) == pl.num_programs(1) - 1)
    def _():
        state_ref[0] = pad_scr[blk + lead:blk + CONV_HALO, :]


def conv_module(u, conv_prev, w_dw, b_dw, ln_g, ln_b):
    bx, t, _ = u.shape
    blk = min(CONV_BLOCK, t)
    assert t % blk == 0 and blk % 8 == 0
    vec = pl.BlockSpec((1, D_CONV), lambda b, i: (0, 0))
    return pl.pallas_call(
        functools.partial(_conv_kernel, blk=blk),
        grid=(bx, t // blk),
        in_specs=[
            pl.BlockSpec((1, blk, 2 * D_CONV), lambda b, i: (b, i, 0)),
            pl.BlockSpec((1, CONV_W - 1, D_CONV), lambda b, i: (b, 0, 0)),
            pl.BlockSpec((CONV_W, D_CONV), lambda b, i: (0, 0)),
            vec, vec, vec,
        ],
        out_specs=[
            pl.BlockSpec((1, blk, D_CONV), lambda b, i: (b, i, 0)),
            pl.BlockSpec((1, CONV_W - 1, D_CONV), lambda b, i: (b, 0, 0)),
        ],
        out_shape=[jax.ShapeDtypeStruct((bx, t, D_CONV), F32),
                   jax.ShapeDtypeStruct((bx, CONV_W - 1, D_CONV), F32)],
        scratch_shapes=[pltpu.VMEM((CONV_HALO + blk, D_CONV), F32)],
        compiler_params=pltpu.CompilerParams(
            dimension_semantics=("arbitrary", "arbitrary"), vmem_limit_bytes=VMEM_LIMIT),
        name="conv_module",
    )(u, conv_prev.astype(F32), w_dw.astype(F32), b_dw[None].astype(F32), ln_g[None].astype(F32),
      ln_b[None].astype(F32))


EVEN_BLOCK = 256
_HI = lax.Precision.HIGHEST


def _head_sum(x, hs):
    return jnp.dot(x, hs, preferred_element_type=F32, precision=_HI)


def _even_prep_kernel(p_ref, shift_ref, mu_ref, w0_ref, a0_ref, wdu_ref, wau_ref, wgu_ref, kk_ref, ka_ref,
                      qn_ref, kn_ref, hs_ref,
                      r_o, lw_o, k_o, v_o, kap_o, a_o, gate_o, qpad_o, kn_o, last_scr, *, blk):
    @pl.when(pl.program_id(1) == 0)
    def _():
        last_scr[0:1, :] = shift_ref[0]

    p = p_ref[0]
    pa = p[:, :A_PROJ]
    first = lax.broadcasted_iota(jnp.int32, pa.shape, 0) == 0
    prev = jnp.where(first, last_scr[0:1, :], pltpu.roll(pa, 1, axis=0))
    last_scr[0:1, :] = pa[blk - 1:blk, :]
    xm = pa + (prev - pa) * mu_ref[...]
    r, k, v = xm[:, :D_A], xm[:, D_A:2 * D_A], xm[:, 2 * D_A:3 * D_A]
    c = 3 * D_A
    dw, da, dg = xm[:, c:c + LORA_W], xm[:, c + LORA_W:c + LORA_W + LORA_A], xm[:, c + LORA_W + LORA_A:A_PROJ]
    bdot = lambda x, w: jnp.dot(x.astype(BF16), w[...], preferred_element_type=F32)
    z = -(w0_ref[...] + bdot(jnp.tanh(dw), wdu_ref))
    w_log = -(jnp.maximum(z, 0.0) + jnp.log(1.0 + jnp.exp(-jnp.abs(z)))) - 0.5
    a = jax.nn.sigmoid(a0_ref[...] + bdot(da, wau_ref))
    kap = k * kk_ref[...]
    hs = hs_ref[...]
    kap = kap / jnp.maximum(jnp.sqrt(_head_sum(kap * kap, hs)), 1e-12)
    r_o[0] = r
    lw_o[0] = -jnp.exp(w_log)
    k_o[0] = k * (1.0 + (a - 1.0) * ka_ref[...])
    v_o[0] = v
    kap_o[0] = kap
    a_o[0] = a
    gate_o[0] = bdot(jax.nn.sigmoid(dg), wgu_ref)

    q = p[:, A_PROJ:A_PROJ + D_B]
    kb = p[:, A_PROJ + D_B:A_PROJ + 2 * D_B]
    qn = q * lax.rsqrt(_head_sum(q * q, hs) * (1.0 / DH_B) + NORM_EPS) * qn_ref[...]
    kn_o[0] = kb * lax.rsqrt(_head_sum(kb * kb, hs) * (1.0 / DH_B) + NORM_EPS) * kn_ref[...]
    upper = lax.broadcasted_iota(jnp.int32, (blk, LANES), 1) >= DH_B
    chunks = []
    for h in range(H_B):
        pair = qn[:, (h // 2) * LANES:(h // 2 + 1) * LANES]
        chunks.append(jnp.where(upper == (h % 2 == 1), pair, 0.0))
    qpad_o[0] = jnp.concatenate(chunks, axis=1).astype(BF16)


def even_prep(p, shift_prev, mu, w0, w_du, a0, w_au, w_gu, k_k, k_a, qn_g, kn_g):
    bx, t, width = p.shape
    blk = min(EVEN_BLOCK, t)
    assert t % blk == 0
    head = jnp.arange(D_A) // DH_A
    hs = (head[:, None] == head[None, :]).astype(F32)
    row = lambda v: v.reshape(1, -1).astype(F32)
    full = lambda a: pl.BlockSpec(a.shape, lambda b, i: (0,) * a.ndim)
    consts = (row(mu), row(w0), row(a0), w_du.astype(BF16), w_au.astype(BF16), w_gu.astype(BF16),
              row(k_k), row(k_a), row(jnp.tile(qn_g, H_B)), row(jnp.tile(kn_g, H_B)), hs)
    tok = pl.BlockSpec((1, blk, D_A), lambda b, i: (b, i, 0))
    return pl.pallas_call(
        functools.partial(_even_prep_kernel, blk=blk),
        grid=(bx, t // blk),
        in_specs=[pl.BlockSpec((1, blk, width), lambda b, i: (b, i, 0)),
                  pl.BlockSpec((1, 1, A_PROJ), lambda b, i: (b, 0, 0))] + [full(c) for c in consts],
        out_specs=[tok] * 7 + [pl.BlockSpec((1, blk, H_B * LANES), lambda b, i: (b, i, 0)), tok],
        out_shape=[jax.ShapeDtypeStruct((bx, t, D_A), F32)] * 7
        + [jax.ShapeDtypeStruct((bx, t, H_B * LANES), BF16), jax.ShapeDtypeStruct((bx, t, D_B), F32)],
        scratch_shapes=[pltpu.VMEM((8, A_PROJ), F32)],
        compiler_params=pltpu.CompilerParams(
            dimension_semantics=("arbitrary", "arbitrary"), vmem_limit_bytes=VMEM_LIMIT),
        name="even_prep",
    )(p, shift_prev[:, None, :].astype(F32), *consts)


def _rwkv_post_kernel(o_ref, r_ref, k_ref, v_ref, gate_ref, rk_ref, gw_ref, gb_ref, hs_ref, out_ref):
    hs = hs_ref[...]
    o = o_ref[...]
    d = o - _head_sum(o, hs) * (1.0 / DH_A)
    var = _head_sum(d * d, hs) * (1.0 / DH_A)
    on = d * lax.rsqrt(var + GN_EPS) * gw_ref[...] + gb_ref[...]
    bonus = _head_sum(r_ref[...] * k_ref[...] * rk_ref[...], hs) * v_ref[...]
    out_ref[...] = (on + bonus) * gate_ref[...]


def rwkv_post(o, r, k, v, gate, r_k, gn_w, gn_b):
    n = o.shape[0]
    blk = min(512, n)
    assert n % blk == 0
    head = jnp.arange(D_A) // DH_A
    hs = (head[:, None] == head[None, :]).astype(F32)
    row = lambda v: v.reshape(1, -1).astype(F32)
    tok = pl.BlockSpec((blk, D_A), lambda i: (i, 0))
    vec = pl.BlockSpec((1, D_A), lambda i: (0, 0))
    return pl.pallas_call(
        _rwkv_post_kernel,
        grid=(n // blk,),
        in_specs=[tok] * 5 + [vec] * 3 + [pl.BlockSpec((D_A, D_A), lambda i: (0, 0))],
        out_specs=tok,
        out_shape=jax.ShapeDtypeStruct((n, D_A), F32),
        compiler_params=pltpu.CompilerParams(
            dimension_semantics=("arbitrary",), vmem_limit_bytes=VMEM_LIMIT),
        name="rwkv_post",
    )(o, r, k, v, gate, row(r_k), row(gn_w), row(gn_b), hs)


def _linear2_residual_kernel(xa_ref, xb_ref, wa_ref, wb_ref, r_ref, o_ref):
    o_ref[...] = (r_ref[...]
                  + jnp.dot(xa_ref[...].astype(BF16), wa_ref[...], preferred_element_type=F32)
                  + jnp.dot(xb_ref[...].astype(BF16), wb_ref[...], preferred_element_type=F32))


def linear2_residual(xa, xb, w, res, row_tile=512):
    n, ka = xa.shape
    kb = xb.shape[1]
    m = w.shape[1]
    tm = min(row_tile, n)
    assert n % tm == 0 and w.shape[0] == ka + kb
    wb16 = w.astype(BF16)
    return pl.pallas_call(
        _linear2_residual_kernel,
        grid=(n // tm,),
        in_specs=[
            pl.BlockSpec((tm, ka), lambda i: (i, 0)),
            pl.BlockSpec((tm, kb), lambda i: (i, 0)),
            pl.BlockSpec((ka, m), lambda i: (0, 0)),
            pl.BlockSpec((kb, m), lambda i: (0, 0)),
            pl.BlockSpec((tm, m), lambda i: (i, 0)),
        ],
        out_specs=pl.BlockSpec((tm, m), lambda i: (i, 0)),
        out_shape=jax.ShapeDtypeStruct((n, m), F32),
        compiler_params=pltpu.CompilerParams(
            dimension_semantics=("arbitrary",), vmem_limit_bytes=VMEM_LIMIT),
        name="linear2_residual",
    )(xa, xb, wb16[:ka], wb16[ka:], res)


def _t5_bucket(rel):
    nb = REL_BUCKETS // 2
    ret = jnp.where(rel > 0, nb, 0)
    n = jnp.abs(rel)
    max_exact = nb // 2
    nf = jnp.maximum(n, 1).astype(F32)
    large = max_exact + (jnp.log(nf / max_exact) / math.log(REL_MAX_DIST / max_exact)
                         * (nb - max_exact)).astype(jnp.int32)
    large = jnp.minimum(large, nb - 1)
    return ret + jnp.where(n < max_exact, n, large)


def _even_mixer(x, shift_prev, wkv_prev, k_past, v_past, kidx_past,
                g_mix, w_in, mu, w0, w_du, a0, w_au, w_gu, k_k, k_a, r_k, gn_w, gn_b,
                qn_g, kn_g, rel_bias, w_out):
    Bx, T, _ = x.shape
    n = Bx * T
    p = norm_linear(x.reshape(n, D_MODEL), g_mix, w_in, keep_padding=True).reshape(Bx, T, -1)
    r, log_decay, k, v, kk, a, gate, qpad, kb = even_prep(
        p, shift_prev, mu, w0, w_du, a0, w_au, w_gu, k_k, k_a, qn_g, kn_g)
    o, wkv_new = wkv_chunked(r, log_decay, k, v, kk, a, wkv_prev)
    rows = lambda t: t.reshape(n, D_A)
    o_a = rwkv_post(rows(o), rows(r), rows(k), rows(v), rows(gate), r_k, gn_w, gn_b)
    c = A_PROJ + 2 * D_B
    vb = p[..., c:c + D_B]
    qi = p[..., c + D_B:c + D_B + H_I * D_IDX]
    ki = p[..., c + D_B + H_I * D_IDX:c + D_B + H_I * D_IDX + D_IDX]
    wi = p[..., c + D_B + H_I * D_IDX + D_IDX:A_PROJ + B_PROJ]
    offset = k_past.shape[1]
    k_all = jnp.concatenate([k_past.reshape(Bx, offset, D_B).astype(F32), kb], axis=1)
    v_all = jnp.concatenate([v_past.reshape(Bx, offset, D_B).astype(F32), vb], axis=1)
    ki_all = jnp.concatenate([kidx_past.astype(F32), ki], axis=1)
    o_b = dsa_attention(qpad, k_all, v_all, qi, wi, ki_all, rel_bias, offset)
    x_new = linear2_residual(o_a, o_b.reshape(n, D_B), w_out, x.reshape(n, D_MODEL)).reshape(Bx, T, D_MODEL)
    heads = lambda t: t.reshape(Bx, T, H_B, DH_B)
    return x_new, p[:, -1, :A_PROJ], wkv_new, heads(kb), heads(vb), ki


def _conv_mixer(x, conv_prev, g_mix, w1, b1, w_dw, b_dw, ln_g, ln_b, w2, b2):
    Bx, T, _ = x.shape
    u = norm_linear(x.reshape(Bx * T, D_MODEL), g_mix, w1, b1).reshape(Bx, T, -1)
    yn, conv_new = conv_module(u, conv_prev, w_dw, b_dw, ln_g, ln_b)
    x_new = linear_residual(yn.reshape(Bx * T, D_CONV), w2, b2,
                            x.reshape(Bx * T, D_MODEL)).reshape(Bx, T, D_MODEL)
    return x_new, conv_new


def _peer_both(xp, xs, g, wq, subkeys, u_tab, v_tab):
    np_ = xp.shape[0] * xp.shape[1]
    rows = jnp.concatenate([xp.reshape(np_, D_MODEL), xs.reshape(-1, D_MODEL)], axis=0)
    out = peer(rows, g, wq, subkeys, u_tab, v_tab)
    return out[:np_].reshape(xp.shape), out[np_:].reshape(xs.shape)


def kernel(x_prompt, x_sample, state_wkv, state_shift, cache_k, cache_v, cache_kidx, state_conv,
           norm_mix, norm_ffn, w_in, mu_shift, w0, w_decay_up, a0, w_iclr_up, w_gate_up,
           k_k, k_a, r_k, gn_w, gn_b, qn_g, kn_g, rel_bias, w_out,
           conv_w1, conv_b1, conv_dw, conv_bdw, conv_ln_g, conv_ln_b, conv_w2, conv_b2,
           peer_wq, peer_subkeys, peer_u, peer_v):
    xp, xs = x_prompt, x_sample
    Bp = xp.shape[0]
    dt = xp.dtype
    wkv_p, shift_p, k_p, v_p, kidx_p, conv_p = [], [], [], [], [], []
    wkv_s, shift_s, k_s, v_s, kidx_s, conv_s = [], [], [], [], [], []
    for li in range(DEPTH):
        if li % 2 == 0:
            e = li // 2
            prm = (norm_mix[li], w_in[e], mu_shift[e], w0[e], w_decay_up[e], a0[e], w_iclr_up[e],
                   w_gate_up[e], k_k[e], k_a[e], r_k[e], gn_w[e], gn_b[e], qn_g[e], kn_g[e],
                   rel_bias, w_out[e])
            xp, sh, wk, kb, vb, ki = _even_mixer(
                xp, jnp.zeros((Bp, A_PROJ), dt), jnp.zeros((Bp, H_A, DH_A, DH_A), dt),
                jnp.zeros((Bp, 0, H_B, DH_B), dt), jnp.zeros((Bp, 0, H_B, DH_B), dt),
                jnp.zeros((Bp, 0, D_IDX), dt), *prm)
            wkv_p.append(wk); shift_p.append(sh); k_p.append(kb); v_p.append(vb); kidx_p.append(ki)
            xs, sh, wk, kb, vb, ki = _even_mixer(
                xs, state_shift[e], state_wkv[e], cache_k[e], cache_v[e], cache_kidx[e], *prm)
            wkv_s.append(wk); shift_s.append(sh); k_s.append(kb); v_s.append(vb); kidx_s.append(ki)
        else:
            o = li // 2
            prm = (norm_mix[li], conv_w1[o], conv_b1[o], conv_dw[o], conv_bdw[o],
                   conv_ln_g[o], conv_ln_b[o], conv_w2[o], conv_b2[o])
            xp, cp = _conv_mixer(xp, jnp.zeros((Bp, CONV_W - 1, D_CONV), dt), *prm)
            xs, cs = _conv_mixer(xs, state_conv[o], *prm)
            conv_p.append(cp); conv_s.append(cs)
        pprm = (norm_ffn[li], peer_wq[li], peer_subkeys[li], peer_u[li], peer_v[li])
        xp, xs = _peer_both(xp, xs, *pprm)
    return (xp, xs,
            jnp.stack(wkv_p), jnp.stack(shift_p), jnp.stack(k_p), jnp.stack(v_p), jnp.stack(kidx_p), jnp.stack(conv_p),
            jnp.stack(wkv_s), jnp.stack(shift_s), jnp.stack(k_s), jnp.stack(v_s), jnp.stack(kidx_s), jnp.stack(conv_s))
```
